```python
import math
import jax
import jax.numpy as jnp
from jax import lax
import numpy as np

D_MODEL = 2048
BATCH = 4
SEQ = 4096
DEPTH = 2

GRID_W = 64
CTX_LEN = 256
D_MIX = D_MODEL
HY_W = D_MIX // 4
RW_W = D_MIX // 4
GD_W = D_MIX // 4
RG_W = D_MIX - HY_W - RW_W - GD_W
D_FF = 4 * D_MODEL
NORM_EPS = 1e-6

HY_SHORT = 3
HY_EMB = 33
HY_BANDS = (HY_EMB - 1) // 2
HY_FILT = 64
HY_DECAY_TARGET = 1e-2
HY_FAST_PCT = 0.3
HY_SLOW_PCT = 1.5
HY_IN = 3 * HY_W

RW_HEAD = 64
RW_HEADS = RW_W // RW_HEAD
RW_LORA_W = 64
RW_LORA_A = 64
RW_LORA_G = 128
RW_GN_EPS = 64e-5
RW_IN = 3 * RW_W + RW_LORA_W + RW_LORA_A + RW_LORA_G
RW_SPLITS = (RW_W, 2 * RW_W, 3 * RW_W, 3 * RW_W + RW_LORA_W, 3 * RW_W + RW_LORA_W + RW_LORA_A)

GD_HEAD = 128
GD_HEADS = GD_W // GD_HEAD
GD_CHUNK = 64
GD_IN = 4 * GD_W + 4 * GD_HEADS

RG_BLOCKS = 4
RG_BLOCK = RG_W // RG_BLOCKS
RG_C = 8.0
RG_IN = 2 * RG_W

SHORT_CONV = 4
SHORT_PAD = 2

N_IN = HY_IN + RW_IN + GD_IN + RG_IN
IN_SPLITS = (HY_IN, HY_IN + RW_IN, HY_IN + RW_IN + GD_IN)

kernel_name = 'hybrid_parallel_mixer_dit'


def _rmsnorm(x, g):
    xf = x.astype(jnp.float32)
    y = xf * lax.rsqrt(jnp.mean(xf * xf, axis=-1, keepdims=True) + NORM_EPS)
    return (y * g.astype(jnp.float32)).astype(x.dtype)


def _modulation(cond, ada_w, ada_b):
    return jnp.split(jax.nn.silu(cond) @ ada_w + ada_b, 6, axis=-1)


def _modulate(h, shift, scale):
    return h * (1 + scale) + shift


def _flip_if(t, d):
    return t if d == 0 else jnp.flip(t, axis=1)


def _short_conv(u, w, n_rows, row_len, pad_left):
    bsz, _, ch = u.shape
    width = w.shape[0]
    ur = u.reshape(bsz, n_rows, row_len, ch)
    up = jnp.pad(ur, ((0, 0), (0, 0), (pad_left, width - 1 - pad_left), (0, 0)))
    y = up[:, :, 0:row_len] * w[0]
    for j in range(1, width):
        y = y + up[:, :, j:j + row_len] * w[j]
    return y.reshape(bsz, n_rows * row_len, ch)


def _mlp(h, w1, w2):
    return jnp.square(jax.nn.relu(h @ w1)) @ w2


def _hyena_filter(length, w1, b1, w2, b2, w3, freq):
    t = jnp.arange(length, dtype=jnp.float32)
    z = t / max(length - 1, 1)
    bands = jnp.linspace(1e-4, HY_BANDS - 1, HY_BANDS, dtype=jnp.float32)
    ang = (2.0 * math.pi / length) * t[:, None] * bands[None, :]
    feat = jnp.concatenate([z[:, None], jnp.cos(ang), -jnp.sin(ang)], axis=-1)
    h = jnp.sin(freq[0] * (feat @ w1 + b1))
    h = jnp.sin(freq[1] * (h @ w2 + b2))
    h = (h @ w3).astype(jnp.float32)
    deltas = jnp.abs(jnp.linspace(math.log(HY_DECAY_TARGET) / HY_SLOW_PCT,
                                  math.log(HY_DECAY_TARGET) / HY_FAST_PCT, HY_W, dtype=jnp.float32))
    h = h * jnp.exp(-z[:, None] * jnp.tile(deltas, 2)[None, :])
    h_fwd, h_bwd = h[:, :HY_W], h[:, HY_W:]
    filt = jnp.concatenate([h_fwd, jnp.zeros((1, HY_W), jnp.float32), h_bwd[:0:-1]], axis=0)
    return filt / jnp.sum(jnp.abs(filt), axis=0, keepdims=True)


def _hyena_mixer(p, n_rows, row_len, conv_w, w1, b1, w2, b2, w3, freq, skip):
    length = p.shape[1]
    u = _short_conv(p, conv_w, n_rows, row_len, 1)
    v, x0, x1 = jnp.split(u.astype(jnp.float32), 3, axis=-1)
    filt = _hyena_filter(length, w1, b1, w2, b2, w3, freq)
    zin = x1 * v
    zf = jnp.fft.rfft(zin, n=2 * length, axis=1)
    y = jnp.fft.irfft(zf * jnp.fft.rfft(filt, axis=0)[None], n=2 * length, axis=1)[:, :length]
    y = y + zin * skip
    return (x0 * y).astype(p.dtype)


def _rwkv7_direction(p, s0, mu, w0, w_up, a0, a_up, g_up, k_k, k_a, r_k, ln_g, ln_b, with_out):
    bsz, length, _ = p.shape
    p = p.astype(jnp.float32)
    prev = jnp.pad(p, ((0, 0), (1, 0), (0, 0)))[:, :-1]
    pm = p + (prev - p) * mu
    r, k, v, xw, xa, xg = jnp.split(pm, RW_SPLITS, axis=-1)
    decay = jnp.exp(-jnp.exp(-jax.nn.softplus(-(w0 + jnp.tanh(xw) @ w_up)) - 0.5))
    a = jax.nn.sigmoid(a0 + xa @ a_up)
    kk = (k * k_k).reshape(bsz, length, RW_HEADS, RW_HEAD)
    kk = kk / jnp.maximum(jnp.linalg.norm(kk, axis=-1, keepdims=True), 1e-12)
    k = k * (1 + (a - 1) * k_a)

    def heads(t):
        return jnp.moveaxis(t.reshape(bsz, length, RW_HEADS, RW_HEAD), 1, 0)

    seqs = [heads(decay), heads(k), heads(v), jnp.moveaxis(kk, 1, 0), heads(a)]
    if with_out:
        seqs.append(heads(r))

    def step(state, inp):
        w_t, k_t, v_t, kk_t, a_t = inp[:5]
        sk = jnp.einsum('bhvk,bhk->bhv', state, kk_t)
        state = (state * w_t[:, :, None, :] - sk[..., None] * (kk_t * a_t)[:, :, None, :]
                 + v_t[..., None] * k_t[:, :, None, :])
        y = jnp.einsum('bhvk,bhk->bhv', state, inp[5]) if with_out else None
        return state, y

    s_fin, ys = lax.scan(step, s0, tuple(seqs))
    if not with_out:
        return None, s_fin
    y = jnp.moveaxis(ys, 0, 1)
    mean = jnp.mean(y, axis=-1, keepdims=True)
    var = jnp.mean(jnp.square(y - mean), axis=-1, keepdims=True)
    yn = ((y - mean) * lax.rsqrt(var + RW_GN_EPS)).reshape(bsz, length, RW_W) * ln_g + ln_b
    bonus = (jnp.sum((r * k * r_k).reshape(bsz, length, RW_HEADS, RW_HEAD), axis=-1, keepdims=True)
             * v.reshape(bsz, length, RW_HEADS, RW_HEAD)).reshape(bsz, length, RW_W)
    g = jax.nn.sigmoid(xg) @ g_up
    return (yn + bonus) * g, s_fin


def _rwkv7_mixer(p, s0s, prm, with_out):
    bsz = p.shape[0]
    outs, states = [], []
    for d in range(2):
        s0 = jnp.zeros((bsz, RW_HEADS, RW_HEAD, RW_HEAD), jnp.float32) if s0s is None else s0s[d]
        o, s = _rwkv7_direction(_flip_if(p, d), s0, *[t[d] for t in prm], with_out)
        outs.append(o)
        states.append(s)
    if not with_out:
        return None, states
    return (outs[0] + jnp.flip(outs[1], axis=1)).astype(p.dtype), states


def _l2norm(t):
    return t * lax.rsqrt(jnp.sum(t * t, axis=-1, keepdims=True) + 1e-6)


def _gdn_direction(q, k, v, g_raw, b_raw, s0, a_log, dt_bias, with_out):
    bsz, length, nh, _ = q.shape
    nc = length // GD_CHUNK
    g = -jnp.exp(a_log) * jax.nn.softplus(g_raw + dt_bias)
    beta = jax.nn.sigmoid(b_raw)

    def chunks(t):
        t = t.reshape((bsz, nc, GD_CHUNK, nh) + t.shape[3:])
        return jnp.moveaxis(jnp.moveaxis(t, 3, 2), 1, 0)

    qc, kc, vc = chunks(q), chunks(k), chunks(v)
    gc = jnp.cumsum(chunks(g), axis=-1)
    bc = chunks(beta)[..., None]
    idx = jnp.arange(GD_CHUNK)
    lower = idx[:, None] >= idx[None, :]
    decay_in = jnp.exp(jnp.where(lower, gc[..., :, None] - gc[..., None, :], -jnp.inf))
    decay_strict = jnp.where(idx[:, None] > idx[None, :], decay_in, 0.0)
    kb = kc * bc
    tmat = jnp.einsum('nbhik,nbhjk->nbhij', kb, kc) * decay_strict + jnp.eye(GD_CHUNK, dtype=jnp.float32)
    rhs = jnp.concatenate([vc * bc, kb * jnp.exp(gc)[..., None]], axis=-1)
    sol = lax.linalg.triangular_solve(tmat, rhs, left_side=True, lower=True, unit_diagonal=True)
    u, w = sol[..., :GD_HEAD], sol[..., GD_HEAD:]
    g_last = gc[..., -1]
    k_dec = kc * jnp.exp(g_last[..., None] - gc)[..., None]
    seqs = [u, w, k_dec, g_last]
    if with_out:
        seqs += [qc * jnp.exp(gc)[..., None], jnp.einsum('nbhik,nbhjk->nbhij', qc, kc) * decay_in]

    def step(state, inp):
        u_c, w_c, kd_c, gl_c = inp[:4]
        v_new = u_c - w_c @ state
        new_state = state * jnp.exp(gl_c)[..., None, None] + jnp.einsum('bhck,bhcv->bhkv', kd_c, v_new)
        o = (inp[4] @ state + inp[5] @ v_new) if with_out else None
        return new_state, o

    s_fin, o = lax.scan(step, s0, tuple(seqs))
    if not with_out:
        return None, s_fin
    o = jnp.swapaxes(jnp.moveaxis(o, 0, 1), 2, 3).reshape(bsz, length, nh, GD_HEAD)
    return o, s_fin


def _gdn_mixer(p, n_rows, row_len, s0s, conv_w, a_log, dt_bias, norm_g, with_out):
    bsz, length, _ = p.shape
    qkv = jax.nn.silu(_short_conv(p[..., :3 * GD_W], conv_w, n_rows, row_len, SHORT_PAD)).astype(jnp.float32)
    q, k, v = [t.reshape(bsz, length, GD_HEADS, GD_HEAD) for t in jnp.split(qkv, 3, axis=-1)]
    q = _l2norm(q) * GD_HEAD ** -0.5
    k = _l2norm(k)
    gb = p[..., 4 * GD_W:].astype(jnp.float32).reshape(bsz, length, 4, GD_HEADS)
    outs, states = [], []
    for d in range(2):
        s0 = jnp.zeros((bsz, GD_HEADS, GD_HEAD, GD_HEAD), jnp.float32) if s0s is None else s0s[d]
        o, s = _gdn_direction(_flip_if(q, d), _flip_if(k, d), _flip_if(v, d), _flip_if(gb[:, :, d], d),
                              _flip_if(gb[:, :, 2 + d], d), s0, a_log[d], dt_bias[d], with_out)
        outs.append(o)
        states.append(s)
    if not with_out:
        return None, states
    o = outs[0] + jnp.flip(outs[1], axis=1)
    o = o * lax.rsqrt(jnp.mean(o * o, axis=-1, keepdims=True) + NORM_EPS) * norm_g
    out = o.reshape(bsz, length, GD_W) * jax.nn.silu(p[..., 3 * GD_W:4 * GD_W].astype(jnp.float32))
    return out.astype(p.dtype), states


def _lin_combine(left, right):
    a1, b1 = left
    a2, b2 = right
    return a1 * a2, a2 * b1 + b2


def _rglru_direction(xc, h0, wa, ba, wx, bx, lam):
    bsz, length, _ = xc.shape
    xb = xc.reshape(bsz, length, RG_BLOCKS, RG_BLOCK)
    gate_r = jax.nn.sigmoid(jnp.einsum('blni,nij->blnj', xb, wa).reshape(bsz, length, RG_W) + ba)
    gate_i = jax.nn.sigmoid(jnp.einsum('blni,nij->blnj', xb, wx).reshape(bsz, length, RG_W) + bx)
    log_a = -RG_C * gate_r * jax.nn.softplus(-lam)
    a = jnp.exp(log_a)
    b = jnp.sqrt(-jnp.expm1(2 * log_a)) * (gate_i * xc)
    b = b.at[:, 0].add(a[:, 0] * h0)
    _, h = lax.associative_scan(_lin_combine, (a, b), axis=1)
    return h, h[:, -1]


def _rglru_mixer(p, n_rows, row_len, h0s, conv_w, conv_b, wa, ba, wx, bx, lam, with_out):
    bsz = p.shape[0]
    xc = (_short_conv(p[..., :RG_W], conv_w, n_rows, row_len, SHORT_PAD) + conv_b).astype(jnp.float32)
    hs, states = [], []
    for d in range(2):
        h0 = jnp.zeros((bsz, RG_W), jnp.float32) if h0s is None else h0s[d]
        h, s = _rglru_direction(_flip_if(xc, d), h0, wa[d], ba[d], wx[d], bx[d], lam[d])
        hs.append(h)
        states.append(s)
    if not with_out:
        return None, states
    out = jax.nn.gelu(p[..., RG_W:].astype(jnp.float32)) * (hs[0] + jnp.flip(hs[1], axis=1))
    return out.astype(p.dtype), states


def setup_inputs(seed: int = 0) -> dict:
    key = jax.random.key(seed)
    keys = list(jax.random.split(key, 64))

    def nrm(shape, scale):
        return scale * jax.random.normal(keys.pop(), shape, jnp.float32)

    def uni(shape, lo, hi):
        return jax.random.uniform(keys.pop(), shape, jnp.float32, lo, hi)

    def gain(shape):
        return 1.0 + nrm(shape, 0.02)

    dt = jnp.exp(uni((DEPTH, 2, GD_HEADS), math.log(1e-3), math.log(1e-1)))
    a_pow = uni((DEPTH, 2, RG_W), 0.9, 0.999)
    a_base = a_pow ** (1.0 / RG_C)
    return {
        'x': nrm((BATCH, SEQ, D_MODEL), 1.0),
        'c': nrm((BATCH, D_MODEL), 1.0),
        'ctx': nrm((BATCH, CTX_LEN, D_MODEL), 1.0),
        'c_ctx': nrm((D_MODEL,), 1.0),
        'ada_w': nrm((DEPTH, D_MODEL, 6 * D_MODEL), 0.5 * D_MODEL ** -0.5),
        'ada_b': nrm((DEPTH, 6 * D_MODEL), 0.01),
        'norm_mix_g': gain((DEPTH, D_MODEL)),
        'norm_mlp_g': gain((DEPTH, D_MODEL)),
        'w_in': nrm((DEPTH, D_MODEL, N_IN), D_MODEL ** -0.5),
        'w_out': nrm((DEPTH, D_MIX, D_MODEL), D_MIX ** -0.5),
        'hy_conv': nrm((DEPTH, HY_SHORT, HY_IN), HY_SHORT ** -0.5),
        'hy_w1': nrm((DEPTH, HY_EMB, HY_FILT), HY_EMB ** -0.5),
        'hy_b1': nrm((DEPTH, HY_FILT), 0.1),
        'hy_w2': nrm((DEPTH, HY_FILT, HY_FILT), HY_FILT ** -0.5),
        'hy_b2': nrm((DEPTH, HY_FILT), 0.1),
        'hy_w3': nrm((DEPTH, HY_FILT, 2 * HY_W), HY_FILT ** -0.5),
        'hy_freq': gain((DEPTH, 2, HY_FILT)),
        'hy_skip': nrm((DEPTH, HY_W), 1.0),
        'rw_mu': uni((DEPTH, 2, RW_IN), 0.0, 1.0),
        'rw_w0': uni((DEPTH, 2, RW_W), -3.0, 2.0),
        'rw_w_up': nrm((DEPTH, 2, RW_LORA_W, RW_W), 0.1 * RW_LORA_W ** -0.5),
        'rw_a0': nrm((DEPTH, 2, RW_W), 0.5),
        'rw_a_up': nrm((DEPTH, 2, RW_LORA_A, RW_W), 0.1 * RW_LORA_A ** -0.5),
        'rw_g_up': nrm((DEPTH, 2, RW_LORA_G, RW_W), RW_LORA_G ** -0.5),
        'rw_k_k': 0.85 + nrm((DEPTH, 2, RW_W), 0.02),
        'rw_k_a': gain((DEPTH, 2, RW_W)),
        'rw_r_k': nrm((DEPTH, 2, RW_W), 0.1),
        'rw_ln_g': gain((DEPTH, 2, RW_W)),
        'rw_ln_b': nrm((DEPTH, 2, RW_W), 0.01),
        'gd_conv': nrm((DEPTH, SHORT_CONV, 3 * GD_W), 0.5),
        'gd_a_log': jnp.log(uni((DEPTH, 2, GD_HEADS), 1.0, 16.0)),
        'gd_dt_bias': dt + jnp.log(-jnp.expm1(-dt)),
        'gd_norm_g': gain((DEPTH, GD_HEAD)),
        'rg_conv': nrm((DEPTH, SHORT_CONV, RG_W), 0.5),
        'rg_conv_b': nrm((DEPTH, RG_W), 0.01),
        'rg_wa': nrm((DEPTH, 2, RG_BLOCKS, RG_BLOCK, RG_BLOCK), RG_BLOCK ** -0.5),
        'rg_ba': nrm((DEPTH, 2, RG_W), 0.01),
        'rg_wx': nrm((DEPTH, 2, RG_BLOCKS, RG_BLOCK, RG_BLOCK), RG_BLOCK ** -0.5),
        'rg_bx': nrm((DEPTH, 2, RG_W), 0.01),
        'rg_lambda': jnp.log(a_base) - jnp.log1p(-a_base),
        'mlp_w1': nrm((DEPTH, D_MODEL, D_FF), D_MODEL ** -0.5),
        'mlp_w2': nrm((DEPTH, D_FF, D_MODEL), D_FF ** -0.5),
        'final_norm_g': gain((D_MODEL,)),
    }


def reference(x, c, ctx, c_ctx, ada_w, ada_b, norm_mix_g, norm_mlp_g, w_in, w_out,
              hy_conv, hy_w1, hy_b1, hy_w2, hy_b2, hy_w3, hy_freq, hy_skip,
              rw_mu, rw_w0, rw_w_up, rw_a0, rw_a_up, rw_g_up, rw_k_k, rw_k_a, rw_r_k, rw_ln_g, rw_ln_b,
              gd_conv, gd_a_log, gd_dt_bias, gd_norm_g,
              rg_conv, rg_conv_b, rg_wa, rg_ba, rg_wx, rg_bx, rg_lambda,
              mlp_w1, mlp_w2, final_norm_g):
    rows = x.shape[1] // GRID_W
    ctx_len = ctx.shape[1]
    for l in range(DEPTH):
        last = l == DEPTH - 1
        sh_x, sc_x, gt_x, sh2_x, sc2_x, gt2_x = _modulation(c[:, None, :], ada_w[l], ada_b[l])
        sh_c, sc_c, gt_c, sh2_c, sc2_c, gt2_c = _modulation(c_ctx[None, None, :], ada_w[l], ada_b[l])
        px = _modulate(_rmsnorm(x, norm_mix_g[l]), sh_x, sc_x) @ w_in[l]
        pc = _modulate(_rmsnorm(ctx, norm_mix_g[l]), sh_c, sc_c) @ w_in[l]
        px_hy, px_rw, px_gd, px_rg = jnp.split(px, IN_SPLITS, axis=-1)
        pc_hy, pc_rw, pc_gd, pc_rg = jnp.split(pc, IN_SPLITS, axis=-1)
        hy_prm = (hy_conv[l], hy_w1[l], hy_b1[l], hy_w2[l], hy_b2[l], hy_w3[l], hy_freq[l], hy_skip[l])
        rw_prm = (rw_mu[l], rw_w0[l], rw_w_up[l], rw_a0[l], rw_a_up[l], rw_g_up[l],
                  rw_k_k[l], rw_k_a[l], rw_r_k[l], rw_ln_g[l], rw_ln_b[l])
        gd_prm = (gd_conv[l], gd_a_log[l], gd_dt_bias[l], gd_norm_g[l])
        rg_prm = (rg_conv[l], rg_conv_b[l], rg_wa[l], rg_ba[l], rg_wx[l], rg_bx[l], rg_lambda[l])
        c_rw, s_rw = _rwkv7_mixer(pc_rw, None, rw_prm, not last)
        c_gd, s_gd = _gdn_mixer(pc_gd, 1, ctx_len, None, *gd_prm, not last)
        c_rg, s_rg = _rglru_mixer(pc_rg, 1, ctx_len, None, *rg_prm, not last)
        x_hy = _hyena_mixer(px_hy, rows, GRID_W, *hy_prm)
        x_rw, _ = _rwkv7_mixer(px_rw, s_rw, rw_prm, True)
        x_gd, _ = _gdn_mixer(px_gd, rows, GRID_W, s_gd, *gd_prm, True)
        x_rg, _ = _rglru_mixer(px_rg, rows, GRID_W, s_rg, *rg_prm, True)
        mix_x = jnp.concatenate([x_hy, x_rw, x_gd, x_rg], axis=-1) @ w_out[l]
        x_new = x + gt_x * mix_x
        x_new = x_new + gt2_x * _mlp(_modulate(_rmsnorm(x_new, norm_mlp_g[l]), sh2_x, sc2_x), mlp_w1[l], mlp_w2[l])
        if not last:
            c_hy = _hyena_mixer(pc_hy, 1, ctx_len, *hy_prm)
            mix_c = jnp.concatenate([c_hy, c_rw, c_gd, c_rg], axis=-1) @ w_out[l]
            ctx = ctx + gt_c * mix_c
            ctx = ctx + gt2_c * _mlp(_modulate(_rmsnorm(ctx, norm_mlp_g[l]), sh2_c, sc2_c), mlp_w1[l], mlp_w2[l])
        x = x_new
    return _rmsnorm(x, final_norm_g)
```

```python
import functools
import math

import jax
import jax.numpy as jnp
from jax import lax
from jax.experimental import pallas as pl
from jax.experimental.pallas import tpu as pltpu

F32 = jnp.float32
BF16 = jnp.bfloat16
HIGHEST = lax.Precision.HIGHEST

D_MODEL = 2048
GRID_W = 64
HY_W = RW_W = GD_W = RG_W = 512
D_FF = 4 * D_MODEL
NORM_EPS = 1e-6
HY_EMB = 33
HY_BANDS = 16
RW_HEAD = 64
RW_GN_EPS = 64e-5
GD_HEAD = 128
GD_HEADS = 4
CHUNK = 64
RG_C = 8.0
LANES = 128

COL_HY = 0
COL_RW = 1536
COL_GD = 3072
COL_RG = 5120
COL_RW_LORA = 6144
COL_GD_GB = 6400
N_PROJ = 6528

VMEM_LIMIT = 56 * 1024 * 1024


def _cparams(n_axes):
    return pltpu.CompilerParams(dimension_semantics=("arbitrary",) * n_axes,
                                vmem_limit_bytes=VMEM_LIMIT)


def _dot(a, b):
    return jnp.dot(a.astype(BF16), b.astype(BF16), preferred_element_type=F32)


def _dot_nt(a, b):
    return lax.dot_general(a.astype(BF16), b.astype(BF16), (((1,), (1,)), ((), ())),
                           preferred_element_type=F32)


def _dot_tn(a, b):
    return lax.dot_general(a.astype(BF16), b.astype(BF16), (((0,), (0,)), ((), ())),
                           preferred_element_type=F32)


def _dot_hp(a, b):
    return jnp.dot(a, b, precision=HIGHEST, preferred_element_type=F32)


def _dot_nt_hp(a, b):
    return lax.dot_general(a, b, (((1,), (1,)), ((), ())), precision=HIGHEST,
                           preferred_element_type=F32)


def _sigmoid(x):
    return 1.0 / (1.0 + jnp.exp(-x))


def _silu(x):
    return x * _sigmoid(x)


def _softplus(x):
    return jnp.maximum(x, 0.0) + jnp.log1p(jnp.exp(-jnp.abs(x)))


def _short_conv(u, w, row_len, pad_left):
    t_len = u.shape[0]
    pos = lax.broadcasted_iota(jnp.int32, u.shape, 0) & (row_len - 1)
    y = None
    for j in range(w.shape[0]):
        off = j - pad_left
        if off == 0:
            term = u * w[j:j + 1, :]
        else:
            sh = pltpu.roll(u, (-off) % t_len, 0)
            ok = (pos + off >= 0) & (pos + off < row_len)
            term = jnp.where(ok, sh, 0.0) * w[j:j + 1, :]
        y = term if y is None else y + term
    return y


def _mod_kernel(c_ref, w_ref, b_ref, o_ref):
    o_ref[...] = _dot(_silu(c_ref[...]), w_ref[...]) + b_ref[...]


def _modulation(cond8, ada_w, ada_b):
    n = ada_w.shape[1]
    tn = 512
    return pl.pallas_call(
        _mod_kernel,
        out_shape=jax.ShapeDtypeStruct((8, n), F32),
        grid=(n // tn,),
        in_specs=[pl.BlockSpec((8, D_MODEL), lambda j: (0, 0)),
                  pl.BlockSpec((D_MODEL, tn), lambda j: (0, j)),
                  pl.BlockSpec((1, tn), lambda j: (0, j))],
        out_specs=pl.BlockSpec((8, tn), lambda j: (0, j)),
        compiler_params=_cparams(1),
        name="adaln_mod",
    )(cond8, ada_w, ada_b.reshape(1, n))


def _norm_mod(x, g, shift, scale):
    y = x * lax.rsqrt(jnp.mean(x * x, axis=-1, keepdims=True) + NORM_EPS) * g
    return y * (1.0 + scale) + shift


def _inproj_kernel(x_ref, g_ref, mod_ref, w_ref, o_ref, h_scr):
    @pl.when(pl.program_id(2) == 0)
    def _():
        h = _norm_mod(x_ref[0], g_ref[...], mod_ref[0, 0:1, :], mod_ref[0, 1:2, :])
        h_scr[...] = h.astype(BF16)

    o_ref[0] = jnp.dot(h_scr[...], w_ref[...], preferred_element_type=F32)


def _inproj(x, g, mod, w_bf):
    bsz, length, _ = x.shape
    tm = min(length, 1024)
    tn = 384
    per_batch = mod.shape[0] == bsz
    return pl.pallas_call(
        _inproj_kernel,
        out_shape=jax.ShapeDtypeStruct((bsz, length, N_PROJ), F32),
        grid=(bsz, length // tm, N_PROJ // tn),
        in_specs=[pl.BlockSpec((1, tm, D_MODEL), lambda b, i, j: (b, i, 0)),
                  pl.BlockSpec((1, D_MODEL), lambda b, i, j: (0, 0)),
                  pl.BlockSpec((1, 6, D_MODEL), (lambda b, i, j: (b, 0, 0)) if per_batch
                               else (lambda b, i, j: (0, 0, 0))),
                  pl.BlockSpec((D_MODEL, tn), lambda b, i, j: (0, j))],
        out_specs=pl.BlockSpec((1, tm, tn), lambda b, i, j: (b, i, j)),
        scratch_shapes=[pltpu.VMEM((tm, D_MODEL), BF16)],
        compiler_params=_cparams(3),
        name="inproj",
    )(x, g.reshape(1, D_MODEL), mod, w_bf)


def _outproj_kernel(x_ref, mod_ref, m0_ref, m1_ref, m2_ref, m3_ref, w_ref, o_ref):
    acc = jnp.dot(m0_ref[0], w_ref[0:512, :], preferred_element_type=F32)
    acc += jnp.dot(m1_ref[0], w_ref[512:1024, :], preferred_element_type=F32)
    acc += jnp.dot(m2_ref[0], w_ref[1024:1536, :], preferred_element_type=F32)
    acc += jnp.dot(m3_ref[0], w_ref[1536:2048, :], preferred_element_type=F32)
    o_ref[0] = x_ref[0] + mod_ref[0, 2:3, :] * acc


def _outproj(x, mod, mixers, w_bf):
    bsz, length, _ = x.shape
    tm = min(length, 512)
    per_batch = mod.shape[0] == bsz
    mspec = pl.BlockSpec((1, tm, 512), lambda b, i: (b, i, 0))
    return pl.pallas_call(
        _outproj_kernel,
        out_shape=jax.ShapeDtypeStruct((bsz, length, D_MODEL), F32),
        grid=(bsz, length // tm),
        in_specs=[pl.BlockSpec((1, tm, D_MODEL), lambda b, i: (b, i, 0)),
                  pl.BlockSpec((1, 6, D_MODEL), (lambda b, i: (b, 0, 0)) if per_batch
                               else (lambda b, i: (0, 0, 0))),
                  mspec, mspec, mspec, mspec,
                  pl.BlockSpec((D_MODEL, D_MODEL), lambda b, i: (0, 0))],
        out_specs=pl.BlockSpec((1, tm, D_MODEL), lambda b, i: (b, i, 0)),
        compiler_params=_cparams(2),
        name="outproj",
    )(x, mod, *mixers, w_bf)


def _mlp_kernel(x_ref, g_ref, mod_ref, w1_ref, w2_ref, fg_ref, o_ref, h_scr, acc_scr, *, final_norm):
    f = pl.program_id(2)

    @pl.when(f == 0)
    def _():
        h = _norm_mod(x_ref[0], g_ref[...], mod_ref[0, 3:4, :], mod_ref[0, 4:5, :])
        h_scr[...] = h.astype(BF16)
        acc_scr[...] = jnp.zeros_like(acc_scr)

    a = jnp.dot(h_scr[...], w1_ref[...], preferred_element_type=F32)
    a = jnp.square(jnp.maximum(a, 0.0)).astype(BF16)
    acc_scr[...] += jnp.dot(a, w2_ref[...], preferred_element_type=F32)

    @pl.when(f == pl.num_programs(2) - 1)
    def _():
        y = x_ref[0] + mod_ref[0, 5:6, :] * acc_scr[...]
        if final_norm:
            y = y * lax.rsqrt(jnp.mean(y * y, axis=-1, keepdims=True) + NORM_EPS) * fg_ref[...]
        o_ref[0] = y


def _mlp(x, g, mod, w1_bf, w2_bf, final_g):
    bsz, length, _ = x.shape
    tm = min(length, 512)
    tf = 1024
    per_batch = mod.shape[0] == bsz
    final_norm = final_g is not None
    fg = (final_g if final_norm else jnp.ones((D_MODEL,), F32)).reshape(1, D_MODEL)
    return pl.pallas_call(
        functools.partial(_mlp_kernel, final_norm=final_norm),
        out_shape=jax.ShapeDtypeStruct((bsz, length, D_MODEL), F32),
        grid=(bsz, length // tm, D_FF // tf),
        in_specs=[pl.BlockSpec((1, tm, D_MODEL), lambda b, i, f: (b, i, 0)),
                  pl.BlockSpec((1, D_MODEL), lambda b, i, f: (0, 0)),
                  pl.BlockSpec((1, 6, D_MODEL), (lambda b, i, f: (b, 0, 0)) if per_batch
                               else (lambda b, i, f: (0, 0, 0))),
                  pl.BlockSpec((D_MODEL, tf), lambda b, i, f: (0, f)),
                  pl.BlockSpec((tf, D_MODEL), lambda b, i, f: (f, 0)),
                  pl.BlockSpec((1, D_MODEL), lambda b, i, f: (0, 0))],
        out_specs=pl.BlockSpec((1, tm, D_MODEL), lambda b, i, f: (b, i, 0)),
        scratch_shapes=[pltpu.VMEM((tm, D_MODEL), BF16), pltpu.VMEM((tm, D_MODEL), F32)],
        compiler_params=_cparams(3),
        name="mlp",
    )(x, g.reshape(1, D_MODEL), mod, w1_bf, w2_bf, fg)


def _lin_scan(a, b, rev):
    t_len = a.shape[0]
    row = lax.broadcasted_iota(jnp.int32, a.shape, 0)
    s = 1
    while s < t_len:
        if rev:
            a_sh = pltpu.roll(a, t_len - s, 0)
            b_sh = pltpu.roll(b, t_len - s, 0)
            ok = row < t_len - s
        else:
            a_sh = pltpu.roll(a, s, 0)
            b_sh = pltpu.roll(b, s, 0)
            ok = row >= s
        b = a * jnp.where(ok, b_sh, 0.0) + b
        a = a * jnp.where(ok, a_sh, 1.0)
        s *= 2
    return a, b


def _rglru_kernel(*refs, rev, row_len, final):
    if final:
        (x_ref, gate_ref, cw_ref, cb_ref, wa_ref, ba_ref, wx_ref, bx_ref, lam_ref, h0_ref, prev_ref,
         o_ref, st_ref, carry) = refs
    else:
        (x_ref, cw_ref, cb_ref, wa_ref, ba_ref, wx_ref, bx_ref, lam_ref, h0_ref,
         o_ref, st_ref, carry) = refs
    i = pl.program_id(2)

    @pl.when(i == 0)
    def _():
        carry[...] = h0_ref[0]

    xc = _short_conv(x_ref[0], cw_ref[...], row_len, 2) + cb_ref[...]
    gate_r = _sigmoid(_dot(xc, wa_ref[0]) + ba_ref[...])
    gate_i = _sigmoid(_dot(xc, wx_ref[0]) + bx_ref[...])
    log_a = -RG_C * gate_r * _softplus(-lam_ref[...])
    a = jnp.exp(log_a)
    b = jnp.sqrt(-jnp.tanh(log_a) * (a * a + 1.0)) * (gate_i * xc)
    a_cum, b_cum = _lin_scan(a, b, rev)
    h = a_cum * carry[...] + b_cum
    t_len = h.shape[0]
    last = h[0:1, :] if rev else h[t_len - 1:t_len, :]
    carry[...] = last

    @pl.when(i == pl.num_programs(2) - 1)
    def _():
        st_ref[0] = last

    if final:
        gate = gate_ref[0]
        gelu = 0.5 * gate * (1.0 + jnp.tanh(math.sqrt(2.0 / math.pi) * (gate + 0.044715 * gate * gate * gate)))
        o_ref[0] = (gelu * (prev_ref[0] + h)).astype(o_ref.dtype)
    else:
        o_ref[0] = h


def _rglru_pass(p, prm, d, h0, prev, row_len):
    bsz, length, _ = p.shape
    conv_w, conv_b, wa, ba, wx, bx, lam = prm
    final = prev is not None
    rev = d == 1
    t = min(length, 256)
    nt = length // t
    tidx = (lambda i: nt - 1 - i) if rev else (lambda i: i)
    cx = COL_RG // LANES
    col = lambda c0: pl.BlockSpec((1, t, LANES), lambda b, n, i: (b, tidx(i), c0 + n))
    vec = pl.BlockSpec((1, LANES), lambda b, n, i: (0, n))
    mat = pl.BlockSpec((1, LANES, LANES), lambda b, n, i: (n, 0, 0))
    in_specs = [col(cx)]
    args = [p]
    if final:
        in_specs.append(col(cx + 4))
        args.append(p)
    in_specs += [pl.BlockSpec((4, LANES), lambda b, n, i: (0, n)), vec, mat, vec, mat, vec, vec,
                 pl.BlockSpec((1, 1, LANES), lambda b, n, i: (b, 0, n))]
    args += [conv_w, conv_b.reshape(1, RG_W), wa[d].astype(BF16), ba[d].reshape(1, RG_W),
             wx[d].astype(BF16), bx[d].reshape(1, RG_W), lam[d].reshape(1, RG_W), h0]
    if final:
        in_specs.append(pl.BlockSpec((1, t, LANES), lambda b, n, i: (b, tidx(i), n)))
        args.append(prev)
    out, st = pl.pallas_call(
        functools.partial(_rglru_kernel, rev=rev, row_len=row_len, final=final),
        out_shape=[jax.ShapeDtypeStruct((bsz, length, RG_W), BF16 if final else F32),
                   jax.ShapeDtypeStruct((bsz, 1, RG_W), F32)],
        grid=(bsz, 4, nt),
        in_specs=in_specs,
        out_specs=[pl.BlockSpec((1, t, LANES), lambda b, n, i: (b, tidx(i), n)),
                   pl.BlockSpec((1, 1, LANES), lambda b, n, i: (b, 0, n))],
        scratch_shapes=[pltpu.VMEM((1, LANES), F32)],
        compiler_params=_cparams(3),
        name="rglru_bwd" if rev else "rglru_fwd",
    )(*args)
    return out, st


def _rglru_mixer(p, prm, h0s, row_len):
    bsz = p.shape[0]
    if h0s is None:
        h0s = [jnp.zeros((bsz, 1, RG_W), F32)] * 2
    h_f, s_f = _rglru_pass(p, prm, 0, h0s[0], None, row_len)
    out, s_b = _rglru_pass(p, prm, 1, h0s[1], h_f, row_len)
    return out, [s_f, s_b]


def _tri_masks(n, rev):
    row = lax.broadcasted_iota(jnp.int32, (n, n), 0)
    col = lax.broadcasted_iota(jnp.int32, (n, n), 1)
    if rev:
        return row <= col, row < col
    return row >= col, row > col


def _unit_solve(neg_n, rhs, steps):
    x = rhs
    m = neg_n
    for s in range(steps):
        x = x + _dot_hp(m, x)
        if s + 1 < steps:
            m = _dot_hp(m, m)
    return x


def _gdn_kernel(*refs, rev, row_len, final, d):
    if final:
        (q_ref, k_ref, v_ref, z_ref, gb_ref, cq_ref, ck_ref, cv_ref, alog_ref, dtb_ref, ng_ref, s0_ref,
         prev_ref, o_ref, st_ref, state) = refs
    else:
        (q_ref, k_ref, v_ref, gb_ref, cq_ref, ck_ref, cv_ref, alog_ref, dtb_ref, ng_ref, s0_ref,
         o_ref, st_ref, state) = refs
    h = pl.program_id(1)
    i = pl.program_id(2)

    @pl.when(i == 0)
    def _():
        state[...] = s0_ref[0, 0]

    qc = _silu(_short_conv(q_ref[0], cq_ref[...], row_len, 2))
    kc = _silu(_short_conv(k_ref[0], ck_ref[...], row_len, 2))
    vc = _silu(_short_conv(v_ref[0], cv_ref[...], row_len, 2))
    q = qc * lax.rsqrt(jnp.sum(qc * qc, axis=-1, keepdims=True) + 1e-6) * (GD_HEAD ** -0.5)
    k = kc * lax.rsqrt(jnp.sum(kc * kc, axis=-1, keepdims=True) + 1e-6)
    gbb = gb_ref[0]
    lane = lax.broadcasted_iota(jnp.int32, gbb.shape, 1)
    g_raw = jnp.sum(jnp.where(lane == d * GD_HEADS + h, gbb, 0.0), axis=-1, keepdims=True)
    b_raw = jnp.sum(jnp.where(lane == (2 + d) * GD_HEADS + h, gbb, 0.0), axis=-1, keepdims=True)
    g = -jnp.exp(alog_ref[0]) * _softplus(g_raw + dtb_ref[0])
    beta = _sigmoid(b_raw)

    t_len = q.shape[0]
    n_chunks = t_len // CHUNK
    incl, strict = _tri_masks(CHUNK, rev)
    tri = incl.astype(F32)
    ones = jnp.ones((CHUNK, LANES), F32)
    s_cur = state[...]
    order = range(n_chunks - 1, -1, -1) if rev else range(n_chunks)
    for c in order:
        sl = slice(c * CHUNK, (c + 1) * CHUNK)
        q_c, k_c, v_c, b_c = q[sl], k[sl], vc[sl], beta[sl]
        gc = _dot_hp(tri, g[sl])
        g_row = _dot_nt_hp(ones, gc * (1.0 / LANES))
        decay_in = jnp.exp(jnp.where(incl, gc[:, :CHUNK] - g_row, -jnp.inf))
        kb = k_c * b_c
        neg_n = -(_dot_nt(kb, k_c) * jnp.where(strict, decay_in, 0.0))
        e_gc = jnp.exp(gc)
        rhs = jnp.concatenate([v_c * b_c, kb * e_gc], axis=-1)
        sol = _unit_solve(neg_n, rhs, 6)
        u, w = sol[:, :GD_HEAD], sol[:, GD_HEAD:]
        g_last = gc[0:1, :] if rev else gc[CHUNK - 1:CHUNK, :]
        k_dec = k_c * jnp.exp(g_last - gc)
        a_qk = _dot_nt(q_c, k_c) * decay_in
        v_new = u - _dot(w, s_cur)
        o_c = _dot(q_c * e_gc, s_cur) + _dot(a_qk, v_new)
        s_cur = s_cur * jnp.exp(g_last) + _dot_tn(k_dec, v_new)
        if final:
            o_t = prev_ref[0, sl, :] + o_c
            o_t = o_t * lax.rsqrt(jnp.mean(o_t * o_t, axis=-1, keepdims=True) + NORM_EPS) * ng_ref[...]
            o_ref[0, sl, :] = (o_t * _silu(z_ref[0, sl, :])).astype(o_ref.dtype)
        else:
            o_ref[0, sl, :] = o_c
    state[...] = s_cur

    @pl.when(i == pl.num_programs(2) - 1)
    def _():
        st_ref[0, 0] = s_cur


def _gdn_pass(p, prm, d, s0, prev, row_len):
    bsz, length, _ = p.shape
    conv_w, a_log, dt_bias, norm_g = prm
    final = prev is not None
    rev = d == 1
    t = min(length, 256)
    nt = length // t
    tidx = (lambda i: nt - 1 - i) if rev else (lambda i: i)
    c0 = COL_GD // LANES
    col = lambda off: pl.BlockSpec((1, t, LANES), lambda b, h, i: (b, tidx(i), c0 + off + h))
    cw = lambda off: pl.BlockSpec((4, LANES), lambda b, h, i: (0, off + h))
    hvec = pl.BlockSpec((1, 1, LANES), lambda b, h, i: (h, 0, 0))
    in_specs = [col(0), col(4), col(8)]
    args = [p, p, p]
    if final:
        in_specs.append(col(12))
        args.append(p)
    in_specs += [pl.BlockSpec((1, t, LANES), lambda b, h, i: (b, tidx(i), COL_GD_GB // LANES)),
                 cw(0), cw(4), cw(8), hvec, hvec,
                 pl.BlockSpec((1, LANES), lambda b, h, i: (0, 0)),
                 pl.BlockSpec((1, 1, GD_HEAD, GD_HEAD), lambda b, h, i: (b, h, 0, 0))]
    bcast = lambda v: jnp.broadcast_to(v.reshape(GD_HEADS, 1, 1), (GD_HEADS, 1, LANES))
    args += [p, conv_w, conv_w, conv_w, bcast(a_log[d]), bcast(dt_bias[d]), norm_g.reshape(1, GD_HEAD), s0]
    if final:
        in_specs.append(pl.BlockSpec((1, t, LANES), lambda b, h, i: (b, tidx(i), h)))
        args.append(prev)
    out, st = pl.pallas_call(
        functools.partial(_gdn_kernel, rev=rev, row_len=row_len, final=final, d=d),
        out_shape=[jax.ShapeDtypeStruct((bsz, length, GD_W), BF16 if final else F32),
                   jax.ShapeDtypeStruct((bsz, GD_HEADS, GD_HEAD, GD_HEAD), F32)],
        grid=(bsz, GD_HEADS, nt),
        in_specs=in_specs,
        out_specs=[pl.BlockSpec((1, t, LANES), lambda b, h, i: (b, tidx(i), h)),
                   pl.BlockSpec((1, 1, GD_HEAD, GD_HEAD), lambda b, h, i: (b, h, 0, 0))],
        scratch_shapes=[pltpu.VMEM((GD_HEAD, GD_HEAD), F32)],
        compiler_params=_cparams(3),
        name="gdn_bwd" if rev else "gdn_fwd",
    )(*args)
    return out, st


def _gdn_mixer(p, prm, s0s, row_len):
    bsz = p.shape[0]
    if s0s is None:
        s0s = [jnp.zeros((bsz, GD_HEADS, GD_HEAD, GD_HEAD), F32)] * 2
    o_f, s_f = _gdn_pass(p, prm, 0, s0s[0], None, row_len)
    out, s_b = _gdn_pass(p, prm, 1, s0s[1], o_f, row_len)
    return out, [s_f, s_b]


def _stack_heads(x, lo):
    return jnp.concatenate([jnp.where(lo, x, 0.0), jnp.where(lo, 0.0, x)], axis=0)


def _rwkv_kernel(*refs, rev, final):
    if final:
        (r_ref, k_ref, v_ref, lo_ref, mur_ref, muk_ref, muv_ref, mul_ref, w0_ref, wup_ref, a0_ref, aup_ref,
         gup_ref, kk_ref, ka_ref, rk_ref, lng_ref, lnb_ref, s0_ref, prev_ref,
         o_ref, st_ref, state, c_r, c_k, c_v, c_l) = refs
    else:
        (r_ref, k_ref, v_ref, lo_ref, mur_ref, muk_ref, muv_ref, mul_ref, w0_ref, wup_ref, a0_ref, aup_ref,
         gup_ref, kk_ref, ka_ref, rk_ref, lng_ref, lnb_ref, s0_ref,
         o_ref, st_ref, state, c_r, c_k, c_v, c_l) = refs
    i = pl.program_id(2)

    @pl.when(i == 0)
    def _():
        state[...] = s0_ref[0, 0]
        c_r[...] = jnp.zeros_like(c_r)
        c_k[...] = jnp.zeros_like(c_k)
        c_v[...] = jnp.zeros_like(c_v)
        c_l[...] = jnp.zeros_like(c_l)

    t_len = r_ref.shape[1]

    def shifted(x_ref, carry, mu_ref):
        x = x_ref[0]
        row = lax.broadcasted_iota(jnp.int32, x.shape, 0)
        if rev:
            prev = jnp.where(row == t_len - 1, carry[...], pltpu.roll(x, t_len - 1, 0))
            carry[...] = x[0:1, :]
        else:
            prev = jnp.where(row == 0, carry[...], pltpu.roll(x, 1, 0))
            carry[...] = x[t_len - 1:t_len, :]
        return x + (prev - x) * mu_ref[...]

    r = shifted(r_ref, c_r, mur_ref)
    k = shifted(k_ref, c_k, muk_ref)
    v = shifted(v_ref, c_v, muv_ref)
    lora = shifted(lo_ref, c_l, mul_ref)
    xw, xa, xg = lora[:, 0:64], lora[:, 64:128], lora[:, 128:256]

    lane = lax.broadcasted_iota(jnp.int32, (1, LANES), 1)
    lo = lane < RW_HEAD
    rowh = lax.broadcasted_iota(jnp.int32, (LANES, LANES), 0) // RW_HEAD
    colh = lax.broadcasted_iota(jnp.int32, (LANES, LANES), 1) // RW_HEAD
    head_sum = (rowh == colh).astype(F32)

    lw = -math.exp(-0.5) * _sigmoid(w0_ref[...] + _dot(jnp.tanh(xw), wup_ref[...]))
    a = _sigmoid(a0_ref[...] + _dot(xa, aup_ref[...]))
    kk = k * kk_ref[...]
    kappa = kk / jnp.maximum(jnp.sqrt(_dot_hp(kk * kk, head_sum)), 1e-12)
    kt = k * (1.0 + (a - 1.0) * ka_ref[...])
    gate = _dot(_sigmoid(xg), gup_ref[...])
    bonus = _dot_hp(r * kt * rk_ref[...], head_sum) * v

    n_chunks = t_len // CHUNK
    incl, _ = _tri_masks(CHUNK, rev)
    tri = incl.astype(F32)
    rowt = lax.broadcasted_iota(jnp.int32, (LANES, LANES), 0) & (CHUNK - 1)
    colt = lax.broadcasted_iota(jnp.int32, (LANES, LANES), 1) & (CHUNK - 1)
    same = rowh == colh
    incl2 = ((rowt <= colt) if rev else (rowt >= colt)) & same
    strict2 = ((rowt < colt) if rev else (rowt > colt)) & same
    s_cur = state[...]
    order = range(n_chunks - 1, -1, -1) if rev else range(n_chunks)
    for c in order:
        sl = slice(c * CHUNK, (c + 1) * CHUNK)
        lw_c, kap_c, a_c = lw[sl], kappa[sl], a[sl]
        lg = _dot_hp(tri, lw_c)
        lg_tot = lg[0:1, :] if rev else lg[CHUNK - 1:CHUNK, :]
        e_neg = jnp.exp(-lg)
        e_rem = jnp.exp(lg_tot - lg)
        p_raw = -(kap_c * a_c)
        q2 = _stack_heads(kap_c * jnp.exp(lg - lw_c), lo)
        p2 = _stack_heads(p_raw * e_neg, lo)
        k2 = _stack_heads(kt[sl] * e_neg, lo)
        r2 = _stack_heads(r[sl] * jnp.exp(lg), lo)
        v2 = _stack_heads(v[sl], lo)
        pt2 = _stack_heads(p_raw * e_rem, lo)
        kt2 = _stack_heads(kt[sl] * e_rem, lo)
        a_qp = jnp.where(strict2, _dot_nt(q2, p2), 0.0)
        a_qk = jnp.where(strict2, _dot_nt(q2, k2), 0.0)
        a_rp = jnp.where(incl2, _dot_nt(r2, p2), 0.0)
        a_rk = jnp.where(incl2, _dot_nt(r2, k2), 0.0)
        sol = _unit_solve(a_qp, jnp.concatenate([q2, _dot(a_qk, v2)], axis=-1), 6)
        wq, u_loc = sol[:, :LANES], sol[:, LANES:]
        u2 = _dot_nt(wq, s_cur) + u_loc
        y2 = _dot_nt(r2, s_cur) + _dot(a_rp, u2) + _dot(a_rk, v2)
        s_cur = s_cur * jnp.exp(lg_tot) + _dot_tn(u2, pt2) + _dot_tn(v2, kt2)
        y = y2[:CHUNK] + y2[CHUNK:]
        mean = _dot_hp(y, head_sum) * (1.0 / RW_HEAD)
        yc = y - mean
        var = _dot_hp(yc * yc, head_sum) * (1.0 / RW_HEAD)
        yn = yc * lax.rsqrt(var + RW_GN_EPS) * lng_ref[...] + lnb_ref[...]
        out = (yn + bonus[sl]) * gate[sl]
        if final:
            o_ref[0, sl, :] = (prev_ref[0, sl, :] + out).astype(o_ref.dtype)
        else:
            o_ref[0, sl, :] = out
    state[...] = s_cur

    @pl.when(i == pl.num_programs(2) - 1)
    def _():
        st_ref[0, 0] = s_cur


def _rwkv_pass(p, prm, d, s0, prev):
    bsz, length, _ = p.shape
    mu, w0, w_up, a0, a_up, g_up, k_k, k_a, r_k, ln_g, ln_b = [t[d] for t in prm]
    final = prev is not None
    rev = d == 1
    t = min(length, 256)
    nt = length // t
    tidx = (lambda i: nt - 1 - i) if rev else (lambda i: i)
    c0 = COL_RW // LANES
    col = lambda off: pl.BlockSpec((1, t, LANES), lambda b, n, i: (b, tidx(i), c0 + off + n))
    vec = pl.BlockSpec((1, LANES), lambda b, n, i: (0, n))
    whole = lambda shape: pl.BlockSpec(shape, lambda b, n, i: (0,) * len(shape))
    row = lambda x: x.reshape(1, -1)
    in_specs = [col(0), col(4), col(8),
                pl.BlockSpec((1, t, 256), lambda b, n, i: (b, tidx(i), COL_RW_LORA // 256)),
                vec, vec, vec, whole((1, 256)),
                vec, pl.BlockSpec((64, LANES), lambda b, n, i: (0, n)),
                vec, pl.BlockSpec((64, LANES), lambda b, n, i: (0, n)),
                pl.BlockSpec((LANES, LANES), lambda b, n, i: (0, n)),
                vec, vec, vec, vec, vec,
                pl.BlockSpec((1, 1, LANES, LANES), lambda b, n, i: (b, n, 0, 0))]
    args = [p, p, p, p,
            row(mu[0:512]), row(mu[512:1024]), row(mu[1024:1536]), row(mu[1536:1792]),
            row(w0), w_up.astype(BF16), row(a0), a_up.astype(BF16), g_up.astype(BF16),
            row(k_k), row(k_a), row(r_k), row(ln_g), row(ln_b), s0]
    if final:
        in_specs.append(pl.BlockSpec((1, t, LANES), lambda b, n, i: (b, tidx(i), n)))
        args.append(prev)
    out, st = pl.pallas_call(
        functools.partial(_rwkv_kernel, rev=rev, final=final),
        out_shape=[jax.ShapeDtypeStruct((bsz, length, RW_W), BF16 if final else F32),
                   jax.ShapeDtypeStruct((bsz, 4, LANES, LANES), F32)],
        grid=(bsz, 4, nt),
        in_specs=in_specs,
        out_specs=[pl.BlockSpec((1, t, LANES), lambda b, n, i: (b, tidx(i), n)),
                   pl.BlockSpec((1, 1, LANES, LANES), lambda b, n, i: (b, n, 0, 0))],
        scratch_shapes=[pltpu.VMEM((LANES, LANES), F32), pltpu.VMEM((1, LANES), F32),
                        pltpu.VMEM((1, LANES), F32), pltpu.VMEM((1, LANES), F32),
                        pltpu.VMEM((1, 256), F32)],
        compiler_params=_cparams(3),
        name="rwkv_bwd" if rev else "rwkv_fwd",
    )(*args)
    return out, st


def _rwkv_mixer(p, prm, s0s):
    bsz = p.shape[0]
    if s0s is None:
        s0s = [jnp.zeros((bsz, 4, LANES, LANES), F32)] * 2
    o_f, s_f = _rwkv_pass(p, prm, 0, s0s[0], None)
    out, s_b = _rwkv_pass(p, prm, 1, s0s[1], o_f)
    return out, [s_f, s_b]


def _dft_tables(length):
    n = 2 * length
    nfp = -(-(length + 1) // LANES) * LANES
    kf = jnp.arange(nfp, dtype=jnp.int32)
    ang = (2.0 * math.pi / n) * ((kf[:, None] * jnp.arange(n, dtype=jnp.int32)[None, :]) % n).astype(F32)
    ok = (kf <= length)[:, None]
    fwd_c = jnp.where(ok, jnp.cos(ang), 0.0).astype(BF16)
    fwd_s = jnp.where(ok, -jnp.sin(ang), 0.0).astype(BF16)
    ang_i = (2.0 * math.pi / n) * ((jnp.arange(length, dtype=jnp.int32)[:, None] * kf[None, :]) % n).astype(F32)
    wk = jnp.where((kf == 0) | (kf == length), 1.0, 2.0) * (kf <= length) / n
    inv_c = (wk[None, :] * jnp.cos(ang_i)).astype(BF16)
    inv_s = (-wk[None, :] * jnp.sin(ang_i)).astype(BF16)
    return fwd_c, fwd_s, inv_c, inv_s


def _hyena_filter(length, w1, b1, w2, b2, w3, freq):
    t = jnp.arange(length, dtype=F32)
    z = t / max(length - 1, 1)
    bands = jnp.linspace(1e-4, HY_BANDS - 1, HY_BANDS, dtype=F32)
    ang = (2.0 * math.pi / length) * t[:, None] * bands[None, :]
    feat = jnp.concatenate([z[:, None], jnp.cos(ang), -jnp.sin(ang)], axis=-1)
    h = jnp.sin(freq[0] * (feat @ w1 + b1))
    h = jnp.sin(freq[1] * (h @ w2 + b2))
    h = (h @ w3).astype(F32)
    deltas = jnp.abs(jnp.linspace(math.log(1e-2) / 1.5, math.log(1e-2) / 0.3, HY_W, dtype=F32))
    h = h * jnp.exp(-z[:, None] * jnp.tile(deltas, 2)[None, :])
    h_fwd, h_bwd = h[:, :HY_W], h[:, HY_W:]
    filt = jnp.concatenate([h_fwd, jnp.zeros((1, HY_W), F32), h_bwd[:0:-1]], axis=0)
    return filt / jnp.sum(jnp.abs(filt), axis=0, keepdims=True)


def _hy_zin_kernel(v_ref, x1_ref, cw_ref, o_ref, *, row_len):
    vc = _short_conv(v_ref[0], cw_ref[:, 0:512], row_len, 1)
    x1c = _short_conv(x1_ref[0], cw_ref[:, 1024:1536], row_len, 1)
    o_ref[0] = (x1c * vc).astype(o_ref.dtype)


def _hy_zin(p, conv_w, row_len):
    bsz, length, _ = p.shape
    t = min(length, 256)
    return pl.pallas_call(
        functools.partial(_hy_zin_kernel, row_len=row_len),
        out_shape=jax.ShapeDtypeStruct((bsz, length, HY_W), BF16),
        grid=(bsz, length // t),
        in_specs=[pl.BlockSpec((1, t, 512), lambda b, i: (b, i, 0)),
                  pl.BlockSpec((1, t, 512), lambda b, i: (b, i, 2)),
                  pl.BlockSpec((3, 1536), lambda b, i: (0, 0))],
        out_specs=pl.BlockSpec((1, t, 512), lambda b, i: (b, i, 0)),
        compiler_params=_cparams(2),
        name="hyena_zin",
    )(p, p, conv_w)


def _dft_fwd_kernel(*refs, mult):
    if mult:
        c_ref, s_ref, z_ref, fr_ref, fi_ref, yr_ref, yi_ref = refs
    else:
        c_ref, s_ref, z_ref, yr_ref, yi_ref = refs
    z = z_ref[0].astype(BF16)
    zr = jnp.dot(c_ref[...], z, preferred_element_type=F32)
    zi = jnp.dot(s_ref[...], z, preferred_element_type=F32)
    if mult:
        fr, fi = fr_ref[...], fi_ref[...]
        zr, zi = zr * fr - zi * fi, zr * fi + zi * fr
    yr_ref[0] = zr.astype(yr_ref.dtype)
    yi_ref[0] = zi.astype(yi_ref.dtype)


def _dft_fwd(tab_c, tab_s, z, spec):
    bsz, klen, _ = z.shape
    nfp = tab_c.shape[0]
    mult = spec is not None
    tm = 384 if (mult and nfp % 384 == 0) else LANES
    in_specs = [pl.BlockSpec((tm, klen), lambda b, i: (i, 0)),
                pl.BlockSpec((tm, klen), lambda b, i: (i, 0)),
                pl.BlockSpec((1, klen, 512), lambda b, i: (b, 0, 0))]
    args = [tab_c, tab_s, z]
    if mult:
        in_specs += [pl.BlockSpec((tm, 512), lambda b, i: (i, 0))] * 2
        args += list(spec)
    odt = BF16 if mult else F32
    return pl.pallas_call(
        functools.partial(_dft_fwd_kernel, mult=mult),
        out_shape=[jax.ShapeDtypeStruct((bsz, nfp, 512), odt)] * 2,
        grid=(bsz, nfp // tm),
        in_specs=in_specs,
        out_specs=[pl.BlockSpec((1, tm, 512), lambda b, i: (b, i, 0))] * 2,
        compiler_params=_cparams(2),
        name="hyena_dft_mul" if mult else "hyena_dft_filter",
    )(*args)


def _dft_inv_kernel(ci_ref, si_ref, yr_ref, yi_ref, v_ref, x0_ref, x1_ref, cw_ref, skip_ref, o_ref, *, row_len):
    y = jnp.dot(ci_ref[...], yr_ref[0], preferred_element_type=F32)
    y += jnp.dot(si_ref[...], yi_ref[0], preferred_element_type=F32)
    vc = _short_conv(v_ref[0], cw_ref[:, 0:512], row_len, 1)
    x0c = _short_conv(x0_ref[0], cw_ref[:, 512:1024], row_len, 1)
    x1c = _short_conv(x1_ref[0], cw_ref[:, 1024:1536], row_len, 1)
    zin = x1c * vc
    o_ref[0] = (x0c * (y + zin * skip_ref[...])).astype(o_ref.dtype)


def _dft_inv(inv_c, inv_s, yr, yi, p, conv_w, skip, row_len):
    bsz, length, _ = p.shape
    nfp = inv_c.shape[1]
    t = min(length, 256)
    pcol = lambda c: pl.BlockSpec((1, t, 512), lambda b, i: (b, i, c))
    return pl.pallas_call(
        functools.partial(_dft_inv_kernel, row_len=row_len),
        out_shape=jax.ShapeDtypeStruct((bsz, length, HY_W), BF16),
        grid=(bsz, length // t),
        in_specs=[pl.BlockSpec((t, nfp), lambda b, i: (i, 0)),
                  pl.BlockSpec((t, nfp), lambda b, i: (i, 0)),
                  pl.BlockSpec((1, nfp, 512), lambda b, i: (b, 0, 0)),
                  pl.BlockSpec((1, nfp, 512), lambda b, i: (b, 0, 0)),
                  pcol(0), pcol(1), pcol(2),
                  pl.BlockSpec((3, 1536), lambda b, i: (0, 0)),
                  pl.BlockSpec((1, 512), lambda b, i: (0, 0))],
        out_specs=pl.BlockSpec((1, t, 512), lambda b, i: (b, i, 0)),
        compiler_params=_cparams(2),
        name="hyena_idft_gate",
    )(inv_c, inv_s, yr, yi, p, p, p, conv_w, skip.reshape(1, HY_W))


def _hyena_mixer(p, prm, tables, row_len):
    conv_w, w1, b1, w2, b2, w3, freq, skip = prm
    length = p.shape[1]
    fwd_c, fwd_s, inv_c, inv_s = tables
    filt = _hyena_filter(length, w1, b1, w2, b2, w3, freq)
    spec = _dft_fwd(fwd_c, fwd_s, filt[None], None)
    spec = (spec[0][0], spec[1][0])
    zin = _hy_zin(p, conv_w, row_len)
    yr, yi = _dft_fwd(fwd_c, fwd_s, zin, spec)
    return _dft_inv(inv_c, inv_s, yr, yi, p, conv_w, skip, row_len)


def _permute_w_in(w_in):
    hy = w_in[:, 0:1536]
    rw = w_in[:, 1536:3328]
    gd = w_in[:, 3328:5392]
    rg = w_in[:, 5392:6416]
    pad = jnp.zeros((w_in.shape[0], N_PROJ - 6416), w_in.dtype)
    return jnp.concatenate([hy, rw[:, :1536], gd[:, :2048], rg, rw[:, 1536:], gd[:, 2048:], pad],
                           axis=-1).astype(BF16)


def kernel(x, c, ctx, c_ctx, ada_w, ada_b, norm_mix_g, norm_mlp_g, w_in, w_out,
           hy_conv, hy_w1, hy_b1, hy_w2, hy_b2, hy_w3, hy_freq, hy_skip,
           rw_mu, rw_w0, rw_w_up, rw_a0, rw_a_up, rw_g_up, rw_k_k, rw_k_a, rw_r_k, rw_ln_g, rw_ln_b,
           gd_conv, gd_a_log, gd_dt_bias, gd_norm_g,
           rg_conv, rg_conv_b, rg_wa, rg_ba, rg_wx, rg_bx, rg_lambda,
           mlp_w1, mlp_w2, final_norm_g):
    bsz, seq, _ = x.shape
    ctx_len = ctx.shape[1]
    depth = ada_w.shape[0]
    tables_x = _dft_tables(seq)
    tables_c = _dft_tables(ctx_len)
    cond8 = jnp.concatenate([c, c_ctx[None, :], jnp.zeros((8 - bsz - 1, D_MODEL), F32)], axis=0)
    for l in range(depth):
        last = l == depth - 1
        mod = _modulation(cond8, ada_w[l], ada_b[l])
        mod_x = mod[:bsz].reshape(bsz, 6, D_MODEL)
        mod_c = mod[bsz:bsz + 1].reshape(1, 6, D_MODEL)
        w_in_bf = _permute_w_in(w_in[l])
        w_out_bf = w_out[l].astype(BF16)
        w1_bf = mlp_w1[l].astype(BF16)
        w2_bf = mlp_w2[l].astype(BF16)
        px = _inproj(x, norm_mix_g[l], mod_x, w_in_bf)
        pc = _inproj(ctx, norm_mix_g[l], mod_c, w_in_bf)
        hy_prm = (hy_conv[l], hy_w1[l], hy_b1[l], hy_w2[l], hy_b2[l], hy_w3[l], hy_freq[l], hy_skip[l])
        rw_prm = (rw_mu[l], rw_w0[l], rw_w_up[l], rw_a0[l], rw_a_up[l], rw_g_up[l],
                  rw_k_k[l], rw_k_a[l], rw_r_k[l], rw_ln_g[l], rw_ln_b[l])
        gd_prm = (gd_conv[l], gd_a_log[l], gd_dt_bias[l], gd_norm_g[l])
        rg_prm = (rg_conv[l], rg_conv_b[l], rg_wa[l], rg_ba[l], rg_wx[l], rg_bx[l], rg_lambda[l])
        c_rw, s_rw = _rwkv_mixer(pc, rw_prm, None)
        c_gd, s_gd = _gdn_mixer(pc, gd_prm, None, ctx_len)
        c_rg, s_rg = _rglru_mixer(pc, rg_prm, None, ctx_len)
        x_hy = _hyena_mixer(px, hy_prm, tables_x, GRID_W)
        x_rw, _ = _rwkv_mixer(px, rw_prm, s_rw)
        x_gd, _ = _gdn_mixer(px, gd_prm, s_gd, GRID_W)
        x_rg, _ = _rglru_mixer(px, rg_prm, s_rg, GRID_W)
        x_new = _outproj(x, mod_x, (x_hy, x_rw, x_gd, x_rg), w_out_bf)
        x_new = _mlp(x_new, norm_mlp_g[l], mod_x, w1_bf, w2_bf, final_norm_g if last else None)
        if not last:
            c_hy = _hyena_mixer(pc, hy_prm, tables_c, ctx_len)
            ctx_new = _outproj(ctx, mod_c, (c_hy, c_rw, c_gd, c_rg), w_out_bf)
            ctx = _mlp(ctx_new, norm_mlp_g[l], mod_c, w1_bf, w2_bf, None)
        x = x_new
    return x
```

```python
import functools
import math

import jax
import jax.numpy as jnp
from jax import lax
from jax.experimental import pallas as pl
from jax.experimental.pallas import tpu as pltpu

F32 = jnp.float32
BF16 = jnp.bfloat16
HIGHEST = lax.Precision.HIGHEST

D_MODEL = 2048
GRID_W = 64
HY_W = RW_W = GD_W = RG_W = 512
D_FF = 4 * D_MODEL
NORM_EPS = 1e-6
HY_EMB = 33
HY_BANDS = 16
RW_HEAD = 64
RW_GN_EPS = 64e-5
GD_HEAD = 128
GD_HEADS = 4
CHUNK = 64
RG_C = 8.0
LANES = 128

COL_HY = 0
COL_RW = 1536
COL_GD = 3072
COL_RG = 5120
COL_RW_LORA = 6144
COL_GD_GB = 6400
N_PROJ = 6528

VMEM_LIMIT = 56 * 1024 * 1024


def _cparams(n_axes):
    return pltpu.CompilerParams(dimension_semantics=("arbitrary",) * n_axes,
                                vmem_limit_bytes=VMEM_LIMIT)


def _dot(a, b):
    return jnp.dot(a.astype(BF16), b.astype(BF16), preferred_element_type=F32)


def _dot_nt(a, b):
    return lax.dot_general(a.astype(BF16), b.astype(BF16), (((1,), (1,)), ((), ())),
                           preferred_element_type=F32)


def _dot_tn(a, b):
    return lax.dot_general(a.astype(BF16), b.astype(BF16), (((0,), (0,)), ((), ())),
                           preferred_element_type=F32)


def _dot_hp(a, b):
    return jnp.dot(a, b, precision=HIGHEST, preferred_element_type=F32)


def _dot_nt_hp(a, b):
    return lax.dot_general(a, b, (((1,), (1,)), ((), ())), precision=HIGHEST,
                           preferred_element_type=F32)


def _sigmoid(x):
    return 1.0 / (1.0 + jnp.exp(-x))


def _silu(x):
    return x * _sigmoid(x)


def _softplus(x):
    return jnp.maximum(x, 0.0) + jnp.log1p(jnp.exp(-jnp.abs(x)))


def _short_conv(u, w, row_len, pad_left):
    t_len = u.shape[0]
    pos = lax.broadcasted_iota(jnp.int32, u.shape, 0) & (row_len - 1)
    y = None
    for j in range(w.shape[0]):
        off = j - pad_left
        if off == 0:
            term = u * w[j:j + 1, :]
        else:
            sh = pltpu.roll(u, (-off) % t_len, 0)
            ok = (pos + off >= 0) & (pos + off < row_len)
            term = jnp.where(ok, sh, 0.0) * w[j:j + 1, :]
        y = term if y is None else y + term
    return y


def _mod_kernel(c_ref, w_ref, b_ref, o_ref):
    o_ref[...] = _dot(_silu(c_ref[...]), w_ref[...]) + b_ref[...]


def _modulation(cond8, ada_w, ada_b):
    n = ada_w.shape[1]
    tn = 512
    return pl.pallas_call(
        _mod_kernel,
        out_shape=jax.ShapeDtypeStruct((8, n), F32),
        grid=(n // tn,),
        in_specs=[pl.BlockSpec((8, D_MODEL), lambda j: (0, 0)),
                  pl.BlockSpec((D_MODEL, tn), lambda j: (0, j)),
                  pl.BlockSpec((1, tn), lambda j: (0, j))],
        out_specs=pl.BlockSpec((8, tn), lambda j: (0, j)),
        compiler_params=_cparams(1),
        name="adaln_mod",
    )(cond8, ada_w, ada_b.reshape(1, n))


def _norm_mod(x, g, shift, scale):
    y = x * lax.rsqrt(jnp.mean(x * x, axis=-1, keepdims=True) + NORM_EPS) * g
    return y * (1.0 + scale) + shift


def _inproj_kernel(x_ref, g_ref, mod_ref, w_ref, o_ref, h_scr):
    @pl.when(pl.program_id(2) == 0)
    def _():
        h = _norm_mod(x_ref[0], g_ref[...], mod_ref[0, 0:1, :], mod_ref[0, 1:2, :])
        h_scr[...] = h.astype(BF16)

    o_ref[0] = jnp.dot(h_scr[...], w_ref[...], preferred_element_type=F32)


def _inproj(x, g, mod, w_bf):
    bsz, length, _ = x.shape
    tm = min(length, 1024)
    tn = 384
    per_batch = mod.shape[0] == bsz
    return pl.pallas_call(
        _inproj_kernel,
        out_shape=jax.ShapeDtypeStruct((bsz, length, N_PROJ), F32),
        grid=(bsz, length // tm, N_PROJ // tn),
        in_specs=[pl.BlockSpec((1, tm, D_MODEL), lambda b, i, j: (b, i, 0)),
                  pl.BlockSpec((1, D_MODEL), lambda b, i, j: (0, 0)),
                  pl.BlockSpec((1, 6, D_MODEL), (lambda b, i, j: (b, 0, 0)) if per_batch
                               else (lambda b, i, j: (0, 0, 0))),
                  pl.BlockSpec((D_MODEL, tn), lambda b, i, j: (0, j))],
        out_specs=pl.BlockSpec((1, tm, tn), lambda b, i, j: (b, i, j)),
        scratch_shapes=[pltpu.VMEM((tm, D_MODEL), BF16)],
        compiler_params=_cparams(3),
        name="inproj",
    )(x, g.reshape(1, D_MODEL), mod, w_bf)


def _outproj_kernel(x_ref, mod_ref, m0_ref, m1_ref, m2_ref, m3_ref, w_ref, o_ref):
    acc = jnp.dot(m0_ref[0], w_ref[0:512, :], preferred_element_type=F32)
    acc += jnp.dot(m1_ref[0], w_ref[512:1024, :], preferred_element_type=F32)
    acc += jnp.dot(m2_ref[0], w_ref[1024:1536, :], preferred_element_type=F32)
    acc += jnp.dot(m3_ref[0], w_ref[1536:2048, :], preferred_element_type=F32)
    o_ref[0] = x_ref[0] + mod_ref[0, 2:3, :] * acc


def _outproj(x, mod, mixers, w_bf):
    bsz, length, _ = x.shape
    tm = min(length, 512)
    per_batch = mod.shape[0] == bsz
    mspec = pl.BlockSpec((1, tm, 512), lambda b, i: (b, i, 0))
    return pl.pallas_call(
        _outproj_kernel,
        out_shape=jax.ShapeDtypeStruct((bsz, length, D_MODEL), F32),
        grid=(bsz, length // tm),
        in_specs=[pl.BlockSpec((1, tm, D_MODEL), lambda b, i: (b, i, 0)),
                  pl.BlockSpec((1, 6, D_MODEL), (lambda b, i: (b, 0, 0)) if per_batch
                               else (lambda b, i: (0, 0, 0))),
                  mspec, mspec, mspec, mspec,
                  pl.BlockSpec((D_MODEL, D_MODEL), lambda b, i: (0, 0))],
        out_specs=pl.BlockSpec((1, tm, D_MODEL), lambda b, i: (b, i, 0)),
        compiler_params=_cparams(2),
        name="outproj",
    )(x, mod, *mixers, w_bf)


def _mlp_kernel(x_ref, g_ref, mod_ref, w1_ref, w2_ref, fg_ref, o_ref, h_scr, acc_scr, *, final_norm):
    f = pl.program_id(2)

    @pl.when(f == 0)
    def _():
        h = _norm_mod(x_ref[0], g_ref[...], mod_ref[0, 3:4, :], mod_ref[0, 4:5, :])
        h_scr[...] = h.astype(BF16)
        acc_scr[...] = jnp.zeros_like(acc_scr)

    a = jnp.dot(h_scr[...], w1_ref[...], preferred_element_type=F32)
    a = jnp.square(jnp.maximum(a, 0.0)).astype(BF16)
    acc_scr[...] += jnp.dot(a, w2_ref[...], preferred_element_type=F32)

    @pl.when(f == pl.num_programs(2) - 1)
    def _():
        y = x_ref[0] + mod_ref[0, 5:6, :] * acc_scr[...]
        if final_norm:
            y = y * lax.rsqrt(jnp.mean(y * y, axis=-1, keepdims=True) + NORM_EPS) * fg_ref[...]
        o_ref[0] = y


def _mlp(x, g, mod, w1_bf, w2_bf, final_g):
    bsz, length, _ = x.shape
    tm = min(length, 512)
    tf = 1024
    per_batch = mod.shape[0] == bsz
    final_norm = final_g is not None
    fg = (final_g if final_norm else jnp.ones((D_MODEL,), F32)).reshape(1, D_MODEL)
    return pl.pallas_call(
        functools.partial(_mlp_kernel, final_norm=final_norm),
        out_shape=jax.ShapeDtypeStruct((bsz, length, D_MODEL), F32),
        grid=(bsz, length // tm, D_FF // tf),
        in_specs=[pl.BlockSpec((1, tm, D_MODEL), lambda b, i, f: (b, i, 0)),
                  pl.BlockSpec((1, D_MODEL), lambda b, i, f: (0, 0)),
                  pl.BlockSpec((1, 6, D_MODEL), (lambda b, i, f: (b, 0, 0)) if per_batch
                               else (lambda b, i, f: (0, 0, 0))),
                  pl.BlockSpec((D_MODEL, tf), lambda b, i, f: (0, f)),
                  pl.BlockSpec((tf, D_MODEL), lambda b, i, f: (f, 0)),
                  pl.BlockSpec((1, D_MODEL), lambda b, i, f: (0, 0))],
        out_specs=pl.BlockSpec((1, tm, D_MODEL), lambda b, i, f: (b, i, 0)),
        scratch_shapes=[pltpu.VMEM((tm, D_MODEL), BF16), pltpu.VMEM((tm, D_MODEL), F32)],
        compiler_params=_cparams(3),
        name="mlp",
    )(x, g.reshape(1, D_MODEL), mod, w1_bf, w2_bf, fg)


def _lin_scan(a, b, rev):
    t_len = a.shape[0]
    row = lax.broadcasted_iota(jnp.int32, a.shape, 0)
    s = 1
    while s < t_len:
        if rev:
            a_sh = pltpu.roll(a, t_len - s, 0)
            b_sh = pltpu.roll(b, t_len - s, 0)
            ok = row < t_len - s
        else:
            a_sh = pltpu.roll(a, s, 0)
            b_sh = pltpu.roll(b, s, 0)
            ok = row >= s
        b = a * jnp.where(ok, b_sh, 0.0) + b
        a = a * jnp.where(ok, a_sh, 1.0)
        s *= 2
    return a, b


def _rglru_kernel(*refs, rev, row_len, final):
    if final:
        (x_ref, gate_ref, cw_ref, cb_ref, wa_ref, ba_ref, wx_ref, bx_ref, lam_ref, h0_ref, prev_ref,
         o_ref, st_ref, carry) = refs
    else:
        (x_ref, cw_ref, cb_ref, wa_ref, ba_ref, wx_ref, bx_ref, lam_ref, h0_ref,
         o_ref, st_ref, carry) = refs
    i = pl.program_id(2)

    @pl.when(i == 0)
    def _():
        carry[...] = h0_ref[0]

    xc = _short_conv(x_ref[0], cw_ref[...], row_len, 2) + cb_ref[...]
    gate_r = _sigmoid(_dot(xc, wa_ref[0]) + ba_ref[...])
    gate_i = _sigmoid(_dot(xc, wx_ref[0]) + bx_ref[...])
    log_a = -RG_C * gate_r * _softplus(-lam_ref[...])
    a = jnp.exp(log_a)
    b = jnp.sqrt(-jnp.tanh(log_a) * (a * a + 1.0)) * (gate_i * xc)
    a_cum, b_cum = _lin_scan(a, b, rev)
    h = a_cum * carry[...] + b_cum
    t_len = h.shape[0]
    last = h[0:1, :] if rev else h[t_len - 1:t_len, :]
    carry[...] = last

    @pl.when(i == pl.num_programs(2) - 1)
    def _():
        st_ref[0] = last

    if final:
        gate = gate_ref[0]
        gelu = 0.5 * gate * (1.0 + jnp.tanh(math.sqrt(2.0 / math.pi) * (gate + 0.044715 * gate * gate * gate)))
        o_ref[0] = (gelu * (prev_ref[0] + h)).astype(o_ref.dtype)
    else:
        o_ref[0] = h


def _rglru_pass(p, prm, d, h0, prev, row_len):
    bsz, length, _ = p.shape
    conv_w, conv_b, wa, ba, wx, bx, lam = prm
    final = prev is not None
    rev = d == 1
    t = min(length, 256)
    nt = length // t
    tidx = (lambda i: nt - 1 - i) if rev else (lambda i: i)
    cx = COL_RG // LANES
    col = lambda c0: pl.BlockSpec((1, t, LANES), lambda b, n, i: (b, tidx(i), c0 + n))
    vec = pl.BlockSpec((1, LANES), lambda b, n, i: (0, n))
    mat = pl.BlockSpec((1, LANES, LANES), lambda b, n, i: (n, 0, 0))
    in_specs = [col(cx)]
    args = [p]
    if final:
        in_specs.append(col(cx + 4))
        args.append(p)
    in_specs += [pl.BlockSpec((4, LANES), lambda b, n, i: (0, n)), vec, mat, vec, mat, vec, vec,
                 pl.BlockSpec((1, 1, LANES), lambda b, n, i: (b, 0, n))]
    args += [conv_w, conv_b.reshape(1, RG_W), wa[d].astype(BF16), ba[d].reshape(1, RG_W),
             wx[d].astype(BF16), bx[d].reshape(1, RG_W), lam[d].reshape(1, RG_W), h0]
    if final:
        in_specs.append(pl.BlockSpec((1, t, LANES), lambda b, n, i: (b, tidx(i), n)))
        args.append(prev)
    out, st = pl.pallas_call(
        functools.partial(_rglru_kernel, rev=rev, row_len=row_len, final=final),
        out_shape=[jax.ShapeDtypeStruct((bsz, length, RG_W), BF16 if final else F32),
                   jax.ShapeDtypeStruct((bsz, 1, RG_W), F32)],
        grid=(bsz, 4, nt),
        in_specs=in_specs,
        out_specs=[pl.BlockSpec((1, t, LANES), lambda b, n, i: (b, tidx(i), n)),
                   pl.BlockSpec((1, 1, LANES), lambda b, n, i: (b, 0, n))],
        scratch_shapes=[pltpu.VMEM((1, LANES), F32)],
        compiler_params=_cparams(3),
        name="rglru_bwd" if rev else "rglru_fwd",
    )(*args)
    return out, st


def _rglru_mixer(p, prm, h0s, row_len):
    bsz = p.shape[0]
    if h0s is None:
        h0s = [jnp.zeros((bsz, 1, RG_W), F32)] * 2
    h_f, s_f = _rglru_pass(p, prm, 0, h0s[0], None, row_len)
    out, s_b = _rglru_pass(p, prm, 1, h0s[1], h_f, row_len)
    return out, [s_f, s_b]


def _tri_masks(n, rev):
    row = lax.broadcasted_iota(jnp.int32, (n, n), 0)
    col = lax.broadcasted_iota(jnp.int32, (n, n), 1)
    if rev:
        return row <= col, row < col
    return row >= col, row > col


def _dot_x3(a, b):
    a_hi = a.astype(BF16)
    a_lo = (a - a_hi.astype(F32)).astype(BF16)
    b_hi = b.astype(BF16)
    b_lo = (b - b_hi.astype(F32)).astype(BF16)
    mm = lambda p, q: jnp.dot(p, q, preferred_element_type=F32)
    return mm(a_hi, b_hi) + (mm(a_hi, b_lo) + mm(a_lo, b_hi))


def _unit_solve(neg_n, rhs, steps, dot):
    x = rhs
    m = neg_n
    for s in range(steps):
        x = x + dot(m, x)
        if s + 1 < steps:
            m = dot(m, m)
    return x


def _chunk_cumsum(x, rev):
    t_len = x.shape[0]
    pos = lax.broadcasted_iota(jnp.int32, x.shape, 0) & (CHUNK - 1)
    s = 1
    while s < CHUNK:
        if rev:
            x = x + jnp.where(pos < CHUNK - s, pltpu.roll(x, t_len - s, 0), 0.0)
        else:
            x = x + jnp.where(pos >= s, pltpu.roll(x, s, 0), 0.0)
        s *= 2
    return x


def _lane_form(x):
    sel = (lax.broadcasted_iota(jnp.int32, x.shape, 1) == 0).astype(BF16)
    p1 = x.astype(BF16)
    r1 = x - p1.astype(F32)
    p2 = r1.astype(BF16)
    p3 = (r1 - p2.astype(F32)).astype(BF16)
    nt = lambda p: lax.dot_general(sel, p, (((1,), (1,)), ((), ())), preferred_element_type=F32)
    return nt(p1) + nt(p2) + nt(p3)


def _dot_exact_rhs(a, b_exact):
    a_hi = a.astype(BF16)
    a_lo = (a - a_hi.astype(F32)).astype(BF16)
    b = b_exact.astype(BF16)
    return (jnp.dot(a_hi, b, preferred_element_type=F32) + jnp.dot(a_lo, b, preferred_element_type=F32))


def _gdn_kernel(*refs, rev, row_len, final, d):
    if final:
        (q_ref, k_ref, v_ref, z_ref, gb_ref, cq_ref, ck_ref, cv_ref, alog_ref, dtb_ref, ng_ref, s0_ref,
         prev_ref, o_ref, st_ref, state) = refs
    else:
        (q_ref, k_ref, v_ref, gb_ref, cq_ref, ck_ref, cv_ref, alog_ref, dtb_ref, ng_ref, s0_ref,
         o_ref, st_ref, state) = refs
    grp = pl.program_id(1)
    i = pl.program_id(2)
    n_heads = q_ref.shape[2] // GD_HEAD

    @pl.when(i == 0)
    def _():
        state[...] = s0_ref[0]

    qc = _silu(_short_conv(q_ref[0], cq_ref[...], row_len, 2))
    kc = _silu(_short_conv(k_ref[0], ck_ref[...], row_len, 2))
    vc = _silu(_short_conv(v_ref[0], cv_ref[...], row_len, 2))
    gbb = gb_ref[0]
    lane = lax.broadcasted_iota(jnp.int32, gbb.shape, 1)
    t_len = qc.shape[0]
    n_chunks = t_len // CHUNK
    incl, strict = _tri_masks(CHUNK, rev)
    order = list(range(n_chunks - 1, -1, -1) if rev else range(n_chunks))
    bf = lambda x: x.astype(BF16)

    heads = []
    for hh in range(n_heads):
        hs = slice(hh * GD_HEAD, (hh + 1) * GD_HEAD)
        head = grp * n_heads + hh
        q_h, k_h = qc[:, hs], kc[:, hs]
        q_h = q_h * lax.rsqrt(jnp.sum(q_h * q_h, axis=-1, keepdims=True) + 1e-6) * (GD_HEAD ** -0.5)
        k_h = k_h * lax.rsqrt(jnp.sum(k_h * k_h, axis=-1, keepdims=True) + 1e-6)
        g_raw = jnp.sum(jnp.where(lane == d * GD_HEADS + head, gbb, 0.0), axis=-1, keepdims=True)
        b_raw = jnp.sum(jnp.where(lane == (2 + d) * GD_HEADS + head, gbb, 0.0), axis=-1, keepdims=True)
        g = -jnp.exp(alog_ref[hh]) * _softplus(g_raw + dtb_ref[hh])
        heads.append(dict(q=q_h, k=k_h, v=vc[:, hs], beta=_sigmoid(b_raw), gc=_chunk_cumsum(g, rev)))

    jobs = [(hh, c) for c in order for hh in range(n_heads)]
    pre = {}
    for hh, c in jobs:
        hd = heads[hh]
        sl = slice(c * CHUNK, (c + 1) * CHUNK)
        q_c, k_c, v_c, b_c, gc = hd["q"][sl], hd["k"][sl], hd["v"][sl], hd["beta"][sl], hd["gc"][sl]
        g_row = _lane_form(gc)
        decay_in = jnp.exp(jnp.where(incl, gc[:, :CHUNK] - g_row, -jnp.inf))
        kb = k_c * b_c
        e_gc = jnp.exp(gc)
        g_last = gc[0:1, :] if rev else gc[CHUNK - 1:CHUNK, :]
        pre[hh, c] = dict(neg_n=-(_dot_nt(kb, k_c) * jnp.where(strict, decay_in, 0.0)),
                          rhs=jnp.concatenate([v_c * b_c, kb * e_gc], axis=-1),
                          qg=bf(q_c * e_gc), k_dec=bf(k_c * jnp.exp(g_last - gc)),
                          a_qk=bf(_dot_nt(q_c, k_c) * decay_in), dec=jnp.exp(g_last))

    xs = {j: pre[j]["rhs"] for j in jobs}
    ms = {j: pre[j]["neg_n"] for j in jobs}
    for s in range(6):
        xs = {j: xs[j] + _dot_x3(ms[j], xs[j]) for j in jobs}
        if s < 5:
            ms = {j: _dot_x3(ms[j], ms[j]) for j in jobs}

    s_cur = [state[hh] for hh in range(n_heads)]
    os = {}
    for c in order:
        for hh in range(n_heads):
            p, x = pre[hh, c], xs[hh, c]
            t2 = _dot(jnp.concatenate([bf(x[:, GD_HEAD:]), p["qg"]], axis=0), s_cur[hh])
            v_new = bf(x[:, :GD_HEAD] - t2[:CHUNK])
            os[hh, c] = t2[CHUNK:] + _dot(p["a_qk"], v_new)
            s_cur[hh] = s_cur[hh] * p["dec"] + _dot_tn(p["k_dec"], v_new)

    for hh in range(n_heads):
        hs = slice(hh * GD_HEAD, (hh + 1) * GD_HEAD)
        o_h = jnp.concatenate([os[hh, c] for c in range(n_chunks)], axis=0)
        if final:
            o_t = prev_ref[0, :, hs] + o_h
            o_t = o_t * lax.rsqrt(jnp.mean(o_t * o_t, axis=-1, keepdims=True) + NORM_EPS) * ng_ref[...]
            o_ref[0, :, hs] = (o_t * _silu(z_ref[0, :, hs])).astype(o_ref.dtype)
        else:
            o_ref[0, :, hs] = o_h
        state[hh] = s_cur[hh]

    @pl.when(i == pl.num_programs(2) - 1)
    def _():
        for hh in range(n_heads):
            st_ref[0, hh] = s_cur[hh]


GD_HEADS_PER_STEP = 2


def _gdn_pass(p, prm, d, s0, prev, row_len):
    bsz, length, _ = p.shape
    conv_w, a_log, dt_bias, norm_g = prm
    final = prev is not None
    rev = d == 1
    t = min(length, 256)
    nt = length // t
    tidx = (lambda i: nt - 1 - i) if rev else (lambda i: i)
    nh = GD_HEADS_PER_STEP
    wd = nh * GD_HEAD
    ng = GD_W // wd
    c0 = COL_GD // wd
    col = lambda off: pl.BlockSpec((1, t, wd), lambda b, h, i: (b, tidx(i), c0 + off * ng + h))
    cw = lambda off: pl.BlockSpec((4, wd), lambda b, h, i: (0, off * ng + h))
    hvec = pl.BlockSpec((nh, 1, LANES), lambda b, h, i: (h, 0, 0))
    in_specs = [col(0), col(1), col(2)]
    args = [p, p, p]
    if final:
        in_specs.append(col(3))
        args.append(p)
    in_specs += [pl.BlockSpec((1, t, LANES), lambda b, h, i: (b, tidx(i), COL_GD_GB // LANES)),
                 cw(0), cw(1), cw(2), hvec, hvec,
                 pl.BlockSpec((1, LANES), lambda b, h, i: (0, 0)),
                 pl.BlockSpec((1, nh, GD_HEAD, GD_HEAD), lambda b, h, i: (b, h, 0, 0))]
    bcast = lambda v: jnp.broadcast_to(v.reshape(GD_HEADS, 1, 1), (GD_HEADS, 1, LANES))
    args += [p, conv_w, conv_w, conv_w, bcast(a_log[d]), bcast(dt_bias[d]), norm_g.reshape(1, GD_HEAD), s0]
    if final:
        in_specs.append(pl.BlockSpec((1, t, wd), lambda b, h, i: (b, tidx(i), h)))
        args.append(prev)
    out, st = pl.pallas_call(
        functools.partial(_gdn_kernel, rev=rev, row_len=row_len, final=final, d=d),
        out_shape=[jax.ShapeDtypeStruct((bsz, length, GD_W), BF16 if final else F32),
                   jax.ShapeDtypeStruct((bsz, GD_HEADS, GD_HEAD, GD_HEAD), F32)],
        grid=(bsz, ng, nt),
        in_specs=in_specs,
        out_specs=[pl.BlockSpec((1, t, wd), lambda b, h, i: (b, tidx(i), h)),
                   pl.BlockSpec((1, nh, GD_HEAD, GD_HEAD), lambda b, h, i: (b, h, 0, 0))],
        scratch_shapes=[pltpu.VMEM((nh, GD_HEAD, GD_HEAD), F32)],
        compiler_params=_cparams(3),
        name="gdn_bwd" if rev else "gdn_fwd",
    )(*args)
    return out, st


def _gdn_mixer(p, prm, s0s, row_len):
    bsz = p.shape[0]
    if s0s is None:
        s0s = [jnp.zeros((bsz, GD_HEADS, GD_HEAD, GD_HEAD), F32)] * 2
    o_f, s_f = _gdn_pass(p, prm, 0, s0s[0], None, row_len)
    out, s_b = _gdn_pass(p, prm, 1, s0s[1], o_f, row_len)
    return out, [s_f, s_b]


def _stack_heads(x, lo):
    return jnp.concatenate([jnp.where(lo, x, 0.0), jnp.where(lo, 0.0, x)], axis=0)


def _rwkv_kernel(*refs, rev, final):
    if final:
        (r_ref, k_ref, v_ref, lo_ref, mur_ref, muk_ref, muv_ref, mul_ref, w0_ref, wup_ref, a0_ref, aup_ref,
         gup_ref, kk_ref, ka_ref, rk_ref, lng_ref, lnb_ref, s0_ref, prev_ref,
         o_ref, st_ref, state, c_r, c_k, c_v, c_l) = refs
    else:
        (r_ref, k_ref, v_ref, lo_ref, mur_ref, muk_ref, muv_ref, mul_ref, w0_ref, wup_ref, a0_ref, aup_ref,
         gup_ref, kk_ref, ka_ref, rk_ref, lng_ref, lnb_ref, s0_ref,
         o_ref, st_ref, state, c_r, c_k, c_v, c_l) = refs
    i = pl.program_id(2)

    @pl.when(i == 0)
    def _():
        state[...] = s0_ref[0]
        c_r[...] = jnp.zeros_like(c_r)
        c_k[...] = jnp.zeros_like(c_k)
        c_v[...] = jnp.zeros_like(c_v)
        c_l[...] = jnp.zeros_like(c_l)

    t_len = r_ref.shape[1]

    def shifted(x_ref, carry, mu_ref):
        x = x_ref[0]
        row = lax.broadcasted_iota(jnp.int32, x.shape, 0)
        if rev:
            prev = jnp.where(row == t_len - 1, carry[...], pltpu.roll(x, t_len - 1, 0))
            carry[...] = x[0:1, :]
        else:
            prev = jnp.where(row == 0, carry[...], pltpu.roll(x, 1, 0))
            carry[...] = x[t_len - 1:t_len, :]
        return x + (prev - x) * mu_ref[...]

    r = shifted(r_ref, c_r, mur_ref)
    k = shifted(k_ref, c_k, muk_ref)
    v = shifted(v_ref, c_v, muv_ref)
    lora = shifted(lo_ref, c_l, mul_ref)
    xw, xa, xg = lora[:, 0:64], lora[:, 64:128], lora[:, 128:256]

    width = r_ref.shape[2]
    n_pairs = width // LANES
    lane = lax.broadcasted_iota(jnp.int32, (1, LANES), 1)
    lo = lane < RW_HEAD
    head_sum = (lax.broadcasted_iota(jnp.int32, (width, width), 0) // RW_HEAD
                == lax.broadcasted_iota(jnp.int32, (width, width), 1) // RW_HEAD).astype(F32)
    rowh = lax.broadcasted_iota(jnp.int32, (LANES, LANES), 0) // RW_HEAD
    colh = lax.broadcasted_iota(jnp.int32, (LANES, LANES), 1) // RW_HEAD

    lw = -math.exp(-0.5) * _sigmoid(w0_ref[...] + _dot(jnp.tanh(xw), wup_ref[...]))
    a = _sigmoid(a0_ref[...] + _dot(xa, aup_ref[...]))
    kk = k * kk_ref[...]
    kappa = kk / jnp.maximum(jnp.sqrt(_dot_exact_rhs(kk * kk, head_sum)), 1e-12)
    kt = k * (1.0 + (a - 1.0) * ka_ref[...])
    gate = _dot(_sigmoid(xg), gup_ref[...])
    bonus = _dot_exact_rhs(r * kt * rk_ref[...], head_sum) * v

    n_chunks = t_len // CHUNK
    lg_all = _chunk_cumsum(lw, rev)
    rowt = lax.broadcasted_iota(jnp.int32, (LANES, LANES), 0) & (CHUNK - 1)
    colt = lax.broadcasted_iota(jnp.int32, (LANES, LANES), 1) & (CHUNK - 1)
    same = rowh == colh
    incl2 = ((rowt <= colt) if rev else (rowt >= colt)) & same
    strict2 = ((rowt < colt) if rev else (rowt > colt)) & same
    order = list(range(n_chunks - 1, -1, -1) if rev else range(n_chunks))
    bf = lambda x: x.astype(BF16)

    jobs = [(pp, c) for c in order for pp in range(n_pairs)]
    pre = {}
    for pp, c in jobs:
        sl = (slice(c * CHUNK, (c + 1) * CHUNK), slice(pp * LANES, (pp + 1) * LANES))
        lw_c, kap_c, a_c = lw[sl], kappa[sl], a[sl]
        lg = lg_all[sl]
        lg_tot = lg[0:1, :] if rev else lg[CHUNK - 1:CHUNK, :]
        e_neg = jnp.exp(-lg)
        e_rem = jnp.exp(lg_tot - lg)
        p_raw = -(kap_c * a_c)
        q2 = bf(_stack_heads(kap_c * jnp.exp(lg - lw_c), lo))
        p2 = bf(_stack_heads(p_raw * e_neg, lo))
        k2 = bf(_stack_heads(kt[sl] * e_neg, lo))
        r2 = bf(_stack_heads(r[sl] * jnp.exp(lg), lo))
        v2 = bf(_stack_heads(v[sl], lo))
        pt2 = bf(_stack_heads(p_raw * e_rem, lo))
        kt2 = bf(_stack_heads(kt[sl] * e_rem, lo))
        a_qp = jnp.where(strict2, _dot_nt(q2, p2), 0.0)
        a_qk = jnp.where(strict2, _dot_nt(q2, k2), 0.0)
        a_rp = bf(jnp.where(incl2, _dot_nt(r2, p2), 0.0))
        a_rk = jnp.where(incl2, _dot_nt(r2, k2), 0.0)
        pre[pp, c] = dict(q2=q2, r2=r2, pt2=pt2, a_qp=a_qp, a_rp=a_rp,
                          rhs=jnp.concatenate([q2.astype(F32), _dot(a_qk, v2)], axis=-1),
                          y_loc=_dot(a_rk, v2), s_loc=_dot_tn(v2, kt2), dec=jnp.exp(lg_tot))

    xs = {j: pre[j]["rhs"] for j in jobs}
    ms = {j: pre[j]["a_qp"] for j in jobs}
    for s in range(6):
        xs = {j: xs[j] + _dot(ms[j], xs[j]) for j in jobs}
        if s < 5:
            ms = {j: _dot(ms[j], ms[j]) for j in jobs}

    s_cur = [state[pp] for pp in range(n_pairs)]
    ys = {}
    for c in order:
        for pp in range(n_pairs):
            p, x = pre[pp, c], xs[pp, c]
            t2 = _dot_nt(jnp.concatenate([bf(x[:, :LANES]), p["r2"]], axis=0), s_cur[pp])
            u2 = bf(t2[:LANES] + x[:, LANES:])
            y2 = t2[LANES:] + _dot(p["a_rp"], u2) + p["y_loc"]
            s_cur[pp] = s_cur[pp] * p["dec"] + _dot_tn(u2, p["pt2"]) + p["s_loc"]
            ys[pp, c] = y2[:CHUNK] + y2[CHUNK:]

    y = jnp.concatenate([jnp.concatenate([ys[pp, c] for c in range(n_chunks)], axis=0)
                         for pp in range(n_pairs)], axis=1)
    mean = _dot_exact_rhs(y, head_sum) * (1.0 / RW_HEAD)
    yc = y - mean
    var = _dot_exact_rhs(yc * yc, head_sum) * (1.0 / RW_HEAD)
    yn = yc * lax.rsqrt(var + RW_GN_EPS) * lng_ref[...] + lnb_ref[...]
    out = (yn + bonus) * gate
    if final:
        o_ref[0] = (prev_ref[0] + out).astype(o_ref.dtype)
    else:
        o_ref[0] = out
    for pp in range(n_pairs):
        state[pp] = s_cur[pp]

    @pl.when(i == pl.num_programs(2) - 1)
    def _():
        for pp in range(n_pairs):
            st_ref[0, pp] = s_cur[pp]


RW_PAIRS_PER_STEP = 2


def _rwkv_pass(p, prm, d, s0, prev):
    bsz, length, _ = p.shape
    mu, w0, w_up, a0, a_up, g_up, k_k, k_a, r_k, ln_g, ln_b = [t[d] for t in prm]
    final = prev is not None
    rev = d == 1
    t = min(length, 256)
    nt = length // t
    tidx = (lambda i: nt - 1 - i) if rev else (lambda i: i)
    npp = RW_PAIRS_PER_STEP
    wd = npp * LANES
    c0 = COL_RW // wd
    ng = RW_W // wd
    col = lambda off: pl.BlockSpec((1, t, wd), lambda b, n, i: (b, tidx(i), c0 + off * ng + n))
    vec = pl.BlockSpec((1, wd), lambda b, n, i: (0, n))
    whole = lambda shape: pl.BlockSpec(shape, lambda b, n, i: (0,) * len(shape))
    row = lambda x: x.reshape(1, -1)
    in_specs = [col(0), col(1), col(2),
                pl.BlockSpec((1, t, 256), lambda b, n, i: (b, tidx(i), COL_RW_LORA // 256)),
                vec, vec, vec, whole((1, 256)),
                vec, pl.BlockSpec((64, wd), lambda b, n, i: (0, n)),
                vec, pl.BlockSpec((64, wd), lambda b, n, i: (0, n)),
                pl.BlockSpec((LANES, wd), lambda b, n, i: (0, n)),
                vec, vec, vec, vec, vec,
                pl.BlockSpec((1, npp, LANES, LANES), lambda b, n, i: (b, n, 0, 0))]
    args = [p, p, p, p,
            row(mu[0:512]), row(mu[512:1024]), row(mu[1024:1536]), row(mu[1536:1792]),
            row(w0), w_up.astype(BF16), row(a0), a_up.astype(BF16), g_up.astype(BF16),
            row(k_k), row(k_a), row(r_k), row(ln_g), row(ln_b), s0]
    if final:
        in_specs.append(pl.BlockSpec((1, t, wd), lambda b, n, i: (b, tidx(i), n)))
        args.append(prev)
    out, st = pl.pallas_call(
        functools.partial(_rwkv_kernel, rev=rev, final=final),
        out_shape=[jax.ShapeDtypeStruct((bsz, length, RW_W), BF16 if final else F32),
                   jax.ShapeDtypeStruct((bsz, 4, LANES, LANES), F32)],
        grid=(bsz, ng, nt),
        in_specs=in_specs,
        out_specs=[pl.BlockSpec((1, t, wd), lambda b, n, i: (b, tidx(i), n)),
                   pl.BlockSpec((1, npp, LANES, LANES), lambda b, n, i: (b, n, 0, 0))],
        scratch_shapes=[pltpu.VMEM((npp, LANES, LANES), F32), pltpu.VMEM((1, wd), F32),
                        pltpu.VMEM((1, wd), F32), pltpu.VMEM((1, wd), F32),
                        pltpu.VMEM((1, 256), F32)],
        compiler_params=_cparams(3),
        name="rwkv_bwd" if rev else "rwkv_fwd",
    )(*args)
    return out, st


def _rwkv_mixer(p, prm, s0s):
    bsz = p.shape[0]
    if s0s is None:
        s0s = [jnp.zeros((bsz, 4, LANES, LANES), F32)] * 2
    o_f, s_f = _rwkv_pass(p, prm, 0, s0s[0], None)
    out, s_b = _rwkv_pass(p, prm, 1, s0s[1], o_f)
    return out, [s_f, s_b]


def _dft_tables(length):
    n = 2 * length
    nfp = -(-(length + 1) // LANES) * LANES
    kf = jnp.arange(nfp, dtype=jnp.int32)
    ang = (2.0 * math.pi / n) * ((kf[:, None] * jnp.arange(n, dtype=jnp.int32)[None, :]) % n).astype(F32)
    ok = (kf <= length)[:, None]
    fwd_c = jnp.where(ok, jnp.cos(ang), 0.0).astype(BF16)
    fwd_s = jnp.where(ok, -jnp.sin(ang), 0.0).astype(BF16)
    ang_i = (2.0 * math.pi / n) * ((jnp.arange(length, dtype=jnp.int32)[:, None] * kf[None, :]) % n).astype(F32)
    wk = jnp.where((kf == 0) | (kf == length), 1.0, 2.0) * (kf <= length) / n
    inv_c = (wk[None, :] * jnp.cos(ang_i)).astype(BF16)
    inv_s = (-wk[None, :] * jnp.sin(ang_i)).astype(BF16)
    return fwd_c, fwd_s, inv_c, inv_s


def _hyena_filter(length, w1, b1, w2, b2, w3, freq):
    t = jnp.arange(length, dtype=F32)
    z = t / max(length - 1, 1)
    bands = jnp.linspace(1e-4, HY_BANDS - 1, HY_BANDS, dtype=F32)
    ang = (2.0 * math.pi / length) * t[:, None] * bands[None, :]
    feat = jnp.concatenate([z[:, None], jnp.cos(ang), -jnp.sin(ang)], axis=-1)
    h = jnp.sin(freq[0] * (feat @ w1 + b1))
    h = jnp.sin(freq[1] * (h @ w2 + b2))
    h = (h @ w3).astype(F32)
    deltas = jnp.abs(jnp.linspace(math.log(1e-2) / 1.5, math.log(1e-2) / 0.3, HY_W, dtype=F32))
    h = h * jnp.exp(-z[:, None] * jnp.tile(deltas, 2)[None, :])
    h_fwd, h_bwd = h[:, :HY_W], h[:, HY_W:]
    filt = jnp.concatenate([h_fwd, jnp.zeros((1, HY_W), F32), h_bwd[:0:-1]], axis=0)
    return filt / jnp.sum(jnp.abs(filt), axis=0, keepdims=True)


def _hy_zin_kernel(v_ref, x1_ref, cw_ref, o_ref, *, row_len):
    vc = _short_conv(v_ref[0], cw_ref[:, 0:512], row_len, 1)
    x1c = _short_conv(x1_ref[0], cw_ref[:, 1024:1536], row_len, 1)
    o_ref[0] = (x1c * vc).astype(o_ref.dtype)


def _hy_zin(p, conv_w, row_len):
    bsz, length, _ = p.shape
    t = min(length, 256)
    return pl.pallas_call(
        functools.partial(_hy_zin_kernel, row_len=row_len),
        out_shape=jax.ShapeDtypeStruct((bsz, length, HY_W), BF16),
        grid=(bsz, length // t),
        in_specs=[pl.BlockSpec((1, t, 512), lambda b, i: (b, i, 0)),
                  pl.BlockSpec((1, t, 512), lambda b, i: (b, i, 2)),
                  pl.BlockSpec((3, 1536), lambda b, i: (0, 0))],
        out_specs=pl.BlockSpec((1, t, 512), lambda b, i: (b, i, 0)),
        compiler_params=_cparams(2),
        name="hyena_zin",
    )(p, p, conv_w)


def _dft_fwd_kernel(*refs, mult):
    if mult:
        c_ref, s_ref, z_ref, fr_ref, fi_ref, yr_ref, yi_ref = refs
    else:
        c_ref, s_ref, z_ref, yr_ref, yi_ref = refs
    z = z_ref[0].astype(BF16)
    zr = jnp.dot(c_ref[...], z, preferred_element_type=F32)
    zi = jnp.dot(s_ref[...], z, preferred_element_type=F32)
    if mult:
        fr, fi = fr_ref[...], fi_ref[...]
        zr, zi = zr * fr - zi * fi, zr * fi + zi * fr
    yr_ref[0] = zr.astype(yr_ref.dtype)
    yi_ref[0] = zi.astype(yi_ref.dtype)


def _dft_fwd(tab_c, tab_s, z, spec):
    bsz, klen, _ = z.shape
    nfp = tab_c.shape[0]
    mult = spec is not None
    tm = 384 if (mult and nfp % 384 == 0) else LANES
    in_specs = [pl.BlockSpec((tm, klen), lambda b, i: (i, 0)),
                pl.BlockSpec((tm, klen), lambda b, i: (i, 0)),
                pl.BlockSpec((1, klen, 512), lambda b, i: (b, 0, 0))]
    args = [tab_c, tab_s, z]
    if mult:
        in_specs += [pl.BlockSpec((tm, 512), lambda b, i: (i, 0))] * 2
        args += list(spec)
    odt = BF16 if mult else F32
    return pl.pallas_call(
        functools.partial(_dft_fwd_kernel, mult=mult),
        out_shape=[jax.ShapeDtypeStruct((bsz, nfp, 512), odt)] * 2,
        grid=(bsz, nfp // tm),
        in_specs=in_specs,
        out_specs=[pl.BlockSpec((1, tm, 512), lambda b, i: (b, i, 0))] * 2,
        compiler_params=_cparams(2),
        name="hyena_dft_mul" if mult else "hyena_dft_filter",
    )(*args)


def _dft_inv_kernel(ci_ref, si_ref, yr_ref, yi_ref, v_ref, x0_ref, x1_ref, cw_ref, skip_ref, o_ref, *, row_len):
    y = jnp.dot(ci_ref[...], yr_ref[0], preferred_element_type=F32)
    y += jnp.dot(si_ref[...], yi_ref[0], preferred_element_type=F32)
    vc = _short_conv(v_ref[0], cw_ref[:, 0:512], row_len, 1)
    x0c = _short_conv(x0_ref[0], cw_ref[:, 512:1024], row_len, 1)
    x1c = _short_conv(x1_ref[0], cw_ref[:, 1024:1536], row_len, 1)
    zin = x1c * vc
    o_ref[0] = (x0c * (y + zin * skip_ref[...])).astype(o_ref.dtype)


def _dft_inv(inv_c, inv_s, yr, yi, p, conv_w, skip, row_len):
    bsz, length, _ = p.shape
    nfp = inv_c.shape[1]
    t = min(length, 256)
    pcol = lambda c: pl.BlockSpec((1, t, 512), lambda b, i: (b, i, c))
    return pl.pallas_call(
        functools.partial(_dft_inv_kernel, row_len=row_len),
        out_shape=jax.ShapeDtypeStruct((bsz, length, HY_W), BF16),
        grid=(bsz, length // t),
        in_specs=[pl.BlockSpec((t, nfp), lambda b, i: (i, 0)),
                  pl.BlockSpec((t, nfp), lambda b, i: (i, 0)),
                  pl.BlockSpec((1, nfp, 512), lambda b, i: (b, 0, 0)),
                  pl.BlockSpec((1, nfp, 512), lambda b, i: (b, 0, 0)),
                  pcol(0), pcol(1), pcol(2),
                  pl.BlockSpec((3, 1536), lambda b, i: (0, 0)),
                  pl.BlockSpec((1, 512), lambda b, i: (0, 0))],
        out_specs=pl.BlockSpec((1, t, 512), lambda b, i: (b, i, 0)),
        compiler_params=_cparams(2),
        name="hyena_idft_gate",
    )(inv_c, inv_s, yr, yi, p, p, p, conv_w, skip.reshape(1, HY_W))


def _hyena_mixer(p, prm, tables, row_len):
    conv_w, w1, b1, w2, b2, w3, freq, skip = prm
    length = p.shape[1]
    fwd_c, fwd_s, inv_c, inv_s = tables
    filt = _hyena_filter(length, w1, b1, w2, b2, w3, freq)
    spec = _dft_fwd(fwd_c, fwd_s, filt[None], None)
    spec = (spec[0][0], spec[1][0])
    zin = _hy_zin(p, conv_w, row_len)
    yr, yi = _dft_fwd(fwd_c, fwd_s, zin, spec)
    return _dft_inv(inv_c, inv_s, yr, yi, p, conv_w, skip, row_len)


def _permute_w_in(w_in):
    hy = w_in[:, 0:1536]
    rw = w_in[:, 1536:3328]
    gd = w_in[:, 3328:5392]
    rg = w_in[:, 5392:6416]
    pad = jnp.zeros((w_in.shape[0], N_PROJ - 6416), w_in.dtype)
    return jnp.concatenate([hy, rw[:, :1536], gd[:, :2048], rg, rw[:, 1536:], gd[:, 2048:], pad],
                           axis=-1).astype(BF16)


def kernel(x, c, ctx, c_ctx, ada_w, ada_b, norm_mix_g, norm_mlp_g, w_in, w_out,
           hy_conv, hy_w1, hy_b1, hy_w2, hy_b2, hy_w3, hy_freq, hy_skip,
           rw_mu, rw_w0, rw_w_up, rw_a0, rw_a_up, rw_g_up, rw_k_k, rw_k_a, rw_r_k, rw_ln_g, rw_ln_b,
           gd_conv, gd_a_log, gd_dt_bias, gd_norm_g,
           rg_conv, rg_conv_b, rg_wa, rg_ba, rg_wx, rg_bx, rg_lambda,
           mlp_w1, mlp_w2, final_norm_g):
    bsz, seq, _ = x.shape
    ctx_len = ctx.shape[1]
    depth = ada_w.shape[0]
    tables_x = _dft_tables(seq)
    tables_c = _dft_tables(ctx_len)
    cond8 = jnp.concatenate([c, c_ctx[None, :], jnp.zeros((8 - bsz - 1, D_MODEL), F32)], axis=0)
    for l in range(depth):
        last = l == depth - 1
        mod = _modulation(cond8, ada_w[l], ada_b[l])
        mod_x = mod[:bsz].reshape(bsz, 6, D_MODEL)
        mod_c = mod[bsz:bsz + 1].reshape(1, 6, D_MODEL)
        w_in_bf = _permute_w_in(w_in[l])
        w_out_bf = w_out[l].astype(BF16)
        w1_bf = mlp_w1[l].astype(BF16)
        w2_bf = mlp_w2[l].astype(BF16)
        px = _inproj(x, norm_mix_g[l], mod_x, w_in_bf)
        pc = _inproj(ctx, norm_mix_g[l], mod_c, w_in_bf)
        hy_prm = (hy_conv[l], hy_w1[l], hy_b1[l], hy_w2[l], hy_b2[l], hy_w3[l], hy_freq[l], hy_skip[l])
        rw_prm = (rw_mu[l], rw_w0[l], rw_w_up[l], rw_a0[l], rw_a_up[l], rw_g_up[l],
                  rw_k_k[l], rw_k_a[l], rw_r_k[l], rw_ln_g[l], rw_ln_b[l])
        gd_prm = (gd_conv[l], gd_a_log[l], gd_dt_bias[l], gd_norm_g[l])
        rg_prm = (rg_conv[l], rg_conv_b[l], rg_wa[l], rg_ba[l], rg_wx[l], rg_bx[l], rg_lambda[l])
        c_rw, s_rw = _rwkv_mixer(pc, rw_prm, None)
        c_gd, s_gd = _gdn_mixer(pc, gd_prm, None, ctx_len)
        c_rg, s_rg = _rglru_mixer(pc, rg_prm, None, ctx_len)
        x_hy = _hyena_mixer(px, hy_prm, tables_x, GRID_W)
        x_rw, _ = _rwkv_mixer(px, rw_prm, s_rw)
        x_gd, _ = _gdn_mixer(px, gd_prm, s_gd, GRID_W)
        x_rg, _ = _rglru_mixer(px, rg_prm, s_rg, GRID_W)
        x_new = _outproj(x, mod_x, (x_hy, x_rw, x_gd, x_rg), w_out_bf)
        x_new = _mlp(x_new, norm_mlp_g[l], mod_x, w1_bf, w2_bf, final_norm_g if last else None)
        if not last:
            c_hy = _hyena_mixer(pc, hy_prm, tables_c, ctx_len)
            ctx_new = _outproj(ctx, mod_c, (c_hy, c_rw, c_gd, c_rg), w_out_bf)
            ctx = _mlp(ctx_new, norm_mlp_g[l], mod_c, w1_bf, w2_bf, None)
        x = x_new
    return x
```

```python
import functools
import math

import jax
import jax.numpy as jnp
from jax import lax
from jax.experimental import pallas as pl
from jax.experimental.pallas import tpu as pltpu

F32 = jnp.float32
BF16 = jnp.bfloat16
HIGHEST = lax.Precision.HIGHEST

D_MODEL = 2048
GRID_W = 64
HY_W = RW_W = GD_W = RG_W = 512
D_FF = 4 * D_MODEL
NORM_EPS = 1e-6
HY_EMB = 33
HY_BANDS = 16
RW_HEAD = 64
RW_GN_EPS = 64e-5
GD_HEAD = 128
GD_HEADS = 4
CHUNK = 64
RG_C = 8.0
LANES = 128

COL_HY = 0
COL_RW = 1536
COL_GD = 3072
COL_RG = 5120
COL_RW_LORA = 6144
COL_GD_GB = 6400
N_PROJ = 6656

VMEM_LIMIT = 56 * 1024 * 1024


def _cparams(n_axes):
    return pltpu.CompilerParams(dimension_semantics=("arbitrary",) * n_axes,
                                vmem_limit_bytes=VMEM_LIMIT)


def _dot(a, b):
    return jnp.dot(a.astype(BF16), b.astype(BF16), preferred_element_type=F32)


def _dot_nt(a, b):
    return lax.dot_general(a.astype(BF16), b.astype(BF16), (((1,), (1,)), ((), ())),
                           preferred_element_type=F32)


def _dot_tn(a, b):
    return lax.dot_general(a.astype(BF16), b.astype(BF16), (((0,), (0,)), ((), ())),
                           preferred_element_type=F32)


def _dot_hp(a, b):
    return jnp.dot(a, b, precision=HIGHEST, preferred_element_type=F32)


def _dot_nt_hp(a, b):
    return lax.dot_general(a, b, (((1,), (1,)), ((), ())), precision=HIGHEST,
                           preferred_element_type=F32)


def _sigmoid(x):
    return 1.0 / (1.0 + jnp.exp(-x))


def _silu(x):
    return x * _sigmoid(x)


def _softplus(x):
    return jnp.maximum(x, 0.0) + jnp.log1p(jnp.exp(-jnp.abs(x)))


def _short_conv(u, w, row_len, pad_left):
    t_len = u.shape[0]
    pos = lax.broadcasted_iota(jnp.int32, u.shape, 0) & (row_len - 1)
    y = None
    for j in range(w.shape[0]):
        off = j - pad_left
        if off == 0:
            term = u * w[j:j + 1, :]
        else:
            sh = pltpu.roll(u, (-off) % t_len, 0)
            ok = (pos + off >= 0) & (pos + off < row_len)
            term = jnp.where(ok, sh, 0.0) * w[j:j + 1, :]
        y = term if y is None else y + term
    return y


def _mod_kernel(c_ref, w_ref, b_ref, o_ref):
    o_ref[...] = _dot(_silu(c_ref[...]), w_ref[...]) + b_ref[...]


def _modulation(cond8, ada_w, ada_b):
    n = ada_w.shape[1]
    tn = 512
    return pl.pallas_call(
        _mod_kernel,
        out_shape=jax.ShapeDtypeStruct((8, n), F32),
        grid=(n // tn,),
        in_specs=[pl.BlockSpec((8, D_MODEL), lambda j: (0, 0)),
                  pl.BlockSpec((D_MODEL, tn), lambda j: (0, j)),
                  pl.BlockSpec((1, tn), lambda j: (0, j))],
        out_specs=pl.BlockSpec((8, tn), lambda j: (0, j)),
        compiler_params=_cparams(1),
        name="adaln_mod",
    )(cond8, ada_w, ada_b.reshape(1, n))


def _norm_mod(x, g, shift, scale):
    y = x * lax.rsqrt(jnp.mean(x * x, axis=-1, keepdims=True) + NORM_EPS) * g
    return y * (1.0 + scale) + shift


def _inproj_kernel(x_ref, g_ref, mod_ref, w_ref, o_ref, h_scr):
    @pl.when(pl.program_id(2) == 0)
    def _():
        h = _norm_mod(x_ref[0], g_ref[...], mod_ref[0, 0:1, :], mod_ref[0, 1:2, :])
        h_scr[...] = h.astype(BF16)

    o_ref[0] = jnp.dot(h_scr[...], w_ref[...], preferred_element_type=F32)


def _inproj(x, g, mod, w_bf):
    bsz, length, _ = x.shape
    tm = min(length, 1024)
    tn = 512
    per_batch = mod.shape[0] == bsz
    return pl.pallas_call(
        _inproj_kernel,
        out_shape=jax.ShapeDtypeStruct((bsz, length, N_PROJ), F32),
        grid=(bsz, length // tm, N_PROJ // tn),
        in_specs=[pl.BlockSpec((1, tm, D_MODEL), lambda b, i, j: (b, i, 0)),
                  pl.BlockSpec((1, D_MODEL), lambda b, i, j: (0, 0)),
                  pl.BlockSpec((1, 6, D_MODEL), (lambda b, i, j: (b, 0, 0)) if per_batch
                               else (lambda b, i, j: (0, 0, 0))),
                  pl.BlockSpec((D_MODEL, tn), lambda b, i, j: (0, j))],
        out_specs=pl.BlockSpec((1, tm, tn), lambda b, i, j: (b, i, j)),
        scratch_shapes=[pltpu.VMEM((tm, D_MODEL), BF16)],
        compiler_params=_cparams(3),
        name="inproj",
    )(x, g.reshape(1, D_MODEL), mod, w_bf)


def _outproj_kernel(x_ref, mod_ref, m0_ref, m1_ref, m2_ref, m3_ref, w_ref, o_ref):
    acc = jnp.dot(m0_ref[0], w_ref[0:512, :], preferred_element_type=F32)
    acc += jnp.dot(m1_ref[0], w_ref[512:1024, :], preferred_element_type=F32)
    acc += jnp.dot(m2_ref[0], w_ref[1024:1536, :], preferred_element_type=F32)
    acc += jnp.dot(m3_ref[0], w_ref[1536:2048, :], preferred_element_type=F32)
    o_ref[0] = x_ref[0] + mod_ref[0, 2:3, :] * acc


def _outproj(x, mod, mixers, w_bf):
    bsz, length, _ = x.shape
    tm = min(length, 512)
    per_batch = mod.shape[0] == bsz
    mspec = pl.BlockSpec((1, tm, 512), lambda b, i: (b, i, 0))
    return pl.pallas_call(
        _outproj_kernel,
        out_shape=jax.ShapeDtypeStruct((bsz, length, D_MODEL), F32),
        grid=(bsz, length // tm),
        in_specs=[pl.BlockSpec((1, tm, D_MODEL), lambda b, i: (b, i, 0)),
                  pl.BlockSpec((1, 6, D_MODEL), (lambda b, i: (b, 0, 0)) if per_batch
                               else (lambda b, i: (0, 0, 0))),
                  mspec, mspec, mspec, mspec,
                  pl.BlockSpec((D_MODEL, D_MODEL), lambda b, i: (0, 0))],
        out_specs=pl.BlockSpec((1, tm, D_MODEL), lambda b, i: (b, i, 0)),
        compiler_params=_cparams(2),
        name="outproj",
    )(x, mod, *mixers, w_bf)


def _mlp_kernel(x_ref, g_ref, mod_ref, w1_ref, w2_ref, fg_ref, o_ref, h_scr, acc_scr, *, final_norm):
    f = pl.program_id(2)

    @pl.when(f == 0)
    def _():
        h = _norm_mod(x_ref[0], g_ref[...], mod_ref[0, 3:4, :], mod_ref[0, 4:5, :])
        h_scr[...] = h.astype(BF16)
        acc_scr[...] = jnp.zeros_like(acc_scr)

    a = jnp.dot(h_scr[...], w1_ref[...], preferred_element_type=F32)
    a = jnp.square(jnp.maximum(a, 0.0)).astype(BF16)
    acc_scr[...] += jnp.dot(a, w2_ref[...], preferred_element_type=F32)

    @pl.when(f == pl.num_programs(2) - 1)
    def _():
        y = x_ref[0] + mod_ref[0, 5:6, :] * acc_scr[...]
        if final_norm:
            y = y * lax.rsqrt(jnp.mean(y * y, axis=-1, keepdims=True) + NORM_EPS) * fg_ref[...]
        o_ref[0] = y


def _mlp(x, g, mod, w1_bf, w2_bf, final_g):
    bsz, length, _ = x.shape
    tm = min(length, 512)
    tf = 1024
    per_batch = mod.shape[0] == bsz
    final_norm = final_g is not None
    fg = (final_g if final_norm else jnp.ones((D_MODEL,), F32)).reshape(1, D_MODEL)
    return pl.pallas_call(
        functools.partial(_mlp_kernel, final_norm=final_norm),
        out_shape=jax.ShapeDtypeStruct((bsz, length, D_MODEL), F32),
        grid=(bsz, length // tm, D_FF // tf),
        in_specs=[pl.BlockSpec((1, tm, D_MODEL), lambda b, i, f: (b, i, 0)),
                  pl.BlockSpec((1, D_MODEL), lambda b, i, f: (0, 0)),
                  pl.BlockSpec((1, 6, D_MODEL), (lambda b, i, f: (b, 0, 0)) if per_batch
                               else (lambda b, i, f: (0, 0, 0))),
                  pl.BlockSpec((D_MODEL, tf), lambda b, i, f: (0, f)),
                  pl.BlockSpec((tf, D_MODEL), lambda b, i, f: (f, 0)),
                  pl.BlockSpec((1, D_MODEL), lambda b, i, f: (0, 0))],
        out_specs=pl.BlockSpec((1, tm, D_MODEL), lambda b, i, f: (b, i, 0)),
        scratch_shapes=[pltpu.VMEM((tm, D_MODEL), BF16), pltpu.VMEM((tm, D_MODEL), F32)],
        compiler_params=_cparams(3),
        name="mlp",
    )(x, g.reshape(1, D_MODEL), mod, w1_bf, w2_bf, fg)


def _lin_scan(a, b, carry, rev):
    t_len, width = a.shape
    n_groups = t_len // 8
    a = a.reshape(n_groups, 8, width)
    b = b.reshape(n_groups, 8, width)
    sub = lax.broadcasted_iota(jnp.int32, a.shape, 1)
    for s in (1, 2, 4):
        if rev:
            a_sh = pltpu.roll(a, 8 - s, 1)
            b_sh = pltpu.roll(b, 8 - s, 1)
            ok = sub < 8 - s
        else:
            a_sh = pltpu.roll(a, s, 1)
            b_sh = pltpu.roll(b, s, 1)
            ok = sub >= s
        b = a * jnp.where(ok, b_sh, 0.0) + b
        a = a * jnp.where(ok, a_sh, 1.0)
    a = a.reshape(t_len, width)
    b = b.reshape(t_len, width)
    hs = [None] * n_groups
    for g in (range(n_groups - 1, -1, -1) if rev else range(n_groups)):
        h_g = a[8 * g:8 * g + 8, :] * carry + b[8 * g:8 * g + 8, :]
        hs[g] = h_g
        carry = h_g[0:1, :] if rev else h_g[7:8, :]
    return jnp.concatenate(hs, axis=0), carry


def _rglru_kernel(*refs, rev, row_len, final):
    if final:
        (x_ref, gate_ref, cw_ref, cb_ref, wa_ref, ba_ref, wx_ref, bx_ref, lam_ref, h0_ref, prev_ref,
         o_ref, st_ref, carry) = refs
    else:
        (x_ref, cw_ref, cb_ref, wa_ref, ba_ref, wx_ref, bx_ref, lam_ref, h0_ref,
         o_ref, st_ref, carry) = refs
    i = pl.program_id(1)

    @pl.when(i == 0)
    def _():
        carry[...] = h0_ref[0]

    xc = _short_conv(x_ref[0], cw_ref[...], row_len, 2) + cb_ref[...]
    xb = xc.astype(BF16)
    blocks = [slice(n * LANES, (n + 1) * LANES) for n in range(RG_W // LANES)]
    gate_r = jnp.concatenate([jnp.dot(xb[:, bs], wa_ref[n], preferred_element_type=F32)
                              for n, bs in enumerate(blocks)], axis=-1)
    gate_i = jnp.concatenate([jnp.dot(xb[:, bs], wx_ref[n], preferred_element_type=F32)
                              for n, bs in enumerate(blocks)], axis=-1)
    gate_r = _sigmoid(gate_r + ba_ref[...])
    gate_i = _sigmoid(gate_i + bx_ref[...])
    log_a = -RG_C * gate_r * _softplus(-lam_ref[...])
    a = jnp.exp(log_a)
    b = jnp.sqrt(-jnp.tanh(log_a) * (a * a + 1.0)) * (gate_i * xc)
    h, last = _lin_scan(a, b, carry[...], rev)
    carry[...] = last

    @pl.when(i == pl.num_programs(1) - 1)
    def _():
        st_ref[0] = last

    if final:
        gate = gate_ref[0]
        gelu = 0.5 * gate * (1.0 + jnp.tanh(math.sqrt(2.0 / math.pi) * (gate + 0.044715 * gate * gate * gate)))
        o_ref[0] = (gelu * (prev_ref[0] + h)).astype(o_ref.dtype)
    else:
        o_ref[0] = h


def _rglru_pass(p, prm, d, h0, prev, row_len):
    bsz, length, _ = p.shape
    conv_w, conv_b, wa, ba, wx, bx, lam = prm
    final = prev is not None
    rev = d == 1
    t = min(length, 256)
    nt = length // t
    tidx = (lambda i: nt - 1 - i) if rev else (lambda i: i)
    cx = COL_RG // RG_W
    col = lambda c0: pl.BlockSpec((1, t, RG_W), lambda b, i: (b, tidx(i), c0))
    vec = pl.BlockSpec((1, RG_W), lambda b, i: (0, 0))
    mat = pl.BlockSpec((RG_W // LANES, LANES, LANES), lambda b, i: (0, 0, 0))
    in_specs = [col(cx)]
    args = [p]
    if final:
        in_specs.append(col(cx + 1))
        args.append(p)
    in_specs += [pl.BlockSpec((4, RG_W), lambda b, i: (0, 0)), vec, mat, vec, mat, vec, vec,
                 pl.BlockSpec((1, 1, RG_W), lambda b, i: (b, 0, 0))]
    args += [conv_w, conv_b.reshape(1, RG_W), wa[d].astype(BF16), ba[d].reshape(1, RG_W),
             wx[d].astype(BF16), bx[d].reshape(1, RG_W), lam[d].reshape(1, RG_W), h0]
    if final:
        in_specs.append(pl.BlockSpec((1, t, RG_W), lambda b, i: (b, tidx(i), 0)))
        args.append(prev)
    out, st = pl.pallas_call(
        functools.partial(_rglru_kernel, rev=rev, row_len=row_len, final=final),
        out_shape=[jax.ShapeDtypeStruct((bsz, length, RG_W), BF16 if final else F32),
                   jax.ShapeDtypeStruct((bsz, 1, RG_W), F32)],
        grid=(bsz, nt),
        in_specs=in_specs,
        out_specs=[pl.BlockSpec((1, t, RG_W), lambda b, i: (b, tidx(i), 0)),
                   pl.BlockSpec((1, 1, RG_W), lambda b, i: (b, 0, 0))],
        scratch_shapes=[pltpu.VMEM((1, RG_W), F32)],
        compiler_params=_cparams(2),
        name="rglru_bwd" if rev else "rglru_fwd",
    )(*args)
    return out, st


def _rglru_mixer(p, prm, h0s, row_len):
    bsz = p.shape[0]
    if h0s is None:
        h0s = [jnp.zeros((bsz, 1, RG_W), F32)] * 2
    h_f, s_f = _rglru_pass(p, prm, 0, h0s[0], None, row_len)
    out, s_b = _rglru_pass(p, prm, 1, h0s[1], h_f, row_len)
    return out, [s_f, s_b]


def _tri_masks(n, rev):
    row = lax.broadcasted_iota(jnp.int32, (n, n), 0)
    col = lax.broadcasted_iota(jnp.int32, (n, n), 1)
    if rev:
        return row <= col, row < col
    return row >= col, row > col


def _dot_x3(a, b):
    a_hi = a.astype(BF16)
    a_lo = (a - a_hi.astype(F32)).astype(BF16)
    b_hi = b.astype(BF16)
    b_lo = (b - b_hi.astype(F32)).astype(BF16)
    mm = lambda p, q: jnp.dot(p, q, preferred_element_type=F32)
    return mm(a_hi, b_hi) + (mm(a_hi, b_lo) + mm(a_lo, b_hi))


def _unit_solve(neg_n, rhs, steps, dot):
    x = rhs
    m = neg_n
    for s in range(steps):
        x = x + dot(m, x)
        if s + 1 < steps:
            m = dot(m, m)
    return x


def _chunk_cumsum(x, rev):
    t_len = x.shape[0]
    pos = lax.broadcasted_iota(jnp.int32, x.shape, 0) & (CHUNK - 1)
    s = 1
    while s < CHUNK:
        if rev:
            x = x + jnp.where(pos < CHUNK - s, pltpu.roll(x, t_len - s, 0), 0.0)
        else:
            x = x + jnp.where(pos >= s, pltpu.roll(x, s, 0), 0.0)
        s *= 2
    return x


def _lane_form(x):
    sel = (lax.broadcasted_iota(jnp.int32, x.shape, 1) == 0).astype(BF16)
    p1 = x.astype(BF16)
    r1 = x - p1.astype(F32)
    p2 = r1.astype(BF16)
    p3 = (r1 - p2.astype(F32)).astype(BF16)
    nt = lambda p: lax.dot_general(sel, p, (((1,), (1,)), ((), ())), preferred_element_type=F32)
    return nt(p1) + nt(p2) + nt(p3)


def _dot_exact_rhs(a, b_exact):
    a_hi = a.astype(BF16)
    a_lo = (a - a_hi.astype(F32)).astype(BF16)
    b = b_exact.astype(BF16)
    return (jnp.dot(a_hi, b, preferred_element_type=F32) + jnp.dot(a_lo, b, preferred_element_type=F32))


def _gdn_kernel(*refs, rev, row_len, final, d):
    if final:
        (q_ref, k_ref, v_ref, z_ref, gb_ref, cq_ref, ck_ref, cv_ref, alog_ref, dtb_ref, ng_ref, s0_ref,
         prev_ref, o_ref, st_ref, state) = refs
    else:
        (q_ref, k_ref, v_ref, gb_ref, cq_ref, ck_ref, cv_ref, alog_ref, dtb_ref, ng_ref, s0_ref,
         o_ref, st_ref, state) = refs
    grp = pl.program_id(1)
    i = pl.program_id(2)
    n_heads = q_ref.shape[2] // GD_HEAD

    @pl.when(i == 0)
    def _():
        state[...] = s0_ref[0]

    qc = _silu(_short_conv(q_ref[0], cq_ref[...], row_len, 2))
    kc = _silu(_short_conv(k_ref[0], ck_ref[...], row_len, 2))
    vc = _silu(_short_conv(v_ref[0], cv_ref[...], row_len, 2))
    gbb = gb_ref[0]
    lane = lax.broadcasted_iota(jnp.int32, gbb.shape, 1)
    t_len = qc.shape[0]
    n_chunks = t_len // CHUNK
    incl, strict = _tri_masks(CHUNK, rev)
    order = list(range(n_chunks - 1, -1, -1) if rev else range(n_chunks))
    bf = lambda x: x.astype(BF16)

    heads = []
    for hh in range(n_heads):
        hs = slice(hh * GD_HEAD, (hh + 1) * GD_HEAD)
        head = grp * n_heads + hh
        q_h, k_h = qc[:, hs], kc[:, hs]
        q_h = q_h * lax.rsqrt(jnp.sum(q_h * q_h, axis=-1, keepdims=True) + 1e-6) * (GD_HEAD ** -0.5)
        k_h = k_h * lax.rsqrt(jnp.sum(k_h * k_h, axis=-1, keepdims=True) + 1e-6)
        g_raw = jnp.sum(jnp.where(lane == d * GD_HEADS + head, gbb, 0.0), axis=-1, keepdims=True)
        b_raw = jnp.sum(jnp.where(lane == (2 + d) * GD_HEADS + head, gbb, 0.0), axis=-1, keepdims=True)
        g = -jnp.exp(alog_ref[hh]) * _softplus(g_raw + dtb_ref[hh])
        heads.append(dict(q=q_h, k=k_h, v=vc[:, hs], beta=_sigmoid(b_raw), gc=_chunk_cumsum(g, rev)))

    pre = {}
    s_cur = [state[hh] for hh in range(n_heads)]
    os = {}

    def decay_tiles(c):
        for hh in range(n_heads):
            hd = heads[hh]
            sl = slice(c * CHUNK, (c + 1) * CHUNK)
            q_c, k_c, v_c, b_c, gc = hd["q"][sl], hd["k"][sl], hd["v"][sl], hd["beta"][sl], hd["gc"][sl]
            g_row = _lane_form(gc)
            kb = k_c * b_c
            e_gc = jnp.exp(gc)
            g_last = gc[0:1, :] if rev else gc[CHUNK - 1:CHUNK, :]
            pre[hh, c] = dict(decay_in=jnp.exp(jnp.where(incl, gc[:, :CHUNK] - g_row, -jnp.inf)),
                              q=q_c, k=k_c, kb=kb, x=jnp.concatenate([v_c * b_c, kb * e_gc], axis=-1),
                              qg=bf(q_c * e_gc), k_dec=bf(k_c * jnp.exp(g_last - gc)), dec=jnp.exp(g_last))

    def score_tiles(c):
        for hh in range(n_heads):
            p = pre[hh, c]
            p["m"] = -(_dot_nt(p["kb"], p["k"]) * jnp.where(strict, p["decay_in"], 0.0))
            p["a_qk"] = bf(_dot_nt(p["q"], p["k"]) * p["decay_in"])

    def solve_step(c, s):
        for hh in range(n_heads):
            p = pre[hh, c]
            p["x"] = p["x"] + _dot_x3(p["m"], p["x"])
            if s < 5:
                p["m"] = _dot_x3(p["m"], p["m"])

    def recur(c):
        for hh in range(n_heads):
            p = pre.pop((hh, c))
            x = p["x"]
            t2 = _dot(jnp.concatenate([bf(x[:, GD_HEAD:]), p["qg"]], axis=0), s_cur[hh])
            v_new = bf(x[:, :GD_HEAD] - t2[:CHUNK])
            os[hh, c] = t2[CHUNK:] + _dot(p["a_qk"], v_new)
            s_cur[hh] = s_cur[hh] * p["dec"] + _dot_tn(p["k_dec"], v_new)

    phases = [decay_tiles, score_tiles] + [functools.partial(solve_step, s=s) for s in range(6)] + [recur]
    for slot in range(n_chunks + len(phases) - 1):
        for ph in range(len(phases) - 1, -1, -1):
            if 0 <= slot - ph < n_chunks:
                phases[ph](order[slot - ph])

    for hh in range(n_heads):
        hs = slice(hh * GD_HEAD, (hh + 1) * GD_HEAD)
        o_h = jnp.concatenate([os[hh, c] for c in range(n_chunks)], axis=0)
        if final:
            o_t = prev_ref[0, :, hs] + o_h
            o_t = o_t * lax.rsqrt(jnp.mean(o_t * o_t, axis=-1, keepdims=True) + NORM_EPS) * ng_ref[...]
            o_ref[0, :, hs] = (o_t * _silu(z_ref[0, :, hs])).astype(o_ref.dtype)
        else:
            o_ref[0, :, hs] = o_h
        state[hh] = s_cur[hh]

    @pl.when(i == pl.num_programs(2) - 1)
    def _():
        for hh in range(n_heads):
            st_ref[0, hh] = s_cur[hh]


GD_HEADS_PER_STEP = 2
GD_TILE = 512


def _gdn_pass(p, prm, d, s0, prev, row_len):
    bsz, length, _ = p.shape
    conv_w, a_log, dt_bias, norm_g = prm
    final = prev is not None
    rev = d == 1
    t = min(length, GD_TILE)
    nt = length // t
    tidx = (lambda i: nt - 1 - i) if rev else (lambda i: i)
    nh = GD_HEADS_PER_STEP
    wd = nh * GD_HEAD
    ng = GD_W // wd
    c0 = COL_GD // wd
    col = lambda off: pl.BlockSpec((1, t, wd), lambda b, h, i: (b, tidx(i), c0 + off * ng + h))
    cw = lambda off: pl.BlockSpec((4, wd), lambda b, h, i: (0, off * ng + h))
    hvec = pl.BlockSpec((nh, 1, LANES), lambda b, h, i: (h, 0, 0))
    in_specs = [col(0), col(1), col(2)]
    args = [p, p, p]
    if final:
        in_specs.append(col(3))
        args.append(p)
    in_specs += [pl.BlockSpec((1, t, LANES), lambda b, h, i: (b, tidx(i), COL_GD_GB // LANES)),
                 cw(0), cw(1), cw(2), hvec, hvec,
                 pl.BlockSpec((1, LANES), lambda b, h, i: (0, 0)),
                 pl.BlockSpec((1, nh, GD_HEAD, GD_HEAD), lambda b, h, i: (b, h, 0, 0))]
    bcast = lambda v: jnp.broadcast_to(v.reshape(GD_HEADS, 1, 1), (GD_HEADS, 1, LANES))
    args += [p, conv_w, conv_w, conv_w, bcast(a_log[d]), bcast(dt_bias[d]), norm_g.reshape(1, GD_HEAD), s0]
    if final:
        in_specs.append(pl.BlockSpec((1, t, wd), lambda b, h, i: (b, tidx(i), h)))
        args.append(prev)
    out, st = pl.pallas_call(
        functools.partial(_gdn_kernel, rev=rev, row_len=row_len, final=final, d=d),
        out_shape=[jax.ShapeDtypeStruct((bsz, length, GD_W), BF16 if final else F32),
                   jax.ShapeDtypeStruct((bsz, GD_HEADS, GD_HEAD, GD_HEAD), F32)],
        grid=(bsz, ng, nt),
        in_specs=in_specs,
        out_specs=[pl.BlockSpec((1, t, wd), lambda b, h, i: (b, tidx(i), h)),
                   pl.BlockSpec((1, nh, GD_HEAD, GD_HEAD), lambda b, h, i: (b, h, 0, 0))],
        scratch_shapes=[pltpu.VMEM((nh, GD_HEAD, GD_HEAD), F32)],
        compiler_params=_cparams(3),
        name="gdn_bwd" if rev else "gdn_fwd",
    )(*args)
    return out, st


def _gdn_mixer(p, prm, s0s, row_len):
    bsz = p.shape[0]
    if s0s is None:
        s0s = [jnp.zeros((bsz, GD_HEADS, GD_HEAD, GD_HEAD), F32)] * 2
    o_f, s_f = _gdn_pass(p, prm, 0, s0s[0], None, row_len)
    out, s_b = _gdn_pass(p, prm, 1, s0s[1], o_f, row_len)
    return out, [s_f, s_b]


def _stack_heads(x, lo):
    return jnp.concatenate([jnp.where(lo, x, 0.0), jnp.where(lo, 0.0, x)], axis=0)


def _rwkv_kernel(*refs, rev, final):
    if final:
        (r_ref, k_ref, v_ref, lo_ref, mur_ref, muk_ref, muv_ref, mul_ref, w0_ref, wup_ref, a0_ref, aup_ref,
         gup_ref, kk_ref, ka_ref, rk_ref, lng_ref, lnb_ref, s0_ref, prev_ref,
         o_ref, st_ref, state, c_r, c_k, c_v, c_l) = refs
    else:
        (r_ref, k_ref, v_ref, lo_ref, mur_ref, muk_ref, muv_ref, mul_ref, w0_ref, wup_ref, a0_ref, aup_ref,
         gup_ref, kk_ref, ka_ref, rk_ref, lng_ref, lnb_ref, s0_ref,
         o_ref, st_ref, state, c_r, c_k, c_v, c_l) = refs
    i = pl.program_id(2)

    @pl.when(i == 0)
    def _():
        state[...] = s0_ref[0]
        c_r[...] = jnp.zeros_like(c_r)
        c_k[...] = jnp.zeros_like(c_k)
        c_v[...] = jnp.zeros_like(c_v)
        c_l[...] = jnp.zeros_like(c_l)

    t_len = r_ref.shape[1]

    def shifted(x_ref, carry, mu_ref):
        x = x_ref[0]
        row = lax.broadcasted_iota(jnp.int32, x.shape, 0)
        if rev:
            prev = jnp.where(row == t_len - 1, carry[...], pltpu.roll(x, t_len - 1, 0))
            carry[...] = x[0:1, :]
        else:
            prev = jnp.where(row == 0, carry[...], pltpu.roll(x, 1, 0))
            carry[...] = x[t_len - 1:t_len, :]
        return x + (prev - x) * mu_ref[...]

    r = shifted(r_ref, c_r, mur_ref)
    k = shifted(k_ref, c_k, muk_ref)
    v = shifted(v_ref, c_v, muv_ref)
    lora = shifted(lo_ref, c_l, mul_ref)
    xw, xa, xg = lora[:, 0:64], lora[:, 64:128], lora[:, 128:256]

    width = r_ref.shape[2]
    n_pairs = width // LANES
    lane = lax.broadcasted_iota(jnp.int32, (1, LANES), 1)
    lo = lane < RW_HEAD
    head_sum = (lax.broadcasted_iota(jnp.int32, (width, width), 0) // RW_HEAD
                == lax.broadcasted_iota(jnp.int32, (width, width), 1) // RW_HEAD).astype(F32)
    rowh = lax.broadcasted_iota(jnp.int32, (LANES, LANES), 0) // RW_HEAD
    colh = lax.broadcasted_iota(jnp.int32, (LANES, LANES), 1) // RW_HEAD

    lw = -math.exp(-0.5) * _sigmoid(w0_ref[...] + _dot(jnp.tanh(xw), wup_ref[...]))
    a = _sigmoid(a0_ref[...] + _dot(xa, aup_ref[...]))
    kk = k * kk_ref[...]
    kappa = kk / jnp.maximum(jnp.sqrt(_dot_exact_rhs(kk * kk, head_sum)), 1e-12)
    kt = k * (1.0 + (a - 1.0) * ka_ref[...])
    gate = _dot(_sigmoid(xg), gup_ref[...])
    bonus = _dot_exact_rhs(r * kt * rk_ref[...], head_sum) * v

    n_chunks = t_len // CHUNK
    lg_all = _chunk_cumsum(lw, rev)
    rowt = lax.broadcasted_iota(jnp.int32, (LANES, LANES), 0) & (CHUNK - 1)
    colt = lax.broadcasted_iota(jnp.int32, (LANES, LANES), 1) & (CHUNK - 1)
    same = rowh == colh
    incl2 = ((rowt <= colt) if rev else (rowt >= colt)) & same
    strict2 = ((rowt < colt) if rev else (rowt > colt)) & same
    order = list(range(n_chunks - 1, -1, -1) if rev else range(n_chunks))
    bf = lambda x: x.astype(BF16)

    pre = {}
    s_cur = [state[pp] for pp in range(n_pairs)]
    ys = {}

    def scores(c):
        for pp in range(n_pairs):
            sl = (slice(c * CHUNK, (c + 1) * CHUNK), slice(pp * LANES, (pp + 1) * LANES))
            lw_c, kap_c, a_c = lw[sl], kappa[sl], a[sl]
            lg = lg_all[sl]
            lg_tot = lg[0:1, :] if rev else lg[CHUNK - 1:CHUNK, :]
            e_neg = jnp.exp(-lg)
            e_rem = jnp.exp(lg_tot - lg)
            p_raw = -(kap_c * a_c)
            q2 = bf(_stack_heads(kap_c * jnp.exp(lg - lw_c), lo))
            p2 = bf(_stack_heads(p_raw * e_neg, lo))
            k2 = bf(_stack_heads(kt[sl] * e_neg, lo))
            r2 = bf(_stack_heads(r[sl] * jnp.exp(lg), lo))
            pre[pp, c] = dict(
                q2=q2, r2=r2, v2=bf(_stack_heads(v[sl], lo)),
                pt2=bf(_stack_heads(p_raw * e_rem, lo)), kt2=bf(_stack_heads(kt[sl] * e_rem, lo)),
                m=jnp.where(strict2, _dot_nt(q2, p2), 0.0),
                a_qk=jnp.where(strict2, _dot_nt(q2, k2), 0.0),
                a_rp=bf(jnp.where(incl2, _dot_nt(r2, p2), 0.0)),
                a_rk=jnp.where(incl2, _dot_nt(r2, k2), 0.0),
                dec=jnp.broadcast_to(jnp.exp(lg_tot), (LANES, LANES)).T)

    def local_terms(c):
        for pp in range(n_pairs):
            p = pre[pp, c]
            p["x"] = jnp.concatenate([p["q2"].astype(F32), _dot(p["a_qk"], p["v2"])], axis=-1)
            p["y_loc"] = _dot(p["a_rk"], p["v2"])
            p["s_loc"] = _dot_tn(p["kt2"], p["v2"])

    def solve_step(c, s):
        for pp in range(n_pairs):
            p = pre[pp, c]
            p["x"] = p["x"] + _dot(p["m"], p["x"])
            if s < 5:
                p["m"] = _dot(p["m"], p["m"])

    def recur(c):
        for pp in range(n_pairs):
            p = pre.pop((pp, c))
            x = p["x"]
            t2 = _dot(jnp.concatenate([bf(x[:, :LANES]), p["r2"]], axis=0), s_cur[pp])
            u2 = bf(t2[:LANES] + x[:, LANES:])
            y2 = t2[LANES:] + _dot(p["a_rp"], u2) + p["y_loc"]
            s_cur[pp] = s_cur[pp] * p["dec"] + _dot_tn(p["pt2"], u2) + p["s_loc"]
            ys[pp, c] = y2[:CHUNK] + y2[CHUNK:]

    phases = [scores, local_terms] + [functools.partial(solve_step, s=s) for s in range(6)] + [recur]
    for slot in range(n_chunks + len(phases) - 1):
        for ph in range(len(phases) - 1, -1, -1):
            if 0 <= slot - ph < n_chunks:
                phases[ph](order[slot - ph])

    y = jnp.concatenate([jnp.concatenate([ys[pp, c] for c in range(n_chunks)], axis=0)
                         for pp in range(n_pairs)], axis=1)
    mean = _dot_exact_rhs(y, head_sum) * (1.0 / RW_HEAD)
    yc = y - mean
    var = _dot_exact_rhs(yc * yc, head_sum) * (1.0 / RW_HEAD)
    yn = yc * lax.rsqrt(var + RW_GN_EPS) * lng_ref[...] + lnb_ref[...]
    out = (yn + bonus) * gate
    if final:
        o_ref[0] = (prev_ref[0] + out).astype(o_ref.dtype)
    else:
        o_ref[0] = out
    for pp in range(n_pairs):
        state[pp] = s_cur[pp]

    @pl.when(i == pl.num_programs(2) - 1)
    def _():
        for pp in range(n_pairs):
            st_ref[0, pp] = s_cur[pp]


RW_PAIRS_PER_STEP = 2
RW_TILE = 512


def _rwkv_pass(p, prm, d, s0, prev):
    bsz, length, _ = p.shape
    mu, w0, w_up, a0, a_up, g_up, k_k, k_a, r_k, ln_g, ln_b = [t[d] for t in prm]
    final = prev is not None
    rev = d == 1
    t = min(length, RW_TILE)
    nt = length // t
    tidx = (lambda i: nt - 1 - i) if rev else (lambda i: i)
    npp = RW_PAIRS_PER_STEP
    wd = npp * LANES
    c0 = COL_RW // wd
    ng = RW_W // wd
    col = lambda off: pl.BlockSpec((1, t, wd), lambda b, n, i: (b, tidx(i), c0 + off * ng + n))
    vec = pl.BlockSpec((1, wd), lambda b, n, i: (0, n))
    whole = lambda shape: pl.BlockSpec(shape, lambda b, n, i: (0,) * len(shape))
    row = lambda x: x.reshape(1, -1)
    in_specs = [col(0), col(1), col(2),
                pl.BlockSpec((1, t, 256), lambda b, n, i: (b, tidx(i), COL_RW_LORA // 256)),
                vec, vec, vec, whole((1, 256)),
                vec, pl.BlockSpec((64, wd), lambda b, n, i: (0, n)),
                vec, pl.BlockSpec((64, wd), lambda b, n, i: (0, n)),
                pl.BlockSpec((LANES, wd), lambda b, n, i: (0, n)),
                vec, vec, vec, vec, vec,
                pl.BlockSpec((1, npp, LANES, LANES), lambda b, n, i: (b, n, 0, 0))]
    args = [p, p, p, p,
            row(mu[0:512]), row(mu[512:1024]), row(mu[1024:1536]), row(mu[1536:1792]),
            row(w0), w_up.astype(BF16), row(a0), a_up.astype(BF16), g_up.astype(BF16),
            row(k_k), row(k_a), row(r_k), row(ln_g), row(ln_b), s0]
    if final:
        in_specs.append(pl.BlockSpec((1, t, wd), lambda b, n, i: (b, tidx(i), n)))
        args.append(prev)
    out, st = pl.pallas_call(
        functools.partial(_rwkv_kernel, rev=rev, final=final),
        out_shape=[jax.ShapeDtypeStruct((bsz, length, RW_W), BF16 if final else F32),
                   jax.ShapeDtypeStruct((bsz, 4, LANES, LANES), F32)],
        grid=(bsz, ng, nt),
        in_specs=in_specs,
        out_specs=[pl.BlockSpec((1, t, wd), lambda b, n, i: (b, tidx(i), n)),
                   pl.BlockSpec((1, npp, LANES, LANES), lambda b, n, i: (b, n, 0, 0))],
        scratch_shapes=[pltpu.VMEM((npp, LANES, LANES), F32), pltpu.VMEM((1, wd), F32),
                        pltpu.VMEM((1, wd), F32), pltpu.VMEM((1, wd), F32),
                        pltpu.VMEM((1, 256), F32)],
        compiler_params=_cparams(3),
        name="rwkv_bwd" if rev else "rwkv_fwd",
    )(*args)
    return out, st


def _rwkv_mixer(p, prm, s0s):
    bsz = p.shape[0]
    if s0s is None:
        s0s = [jnp.zeros((bsz, 4, LANES, LANES), F32)] * 2
    o_f, s_f = _rwkv_pass(p, prm, 0, s0s[0], None)
    out, s_b = _rwkv_pass(p, prm, 1, s0s[1], o_f)
    return out, [s_f, s_b]


def _dft_tables(length):
    n = 2 * length
    nfp = -(-(length + 1) // LANES) * LANES
    kf = jnp.arange(nfp, dtype=jnp.int32)[:, None]
    s1 = jnp.arange(n // 64, dtype=jnp.int32)[None, :]
    s0 = jnp.arange(64, dtype=jnp.int32)[None, :]
    ang_a = (2.0 * math.pi / n) * ((kf * s1 * 64) % n).astype(F32)
    ang_b = (2.0 * math.pi / n) * ((kf * s0) % n).astype(F32)
    ok = (kf <= length).astype(F32)
    ca, sa = (jnp.cos(ang_a) * ok)[:, :, None], (jnp.sin(ang_a) * ok)[:, :, None]
    cb, sb = jnp.cos(ang_b)[:, None, :], jnp.sin(ang_b)[:, None, :]
    tab_c = (ca * cb - sa * sb).reshape(nfp, n).astype(BF16)
    tab_s = (-(sa * cb + ca * sb)).reshape(nfp, n).astype(BF16)
    kk = kf[:, 0]
    wk = jnp.where((kk == 0) | (kk == length), 1.0, 2.0) * (kk <= length) / n
    return tab_c, tab_s, jnp.broadcast_to(wk[:, None], (nfp, LANES)).astype(F32)


def _hyena_filter(length, w1, b1, w2, b2, w3, freq):
    t = jnp.arange(length, dtype=F32)
    z = t / max(length - 1, 1)
    bands = jnp.linspace(1e-4, HY_BANDS - 1, HY_BANDS, dtype=F32)
    ang = (2.0 * math.pi / length) * t[:, None] * bands[None, :]
    feat = jnp.concatenate([z[:, None], jnp.cos(ang), -jnp.sin(ang)], axis=-1)
    h = jnp.sin(freq[0] * (feat @ w1 + b1))
    h = jnp.sin(freq[1] * (h @ w2 + b2))
    h = (h @ w3).astype(F32)
    deltas = jnp.abs(jnp.linspace(math.log(1e-2) / 1.5, math.log(1e-2) / 0.3, HY_W, dtype=F32))
    h = h * jnp.exp(-z[:, None] * jnp.tile(deltas, 2)[None, :])
    h_fwd, h_bwd = h[:, :HY_W], h[:, HY_W:]
    filt = jnp.concatenate([h_fwd, jnp.zeros((1, HY_W), F32), h_bwd[:0:-1]], axis=0)
    return filt / jnp.sum(jnp.abs(filt), axis=0, keepdims=True)


def _hy_zin_kernel(v_ref, x1_ref, cw_ref, o_ref, *, row_len):
    vc = _short_conv(v_ref[0], cw_ref[:, 0:512], row_len, 1)
    x1c = _short_conv(x1_ref[0], cw_ref[:, 1024:1536], row_len, 1)
    o_ref[0] = (x1c * vc).astype(o_ref.dtype)


def _hy_zin(p, conv_w, row_len):
    bsz, length, _ = p.shape
    t = min(length, 256)
    return pl.pallas_call(
        functools.partial(_hy_zin_kernel, row_len=row_len),
        out_shape=jax.ShapeDtypeStruct((bsz, length, HY_W), BF16),
        grid=(bsz, length // t),
        in_specs=[pl.BlockSpec((1, t, 512), lambda b, i: (b, i, 0)),
                  pl.BlockSpec((1, t, 512), lambda b, i: (b, i, 2)),
                  pl.BlockSpec((3, 1536), lambda b, i: (0, 0))],
        out_specs=pl.BlockSpec((1, t, 512), lambda b, i: (b, i, 0)),
        compiler_params=_cparams(2),
        name="hyena_zin",
    )(p, p, conv_w)


def _dft_fwd_kernel(*refs, mult):
    if mult:
        c_ref, s_ref, z_ref, fr_ref, fi_ref, yr_ref, yi_ref = refs
    else:
        c_ref, s_ref, z_ref, wk_ref, yr_ref, yi_ref = refs
    z = z_ref[0].astype(BF16)
    zr = jnp.dot(c_ref[...], z, preferred_element_type=F32)
    zi = jnp.dot(s_ref[...], z, preferred_element_type=F32)
    if mult:
        fr, fi = fr_ref[...], fi_ref[...]
        zr, zi = zr * fr - zi * fi, zr * fi + zi * fr
    else:
        zr, zi = zr * wk_ref[:, 0:1], zi * wk_ref[:, 0:1]
    yr_ref[0] = zr.astype(yr_ref.dtype)
    yi_ref[0] = zi.astype(yi_ref.dtype)


def _dft_fwd(tab_c, tab_s, z, spec, wk=None):
    bsz, klen, _ = z.shape
    nfp = tab_c.shape[0]
    mult = spec is not None
    tm = 384 if (mult and nfp % 384 == 0) else LANES
    in_specs = [pl.BlockSpec((tm, klen), lambda b, i: (i, 0)),
                pl.BlockSpec((tm, klen), lambda b, i: (i, 0)),
                pl.BlockSpec((1, klen, 512), lambda b, i: (b, 0, 0))]
    args = [tab_c, tab_s, z]
    if mult:
        in_specs += [pl.BlockSpec((tm, 512), lambda b, i: (i, 0))] * 2
        args += list(spec)
    else:
        in_specs.append(pl.BlockSpec((tm, LANES), lambda b, i: (i, 0)))
        args.append(wk)
    odt = BF16 if mult else F32
    return pl.pallas_call(
        functools.partial(_dft_fwd_kernel, mult=mult),
        out_shape=[jax.ShapeDtypeStruct((bsz, nfp, 512), odt)] * 2,
        grid=(bsz, nfp // tm),
        in_specs=in_specs,
        out_specs=[pl.BlockSpec((1, tm, 512), lambda b, i: (b, i, 0))] * 2,
        compiler_params=_cparams(2),
        name="hyena_dft_mul" if mult else "hyena_dft_filter",
    )(*args)


def _dft_inv_kernel(ci_ref, si_ref, yr_ref, yi_ref, v_ref, x0_ref, x1_ref, cw_ref, skip_ref, o_ref, *, row_len):
    y = jnp.dot(ci_ref[...], yr_ref[0], preferred_element_type=F32)
    y += jnp.dot(si_ref[...], yi_ref[0], preferred_element_type=F32)
    vc = _short_conv(v_ref[0], cw_ref[:, 0:512], row_len, 1)
    x0c = _short_conv(x0_ref[0], cw_ref[:, 512:1024], row_len, 1)
    x1c = _short_conv(x1_ref[0], cw_ref[:, 1024:1536], row_len, 1)
    zin = x1c * vc
    o_ref[0] = (x0c * (y + zin * skip_ref[...])).astype(o_ref.dtype)


def _dft_inv(tab_c, tab_s, yr, yi, p, conv_w, skip, row_len):
    bsz, length, _ = p.shape
    nfp = tab_c.shape[0]
    t = min(length, 256)
    pcol = lambda c: pl.BlockSpec((1, t, 512), lambda b, i: (b, i, c))
    return pl.pallas_call(
        functools.partial(_dft_inv_kernel, row_len=row_len),
        out_shape=jax.ShapeDtypeStruct((bsz, length, HY_W), BF16),
        grid=(bsz, length // t),
        in_specs=[pl.BlockSpec((t, nfp), lambda b, i: (i, 0)),
                  pl.BlockSpec((t, nfp), lambda b, i: (i, 0)),
                  pl.BlockSpec((1, nfp, 512), lambda b, i: (b, 0, 0)),
                  pl.BlockSpec((1, nfp, 512), lambda b, i: (b, 0, 0)),
                  pcol(0), pcol(1), pcol(2),
                  pl.BlockSpec((3, 1536), lambda b, i: (0, 0)),
                  pl.BlockSpec((1, 512), lambda b, i: (0, 0))],
        out_specs=pl.BlockSpec((1, t, 512), lambda b, i: (b, i, 0)),
        compiler_params=_cparams(2),
        name="hyena_idft_gate",
    )(tab_c, tab_s, yr, yi, p, p, p, conv_w, skip.reshape(1, HY_W))


def _hyena_mixer(p, prm, tables, row_len):
    conv_w, w1, b1, w2, b2, w3, freq, skip = prm
    length = p.shape[1]
    tab_c, tab_s, wk = tables
    filt = _hyena_filter(length, w1, b1, w2, b2, w3, freq)
    spec = _dft_fwd(tab_c, tab_s, filt[None], None, wk)
    spec = (spec[0][0], spec[1][0])
    zin = _hy_zin(p, conv_w, row_len)
    yr, yi = _dft_fwd(tab_c, tab_s, zin, spec)
    return _dft_inv(tab_c, tab_s, yr, yi, p, conv_w, skip, row_len)


def _permute_w_in(w_in):
    hy = w_in[:, 0:1536]
    rw = w_in[:, 1536:3328]
    gd = w_in[:, 3328:5392]
    rg = w_in[:, 5392:6416]
    pad = jnp.zeros((w_in.shape[0], N_PROJ - 6416), w_in.dtype)
    return jnp.concatenate([hy, rw[:, :1536], gd[:, :2048], rg, rw[:, 1536:], gd[:, 2048:], pad],
                           axis=-1).astype(BF16)


def kernel(x, c, ctx, c_ctx, ada_w, ada_b, norm_mix_g, norm_mlp_g, w_in, w_out,
           hy_conv, hy_w1, hy_b1, hy_w2, hy_b2, hy_w3, hy_freq, hy_skip,
           rw_mu, rw_w0, rw_w_up, rw_a0, rw_a_up, rw_g_up, rw_k_k, rw_k_a, rw_r_k, rw_ln_g, rw_ln_b,
           gd_conv, gd_a_log, gd_dt_bias, gd_norm_g,
           rg_conv, rg_conv_b, rg_wa, rg_ba, rg_wx, rg_bx, rg_lambda,
           mlp_w1, mlp_w2, final_norm_g):
    bsz, seq, _ = x.shape
    ctx_len = ctx.shape[1]
    depth = ada_w.shape[0]
    tables_x = _dft_tables(seq)
    tables_c = _dft_tables(ctx_len)
    cond8 = jnp.concatenate([c, c_ctx[None, :], jnp.zeros((8 - bsz - 1, D_MODEL), F32)], axis=0)
    for l in range(depth):
        last = l == depth - 1
        mod = _modulation(cond8, ada_w[l], ada_b[l])
        mod_x = mod[:bsz].reshape(bsz, 6, D_MODEL)
        mod_c = mod[bsz:bsz + 1].reshape(1, 6, D_MODEL)
        w_in_bf = _permute_w_in(w_in[l])
        w_out_bf = w_out[l].astype(BF16)
        w1_bf = mlp_w1[l].astype(BF16)
        w2_bf = mlp_w2[l].astype(BF16)
        px = _inproj(x, norm_mix_g[l], mod_x, w_in_bf)
        pc = _inproj(ctx, norm_mix_g[l], mod_c, w_in_bf)
        hy_prm = (hy_conv[l], hy_w1[l], hy_b1[l], hy_w2[l], hy_b2[l], hy_w3[l], hy_freq[l], hy_skip[l])
        rw_prm = (rw_mu[l], rw_w0[l], rw_w_up[l], rw_a0[l], rw_a_up[l], rw_g_up[l],
                  rw_k_k[l], rw_k_a[l], rw_r_k[l], rw_ln_g[l], rw_ln_b[l])
        gd_prm = (gd_conv[l], gd_a_log[l], gd_dt_bias[l], gd_norm_g[l])
        rg_prm = (rg_conv[l], rg_conv_b[l], rg_wa[l], rg_ba[l], rg_wx[l], rg_bx[l], rg_lambda[l])
        c_rw, s_rw = _rwkv_mixer(pc, rw_prm, None)
        c_gd, s_gd = _gdn_mixer(pc, gd_prm, None, ctx_len)
        c_rg, s_rg = _rglru_mixer(pc, rg_prm, None, ctx_len)
        x_hy = _hyena_mixer(px, hy_prm, tables_x, GRID_W)
        x_rw, _ = _rwkv_mixer(px, rw_prm, s_rw)
        x_gd, _ = _gdn_mixer(px, gd_prm, s_gd, GRID_W)
        x_rg, _ = _rglru_mixer(px, rg_prm, s_rg, GRID_W)
        x_new = _outproj(x, mod_x, (x_hy, x_rw, x_gd, x_rg), w_out_bf)
        x_new = _mlp(x_new, norm_mlp_g[l], mod_x, w1_bf, w2_bf, final_norm_g if last else None)
        if not last:
            c_hy = _hyena_mixer(pc, hy_prm, tables_c, ctx_len)
            ctx_new = _outproj(ctx, mod_c, (c_hy, c_rw, c_gd, c_rg), w_out_bf)
            ctx = _mlp(ctx_new, norm_mlp_g[l], mod_c, w1_bf, w2_bf, None)
        x = x_new
    return x
```

```python
import functools
import math

import jax
import jax.numpy as jnp
from jax import lax
from jax.experimental import pallas as pl
from jax.experimental.pallas import tpu as pltpu

F32 = jnp.float32
BF16 = jnp.bfloat16
HIGHEST = lax.Precision.HIGHEST

D_MODEL = 2048
GRID_W = 64
HY_W = RW_W = GD_W = RG_W = 512
D_FF = 4 * D_MODEL
NORM_EPS = 1e-6
HY_EMB = 33
HY_BANDS = 16
RW_HEAD = 64
RW_GN_EPS = 64e-5
GD_HEAD = 128
GD_HEADS = 4
CHUNK = 64
RG_C = 8.0
LANES = 128

COL_HY = 0
COL_RW = 1536
COL_GD = 3072
COL_RG = 5120
COL_RW_LORA = 6144
COL_GD_GB = 6400
N_PROJ = 6656

VMEM_LIMIT = 56 * 1024 * 1024


def _cparams(n_axes):
    return pltpu.CompilerParams(dimension_semantics=("arbitrary",) * n_axes,
                                vmem_limit_bytes=VMEM_LIMIT)


def _dot(a, b):
    return jnp.dot(a.astype(BF16), b.astype(BF16), preferred_element_type=F32)


def _dot_nt(a, b):
    return lax.dot_general(a.astype(BF16), b.astype(BF16), (((1,), (1,)), ((), ())),
                           preferred_element_type=F32)


def _dot_tn(a, b):
    return lax.dot_general(a.astype(BF16), b.astype(BF16), (((0,), (0,)), ((), ())),
                           preferred_element_type=F32)


def _dot_hp(a, b):
    return jnp.dot(a, b, precision=HIGHEST, preferred_element_type=F32)


def _dot_nt_hp(a, b):
    return lax.dot_general(a, b, (((1,), (1,)), ((), ())), precision=HIGHEST,
                           preferred_element_type=F32)


def _sigmoid(x):
    return 1.0 / (1.0 + jnp.exp(-x))


def _silu(x):
    return x * _sigmoid(x)


def _softplus(x):
    return jnp.maximum(x, 0.0) + jnp.log1p(jnp.exp(-jnp.abs(x)))


def _short_conv(u, w, row_len, pad_left):
    t_len = u.shape[0]
    pos = lax.broadcasted_iota(jnp.int32, u.shape, 0) & (row_len - 1)
    y = None
    for j in range(w.shape[0]):
        off = j - pad_left
        if off == 0:
            term = u * w[j:j + 1, :]
        else:
            sh = pltpu.roll(u, (-off) % t_len, 0)
            ok = (pos + off >= 0) & (pos + off < row_len)
            term = jnp.where(ok, sh, 0.0) * w[j:j + 1, :]
        y = term if y is None else y + term
    return y


def _mod_kernel(c_ref, w_ref, b_ref, o_ref):
    o_ref[...] = _dot(_silu(c_ref[...]), w_ref[0]) + b_ref[...]


def _modulation(cond8, ada_w, ada_b, layer):
    n = ada_w.shape[2]
    tn = 512
    return pl.pallas_call(
        _mod_kernel,
        out_shape=jax.ShapeDtypeStruct((8, n), F32),
        grid=(n // tn,),
        in_specs=[pl.BlockSpec((8, D_MODEL), lambda j: (0, 0)),
                  pl.BlockSpec((1, D_MODEL, tn), lambda j: (layer, 0, j)),
                  pl.BlockSpec((1, tn), lambda j: (0, j))],
        out_specs=pl.BlockSpec((8, tn), lambda j: (0, j)),
        compiler_params=_cparams(1),
        name="adaln_mod",
    )(cond8, ada_w, ada_b.reshape(1, n))


def _norm_mod(x, g, shift, scale):
    y = x * lax.rsqrt(jnp.mean(x * x, axis=-1, keepdims=True) + NORM_EPS) * g
    return y * (1.0 + scale) + shift


def _inproj_kernel(x_ref, g_ref, mod_ref, w_ref, o_ref, h_scr):
    @pl.when(pl.program_id(2) == 0)
    def _():
        h = _norm_mod(x_ref[0], g_ref[...], mod_ref[0, 0:1, :], mod_ref[0, 1:2, :])
        h_scr[...] = h.astype(BF16)

    o_ref[0] = jnp.dot(h_scr[...], w_ref[...], preferred_element_type=F32)


def _inproj(x, g, mod, w_bf):
    bsz, length, _ = x.shape
    tm = min(length, 1024)
    tn = 512
    per_batch = mod.shape[0] == bsz
    return pl.pallas_call(
        _inproj_kernel,
        out_shape=jax.ShapeDtypeStruct((bsz, length, N_PROJ), F32),
        grid=(bsz, length // tm, N_PROJ // tn),
        in_specs=[pl.BlockSpec((1, tm, D_MODEL), lambda b, i, j: (b, i, 0)),
                  pl.BlockSpec((1, D_MODEL), lambda b, i, j: (0, 0)),
                  pl.BlockSpec((1, 6, D_MODEL), (lambda b, i, j: (b, 0, 0)) if per_batch
                               else (lambda b, i, j: (0, 0, 0))),
                  pl.BlockSpec((D_MODEL, tn), lambda b, i, j: (0, j))],
        out_specs=pl.BlockSpec((1, tm, tn), lambda b, i, j: (b, i, j)),
        scratch_shapes=[pltpu.VMEM((tm, D_MODEL), BF16)],
        compiler_params=_cparams(3),
        name="inproj",
    )(x, g.reshape(1, D_MODEL), mod, w_bf)


def _outproj_kernel(x_ref, mod_ref, m0_ref, m1_ref, m2_ref, m3_ref, w_ref, o_ref):
    acc = jnp.dot(m0_ref[0], w_ref[0:512, :], preferred_element_type=F32)
    acc += jnp.dot(m1_ref[0], w_ref[512:1024, :], preferred_element_type=F32)
    acc += jnp.dot(m2_ref[0], w_ref[1024:1536, :], preferred_element_type=F32)
    acc += jnp.dot(m3_ref[0], w_ref[1536:2048, :], preferred_element_type=F32)
    o_ref[0] = x_ref[0] + mod_ref[0, 2:3, :] * acc


def _outproj(x, mod, mixers, w_bf):
    bsz, length, _ = x.shape
    tm = min(length, 512)
    per_batch = mod.shape[0] == bsz
    mspec = pl.BlockSpec((1, tm, 512), lambda b, i: (b, i, 0))
    return pl.pallas_call(
        _outproj_kernel,
        out_shape=jax.ShapeDtypeStruct((bsz, length, D_MODEL), F32),
        grid=(bsz, length // tm),
        in_specs=[pl.BlockSpec((1, tm, D_MODEL), lambda b, i: (b, i, 0)),
                  pl.BlockSpec((1, 6, D_MODEL), (lambda b, i: (b, 0, 0)) if per_batch
                               else (lambda b, i: (0, 0, 0))),
                  mspec, mspec, mspec, mspec,
                  pl.BlockSpec((D_MODEL, D_MODEL), lambda b, i: (0, 0))],
        out_specs=pl.BlockSpec((1, tm, D_MODEL), lambda b, i: (b, i, 0)),
        compiler_params=_cparams(2),
        name="outproj",
    )(x, mod, *mixers, w_bf)


def _mlp_kernel(x_ref, g_ref, mod_ref, w1_ref, w2_ref, fg_ref, o_ref, h_scr, acc_scr, *, final_norm):
    f = pl.program_id(2)

    @pl.when(f == 0)
    def _():
        h = _norm_mod(x_ref[0], g_ref[...], mod_ref[0, 3:4, :], mod_ref[0, 4:5, :])
        h_scr[...] = h.astype(BF16)
        acc_scr[...] = jnp.zeros_like(acc_scr)

    a = jnp.dot(h_scr[...], w1_ref[...], preferred_element_type=F32)
    a = jnp.square(jnp.maximum(a, 0.0)).astype(BF16)
    acc_scr[...] += jnp.dot(a, w2_ref[...], preferred_element_type=F32)

    @pl.when(f == pl.num_programs(2) - 1)
    def _():
        y = x_ref[0] + mod_ref[0, 5:6, :] * acc_scr[...]
        if final_norm:
            y = y * lax.rsqrt(jnp.mean(y * y, axis=-1, keepdims=True) + NORM_EPS) * fg_ref[...]
        o_ref[0] = y


def _mlp(x, g, mod, w1_bf, w2_bf, final_g):
    bsz, length, _ = x.shape
    tm = min(length, 512)
    tf = 1024
    per_batch = mod.shape[0] == bsz
    final_norm = final_g is not None
    fg = (final_g if final_norm else jnp.ones((D_MODEL,), F32)).reshape(1, D_MODEL)
    return pl.pallas_call(
        functools.partial(_mlp_kernel, final_norm=final_norm),
        out_shape=jax.ShapeDtypeStruct((bsz, length, D_MODEL), F32),
        grid=(bsz, length // tm, D_FF // tf),
        in_specs=[pl.BlockSpec((1, tm, D_MODEL), lambda b, i, f: (b, i, 0)),
                  pl.BlockSpec((1, D_MODEL), lambda b, i, f: (0, 0)),
                  pl.BlockSpec((1, 6, D_MODEL), (lambda b, i, f: (b, 0, 0)) if per_batch
                               else (lambda b, i, f: (0, 0, 0))),
                  pl.BlockSpec((D_MODEL, tf), lambda b, i, f: (0, f)),
                  pl.BlockSpec((tf, D_MODEL), lambda b, i, f: (f, 0)),
                  pl.BlockSpec((1, D_MODEL), lambda b, i, f: (0, 0))],
        out_specs=pl.BlockSpec((1, tm, D_MODEL), lambda b, i, f: (b, i, 0)),
        scratch_shapes=[pltpu.VMEM((tm, D_MODEL), BF16), pltpu.VMEM((tm, D_MODEL), F32)],
        compiler_params=_cparams(3),
        name="mlp",
    )(x, g.reshape(1, D_MODEL), mod, w1_bf, w2_bf, fg)


def _lin_scan(a, b, carry, rev):
    t_len, width = a.shape
    n_groups = t_len // 8
    a = a.reshape(n_groups, 8, width)
    b = b.reshape(n_groups, 8, width)
    sub = lax.broadcasted_iota(jnp.int32, a.shape, 1)
    for s in (1, 2, 4):
        if rev:
            a_sh = pltpu.roll(a, 8 - s, 1)
            b_sh = pltpu.roll(b, 8 - s, 1)
            ok = sub < 8 - s
        else:
            a_sh = pltpu.roll(a, s, 1)
            b_sh = pltpu.roll(b, s, 1)
            ok = sub >= s
        b = a * jnp.where(ok, b_sh, 0.0) + b
        a = a * jnp.where(ok, a_sh, 1.0)
    a = a.reshape(t_len, width)
    b = b.reshape(t_len, width)
    hs = [None] * n_groups
    for g in (range(n_groups - 1, -1, -1) if rev else range(n_groups)):
        h_g = a[8 * g:8 * g + 8, :] * carry + b[8 * g:8 * g + 8, :]
        hs[g] = h_g
        carry = h_g[0:1, :] if rev else h_g[7:8, :]
    return jnp.concatenate(hs, axis=0), carry


def _rglru_kernel(*refs, rev, row_len, final):
    if final:
        (x_ref, gate_ref, cw_ref, cb_ref, wa_ref, ba_ref, wx_ref, bx_ref, lam_ref, h0_ref, prev_ref,
         o_ref, st_ref, carry) = refs
    else:
        (x_ref, cw_ref, cb_ref, wa_ref, ba_ref, wx_ref, bx_ref, lam_ref, h0_ref,
         o_ref, st_ref, carry) = refs
    i = pl.program_id(1)

    @pl.when(i == 0)
    def _():
        carry[...] = h0_ref[0]

    xc = _short_conv(x_ref[0], cw_ref[...], row_len, 2) + cb_ref[...]
    xb = xc.astype(BF16)
    blocks = [slice(n * LANES, (n + 1) * LANES) for n in range(RG_W // LANES)]
    gate_r = jnp.concatenate([jnp.dot(xb[:, bs], wa_ref[n], preferred_element_type=F32)
                              for n, bs in enumerate(blocks)], axis=-1)
    gate_i = jnp.concatenate([jnp.dot(xb[:, bs], wx_ref[n], preferred_element_type=F32)
                              for n, bs in enumerate(blocks)], axis=-1)
    gate_r = _sigmoid(gate_r + ba_ref[...])
    gate_i = _sigmoid(gate_i + bx_ref[...])
    log_a = -RG_C * gate_r * _softplus(-lam_ref[...])
    a = jnp.exp(log_a)
    b = jnp.sqrt(-jnp.tanh(log_a) * (a * a + 1.0)) * (gate_i * xc)
    h, last = _lin_scan(a, b, carry[...], rev)
    carry[...] = last

    @pl.when(i == pl.num_programs(1) - 1)
    def _():
        st_ref[0] = last

    if final:
        gate = gate_ref[0]
        gelu = 0.5 * gate * (1.0 + jnp.tanh(math.sqrt(2.0 / math.pi) * (gate + 0.044715 * gate * gate * gate)))
        o_ref[0] = (gelu * (prev_ref[0] + h)).astype(o_ref.dtype)
    else:
        o_ref[0] = h


def _rglru_pass(p, prm, d, h0, prev, row_len):
    bsz, length, _ = p.shape
    conv_w, conv_b, wa, ba, wx, bx, lam = prm
    final = prev is not None
    rev = d == 1
    t = min(length, 256)
    nt = length // t
    tidx = (lambda i: nt - 1 - i) if rev else (lambda i: i)
    cx = COL_RG // RG_W
    col = lambda c0: pl.BlockSpec((1, t, RG_W), lambda b, i: (b, tidx(i), c0))
    vec = pl.BlockSpec((1, RG_W), lambda b, i: (0, 0))
    mat = pl.BlockSpec((RG_W // LANES, LANES, LANES), lambda b, i: (0, 0, 0))
    in_specs = [col(cx)]
    args = [p]
    if final:
        in_specs.append(col(cx + 1))
        args.append(p)
    in_specs += [pl.BlockSpec((4, RG_W), lambda b, i: (0, 0)), vec, mat, vec, mat, vec, vec,
                 pl.BlockSpec((1, 1, RG_W), lambda b, i: (b, 0, 0))]
    args += [conv_w, conv_b.reshape(1, RG_W), wa[d].astype(BF16), ba[d].reshape(1, RG_W),
             wx[d].astype(BF16), bx[d].reshape(1, RG_W), lam[d].reshape(1, RG_W), h0]
    if final:
        in_specs.append(pl.BlockSpec((1, t, RG_W), lambda b, i: (b, tidx(i), 0)))
        args.append(prev)
    out, st = pl.pallas_call(
        functools.partial(_rglru_kernel, rev=rev, row_len=row_len, final=final),
        out_shape=[jax.ShapeDtypeStruct((bsz, length, RG_W), BF16 if final else F32),
                   jax.ShapeDtypeStruct((bsz, 1, RG_W), F32)],
        grid=(bsz, nt),
        in_specs=in_specs,
        out_specs=[pl.BlockSpec((1, t, RG_W), lambda b, i: (b, tidx(i), 0)),
                   pl.BlockSpec((1, 1, RG_W), lambda b, i: (b, 0, 0))],
        scratch_shapes=[pltpu.VMEM((1, RG_W), F32)],
        compiler_params=_cparams(2),
        name="rglru_bwd" if rev else "rglru_fwd",
    )(*args)
    return out, st


def _rglru_mixer(p, prm, h0s, row_len):
    bsz = p.shape[0]
    if h0s is None:
        h0s = [jnp.zeros((bsz, 1, RG_W), F32)] * 2
    h_f, s_f = _rglru_pass(p, prm, 0, h0s[0], None, row_len)
    out, s_b = _rglru_pass(p, prm, 1, h0s[1], h_f, row_len)
    return out, [s_f, s_b]


def _tri_masks(n, rev):
    row = lax.broadcasted_iota(jnp.int32, (n, n), 0)
    col = lax.broadcasted_iota(jnp.int32, (n, n), 1)
    if rev:
        return row <= col, row < col
    return row >= col, row > col


def _dot_x3(a, b):
    a_hi = a.astype(BF16)
    a_lo = (a - a_hi.astype(F32)).astype(BF16)
    b_hi = b.astype(BF16)
    b_lo = (b - b_hi.astype(F32)).astype(BF16)
    mm = lambda p, q: jnp.dot(p, q, preferred_element_type=F32)
    return mm(a_hi, b_hi) + (mm(a_hi, b_lo) + mm(a_lo, b_hi))


GD_SOLVE_DOT = _dot_x3


def _unit_solve(neg_n, rhs, steps, dot):
    x = rhs
    m = neg_n
    for s in range(steps):
        x = x + dot(m, x)
        if s + 1 < steps:
            m = dot(m, m)
    return x


def _chunk_cumsum(x, rev):
    t_len = x.shape[0]
    pos = lax.broadcasted_iota(jnp.int32, x.shape, 0) & (CHUNK - 1)
    s = 1
    while s < CHUNK:
        if rev:
            x = x + jnp.where(pos < CHUNK - s, pltpu.roll(x, t_len - s, 0), 0.0)
        else:
            x = x + jnp.where(pos >= s, pltpu.roll(x, s, 0), 0.0)
        s *= 2
    return x


def _lane_form(x):
    sel = (lax.broadcasted_iota(jnp.int32, x.shape, 1) == 0).astype(BF16)
    p1 = x.astype(BF16)
    r1 = x - p1.astype(F32)
    p2 = r1.astype(BF16)
    p3 = (r1 - p2.astype(F32)).astype(BF16)
    nt = lambda p: lax.dot_general(sel, p, (((1,), (1,)), ((), ())), preferred_element_type=F32)
    return nt(p1) + nt(p2) + nt(p3)


def _dot_exact_rhs(a, b_exact):
    a_hi = a.astype(BF16)
    a_lo = (a - a_hi.astype(F32)).astype(BF16)
    b = b_exact.astype(BF16)
    return (jnp.dot(a_hi, b, preferred_element_type=F32) + jnp.dot(a_lo, b, preferred_element_type=F32))


def _gdn_kernel(*refs, rev, row_len, final, d):
    if final:
        (q_ref, k_ref, v_ref, z_ref, gb_ref, cq_ref, ck_ref, cv_ref, alog_ref, dtb_ref, ng_ref, s0_ref,
         prev_ref, o_ref, st_ref, state) = refs
    else:
        (q_ref, k_ref, v_ref, gb_ref, cq_ref, ck_ref, cv_ref, alog_ref, dtb_ref, ng_ref, s0_ref,
         o_ref, st_ref, state) = refs
    grp = pl.program_id(1)
    i = pl.program_id(2)
    n_heads = q_ref.shape[2] // GD_HEAD

    @pl.when(i == 0)
    def _():
        state[...] = s0_ref[0]

    qc = _silu(_short_conv(q_ref[0], cq_ref[...], row_len, 2))
    kc = _silu(_short_conv(k_ref[0], ck_ref[...], row_len, 2))
    vc = _silu(_short_conv(v_ref[0], cv_ref[...], row_len, 2))
    gbb = gb_ref[0]
    lane = lax.broadcasted_iota(jnp.int32, gbb.shape, 1)
    t_len = qc.shape[0]
    n_chunks = t_len // CHUNK
    incl, strict = _tri_masks(CHUNK, rev)
    order = list(range(n_chunks - 1, -1, -1) if rev else range(n_chunks))
    bf = lambda x: x.astype(BF16)

    heads = []
    for hh in range(n_heads):
        hs = slice(hh * GD_HEAD, (hh + 1) * GD_HEAD)
        head = grp * n_heads + hh
        q_h, k_h = qc[:, hs], kc[:, hs]
        q_h = q_h * lax.rsqrt(jnp.sum(q_h * q_h, axis=-1, keepdims=True) + 1e-6) * (GD_HEAD ** -0.5)
        k_h = k_h * lax.rsqrt(jnp.sum(k_h * k_h, axis=-1, keepdims=True) + 1e-6)
        g_raw = jnp.sum(jnp.where(lane == d * GD_HEADS + head, gbb, 0.0), axis=-1, keepdims=True)
        b_raw = jnp.sum(jnp.where(lane == (2 + d) * GD_HEADS + head, gbb, 0.0), axis=-1, keepdims=True)
        g = -jnp.exp(alog_ref[hh]) * _softplus(g_raw + dtb_ref[hh])
        heads.append(dict(q=q_h, k=k_h, v=vc[:, hs], beta=_sigmoid(b_raw), gc=_chunk_cumsum(g, rev)))

    pre = {}
    s_cur = [state[hh] for hh in range(n_heads)]
    os = {}

    def decay_tiles(c):
        for hh in range(n_heads):
            hd = heads[hh]
            sl = slice(c * CHUNK, (c + 1) * CHUNK)
            q_c, k_c, v_c, b_c, gc = hd["q"][sl], hd["k"][sl], hd["v"][sl], hd["beta"][sl], hd["gc"][sl]
            g_row = _lane_form(gc)
            kb = k_c * b_c
            e_gc = jnp.exp(gc)
            g_last = gc[0:1, :] if rev else gc[CHUNK - 1:CHUNK, :]
            pre[hh, c] = dict(decay_in=jnp.exp(jnp.where(incl, gc[:, :CHUNK] - g_row, -jnp.inf)),
                              q=q_c, k=k_c, kb=kb, x=jnp.concatenate([v_c * b_c, kb * e_gc], axis=-1),
                              qg=bf(q_c * e_gc), k_dec=bf(k_c * jnp.exp(g_last - gc)), dec=jnp.exp(g_last))

    def score_tiles(c):
        for hh in range(n_heads):
            p = pre[hh, c]
            p["m"] = -(_dot_nt(p["kb"], p["k"]) * jnp.where(strict, p["decay_in"], 0.0))
            p["a_qk"] = bf(_dot_nt(p["q"], p["k"]) * p["decay_in"])

    def solve_step(c, s):
        for hh in range(n_heads):
            p = pre[hh, c]
            p["x"] = p["x"] + GD_SOLVE_DOT(p["m"], p["x"])
            if s < 5:
                p["m"] = GD_SOLVE_DOT(p["m"], p["m"])

    def recur(c):
        for hh in range(n_heads):
            p = pre.pop((hh, c))
            x = p["x"]
            t2 = _dot(jnp.concatenate([bf(x[:, GD_HEAD:]), p["qg"]], axis=0), s_cur[hh])
            v_new = bf(x[:, :GD_HEAD] - t2[:CHUNK])
            os[hh, c] = t2[CHUNK:] + _dot(p["a_qk"], v_new)
            s_cur[hh] = s_cur[hh] * p["dec"] + _dot_tn(p["k_dec"], v_new)

    phases = [decay_tiles, score_tiles] + [functools.partial(solve_step, s=s) for s in range(6)] + [recur]
    for slot in range(n_chunks + len(phases) - 1):
        for ph in range(len(phases) - 1, -1, -1):
            if 0 <= slot - ph < n_chunks:
                phases[ph](order[slot - ph])

    for hh in range(n_heads):
        hs = slice(hh * GD_HEAD, (hh + 1) * GD_HEAD)
        o_h = jnp.concatenate([os[hh, c] for c in range(n_chunks)], axis=0)
        if final:
            o_t = prev_ref[0, :, hs] + o_h
            o_t = o_t * lax.rsqrt(jnp.mean(o_t * o_t, axis=-1, keepdims=True) + NORM_EPS) * ng_ref[...]
            o_ref[0, :, hs] = (o_t * _silu(z_ref[0, :, hs])).astype(o_ref.dtype)
        else:
            o_ref[0, :, hs] = o_h
        state[hh] = s_cur[hh]

    @pl.when(i == pl.num_programs(2) - 1)
    def _():
        for hh in range(n_heads):
            st_ref[0, hh] = s_cur[hh]


GD_HEADS_PER_STEP = 2
GD_TILE = 512


def _gdn_pass(p, prm, d, s0, prev, row_len):
    bsz, length, _ = p.shape
    conv_w, a_log, dt_bias, norm_g = prm
    final = prev is not None
    rev = d == 1
    t = min(length, GD_TILE)
    nt = length // t
    tidx = (lambda i: nt - 1 - i) if rev else (lambda i: i)
    nh = GD_HEADS_PER_STEP
    wd = nh * GD_HEAD
    ng = GD_W // wd
    c0 = COL_GD // wd
    col = lambda off: pl.BlockSpec((1, t, wd), lambda b, h, i: (b, tidx(i), c0 + off * ng + h))
    cw = lambda off: pl.BlockSpec((4, wd), lambda b, h, i: (0, off * ng + h))
    hvec = pl.BlockSpec((nh, 1, LANES), lambda b, h, i: (h, 0, 0))
    in_specs = [col(0), col(1), col(2)]
    args = [p, p, p]
    if final:
        in_specs.append(col(3))
        args.append(p)
    in_specs += [pl.BlockSpec((1, t, LANES), lambda b, h, i: (b, tidx(i), COL_GD_GB // LANES)),
                 cw(0), cw(1), cw(2), hvec, hvec,
                 pl.BlockSpec((1, LANES), lambda b, h, i: (0, 0)),
                 pl.BlockSpec((1, nh, GD_HEAD, GD_HEAD), lambda b, h, i: (b, h, 0, 0))]
    bcast = lambda v: jnp.broadcast_to(v.reshape(GD_HEADS, 1, 1), (GD_HEADS, 1, LANES))
    args += [p, conv_w, conv_w, conv_w, bcast(a_log[d]), bcast(dt_bias[d]), norm_g.reshape(1, GD_HEAD), s0]
    if final:
        in_specs.append(pl.BlockSpec((1, t, wd), lambda b, h, i: (b, tidx(i), h)))
        args.append(prev)
    out, st = pl.pallas_call(
        functools.partial(_gdn_kernel, rev=rev, row_len=row_len, final=final, d=d),
        out_shape=[jax.ShapeDtypeStruct((bsz, length, GD_W), BF16 if final else F32),
                   jax.ShapeDtypeStruct((bsz, GD_HEADS, GD_HEAD, GD_HEAD), F32)],
        grid=(bsz, ng, nt),
        in_specs=in_specs,
        out_specs=[pl.BlockSpec((1, t, wd), lambda b, h, i: (b, tidx(i), h)),
                   pl.BlockSpec((1, nh, GD_HEAD, GD_HEAD), lambda b, h, i: (b, h, 0, 0))],
        scratch_shapes=[pltpu.VMEM((nh, GD_HEAD, GD_HEAD), F32)],
        compiler_params=_cparams(3),
        name="gdn_bwd" if rev else "gdn_fwd",
    )(*args)
    return out, st


def _gdn_mixer(p, prm, s0s, row_len):
    bsz = p.shape[0]
    if s0s is None:
        s0s = [jnp.zeros((bsz, GD_HEADS, GD_HEAD, GD_HEAD), F32)] * 2
    o_f, s_f = _gdn_pass(p, prm, 0, s0s[0], None, row_len)
    out, s_b = _gdn_pass(p, prm, 1, s0s[1], o_f, row_len)
    return out, [s_f, s_b]


def _stack_heads(x, lo):
    return jnp.concatenate([jnp.where(lo, x, 0.0), jnp.where(lo, 0.0, x)], axis=0)


def _rwkv_kernel(*refs, rev, final):
    if final:
        (r_ref, k_ref, v_ref, lo_ref, mur_ref, muk_ref, muv_ref, mul_ref, w0_ref, wup_ref, a0_ref, aup_ref,
         gup_ref, kk_ref, ka_ref, rk_ref, lng_ref, lnb_ref, s0_ref, prev_ref,
         o_ref, st_ref, state, c_r, c_k, c_v, c_l) = refs
    else:
        (r_ref, k_ref, v_ref, lo_ref, mur_ref, muk_ref, muv_ref, mul_ref, w0_ref, wup_ref, a0_ref, aup_ref,
         gup_ref, kk_ref, ka_ref, rk_ref, lng_ref, lnb_ref, s0_ref,
         o_ref, st_ref, state, c_r, c_k, c_v, c_l) = refs
    i = pl.program_id(2)

    @pl.when(i == 0)
    def _():
        state[...] = s0_ref[0]
        c_r[...] = jnp.zeros_like(c_r)
        c_k[...] = jnp.zeros_like(c_k)
        c_v[...] = jnp.zeros_like(c_v)
        c_l[...] = jnp.zeros_like(c_l)

    t_len = r_ref.shape[1]

    def shifted(x_ref, carry, mu_ref):
        x = x_ref[0]
        row = lax.broadcasted_iota(jnp.int32, x.shape, 0)
        if rev:
            prev = jnp.where(row == t_len - 1, carry[...], pltpu.roll(x, t_len - 1, 0))
            carry[...] = x[0:1, :]
        else:
            prev = jnp.where(row == 0, carry[...], pltpu.roll(x, 1, 0))
            carry[...] = x[t_len - 1:t_len, :]
        return x + (prev - x) * mu_ref[...]

    r = shifted(r_ref, c_r, mur_ref)
    k = shifted(k_ref, c_k, muk_ref)
    v = shifted(v_ref, c_v, muv_ref)
    lora = shifted(lo_ref, c_l, mul_ref)
    xw, xa, xg = lora[:, 0:64], lora[:, 64:128], lora[:, 128:256]

    width = r_ref.shape[2]
    n_pairs = width // LANES
    lane = lax.broadcasted_iota(jnp.int32, (1, LANES), 1)
    lo = lane < RW_HEAD
    head_sum = (lax.broadcasted_iota(jnp.int32, (width, width), 0) // RW_HEAD
                == lax.broadcasted_iota(jnp.int32, (width, width), 1) // RW_HEAD).astype(F32)
    rowh = lax.broadcasted_iota(jnp.int32, (LANES, LANES), 0) // RW_HEAD
    colh = lax.broadcasted_iota(jnp.int32, (LANES, LANES), 1) // RW_HEAD

    lw = -math.exp(-0.5) * _sigmoid(w0_ref[...] + _dot(jnp.tanh(xw), wup_ref[...]))
    a = _sigmoid(a0_ref[...] + _dot(xa, aup_ref[...]))
    kk = k * kk_ref[...]
    kappa = kk / jnp.maximum(jnp.sqrt(_dot_exact_rhs(kk * kk, head_sum)), 1e-12)
    kt = k * (1.0 + (a - 1.0) * ka_ref[...])
    gate = _dot(_sigmoid(xg), gup_ref[...])
    bonus = _dot_exact_rhs(r * kt * rk_ref[...], head_sum) * v

    n_chunks = t_len // CHUNK
    lg_all = _chunk_cumsum(lw, rev)
    rowt = lax.broadcasted_iota(jnp.int32, (LANES, LANES), 0) & (CHUNK - 1)
    colt = lax.broadcasted_iota(jnp.int32, (LANES, LANES), 1) & (CHUNK - 1)
    same = rowh == colh
    incl2 = ((rowt <= colt) if rev else (rowt >= colt)) & same
    strict2 = ((rowt < colt) if rev else (rowt > colt)) & same
    order = list(range(n_chunks - 1, -1, -1) if rev else range(n_chunks))
    bf = lambda x: x.astype(BF16)

    pre = {}
    s_cur = [state[pp] for pp in range(n_pairs)]
    ys = {}

    def scores(c):
        for pp in range(n_pairs):
            sl = (slice(c * CHUNK, (c + 1) * CHUNK), slice(pp * LANES, (pp + 1) * LANES))
            lw_c, kap_c, a_c = lw[sl], kappa[sl], a[sl]
            lg = lg_all[sl]
            lg_tot = lg[0:1, :] if rev else lg[CHUNK - 1:CHUNK, :]
            e_neg = jnp.exp(-lg)
            e_rem = jnp.exp(lg_tot - lg)
            p_raw = -(kap_c * a_c)
            q2 = bf(_stack_heads(kap_c * jnp.exp(lg - lw_c), lo))
            p2 = bf(_stack_heads(p_raw * e_neg, lo))
            k2 = bf(_stack_heads(kt[sl] * e_neg, lo))
            r2 = bf(_stack_heads(r[sl] * jnp.exp(lg), lo))
            pre[pp, c] = dict(
                q2=q2, r2=r2, v2=bf(_stack_heads(v[sl], lo)),
                pt2=bf(_stack_heads(p_raw * e_rem, lo)), kt2=bf(_stack_heads(kt[sl] * e_rem, lo)),
                m=jnp.where(strict2, _dot_nt(q2, p2), 0.0),
                a_qk=jnp.where(strict2, _dot_nt(q2, k2), 0.0),
                a_rp=bf(jnp.where(incl2, _dot_nt(r2, p2), 0.0)),
                a_rk=jnp.where(incl2, _dot_nt(r2, k2), 0.0),
                dec=jnp.broadcast_to(jnp.exp(lg_tot), (LANES, LANES)).T)

    def local_terms(c):
        for pp in range(n_pairs):
            p = pre[pp, c]
            p["x"] = jnp.concatenate([p["q2"].astype(F32), _dot(p["a_qk"], p["v2"])], axis=-1)
            p["y_loc"] = _dot(p["a_rk"], p["v2"])
            p["s_loc"] = _dot_tn(p["kt2"], p["v2"])

    def solve_step(c, s):
        for pp in range(n_pairs):
            p = pre[pp, c]
            p["x"] = p["x"] + _dot(p["m"], p["x"])
            if s < 5:
                p["m"] = _dot(p["m"], p["m"])

    def recur(c):
        for pp in range(n_pairs):
            p = pre.pop((pp, c))
            x = p["x"]
            t2 = _dot(jnp.concatenate([bf(x[:, :LANES]), p["r2"]], axis=0), s_cur[pp])
            u2 = bf(t2[:LANES] + x[:, LANES:])
            y2 = t2[LANES:] + _dot(p["a_rp"], u2) + p["y_loc"]
            s_cur[pp] = s_cur[pp] * p["dec"] + _dot_tn(p["pt2"], u2) + p["s_loc"]
            ys[pp, c] = y2[:CHUNK] + y2[CHUNK:]

    phases = [scores, local_terms] + [functools.partial(solve_step, s=s) for s in range(6)] + [recur]
    for slot in range(n_chunks + len(phases) - 1):
        for ph in range(len(phases) - 1, -1, -1):
            if 0 <= slot - ph < n_chunks:
                phases[ph](order[slot - ph])

    y = jnp.concatenate([jnp.concatenate([ys[pp, c] for c in range(n_chunks)], axis=0)
                         for pp in range(n_pairs)], axis=1)
    mean = _dot_exact_rhs(y, head_sum) * (1.0 / RW_HEAD)
    yc = y - mean
    var = _dot_exact_rhs(yc * yc, head_sum) * (1.0 / RW_HEAD)
    yn = yc * lax.rsqrt(var + RW_GN_EPS) * lng_ref[...] + lnb_ref[...]
    out = (yn + bonus) * gate
    if final:
        o_ref[0] = (prev_ref[0] + out).astype(o_ref.dtype)
    else:
        o_ref[0] = out
    for pp in range(n_pairs):
        state[pp] = s_cur[pp]

    @pl.when(i == pl.num_programs(2) - 1)
    def _():
        for pp in range(n_pairs):
            st_ref[0, pp] = s_cur[pp]


RW_PAIRS_PER_STEP = 2
RW_TILE = 512


def _rwkv_pass(p, prm, d, s0, prev):
    bsz, length, _ = p.shape
    mu, w0, w_up, a0, a_up, g_up, k_k, k_a, r_k, ln_g, ln_b = [t[d] for t in prm]
    final = prev is not None
    rev = d == 1
    t = min(length, RW_TILE)
    nt = length // t
    tidx = (lambda i: nt - 1 - i) if rev else (lambda i: i)
    npp = RW_PAIRS_PER_STEP
    wd = npp * LANES
    c0 = COL_RW // wd
    ng = RW_W // wd
    col = lambda off: pl.BlockSpec((1, t, wd), lambda b, n, i: (b, tidx(i), c0 + off * ng + n))
    vec = pl.BlockSpec((1, wd), lambda b, n, i: (0, n))
    whole = lambda shape: pl.BlockSpec(shape, lambda b, n, i: (0,) * len(shape))
    row = lambda x: x.reshape(1, -1)
    in_specs = [col(0), col(1), col(2),
                pl.BlockSpec((1, t, 256), lambda b, n, i: (b, tidx(i), COL_RW_LORA // 256)),
                vec, vec, vec, whole((1, 256)),
                vec, pl.BlockSpec((64, wd), lambda b, n, i: (0, n)),
                vec, pl.BlockSpec((64, wd), lambda b, n, i: (0, n)),
                pl.BlockSpec((LANES, wd), lambda b, n, i: (0, n)),
                vec, vec, vec, vec, vec,
                pl.BlockSpec((1, npp, LANES, LANES), lambda b, n, i: (b, n, 0, 0))]
    args = [p, p, p, p,
            row(mu[0:512]), row(mu[512:1024]), row(mu[1024:1536]), row(mu[1536:1792]),
            row(w0), w_up.astype(BF16), row(a0), a_up.astype(BF16), g_up.astype(BF16),
            row(k_k), row(k_a), row(r_k), row(ln_g), row(ln_b), s0]
    if final:
        in_specs.append(pl.BlockSpec((1, t, wd), lambda b, n, i: (b, tidx(i), n)))
        args.append(prev)
    out, st = pl.pallas_call(
        functools.partial(_rwkv_kernel, rev=rev, final=final),
        out_shape=[jax.ShapeDtypeStruct((bsz, length, RW_W), BF16 if final else F32),
                   jax.ShapeDtypeStruct((bsz, 4, LANES, LANES), F32)],
        grid=(bsz, ng, nt),
        in_specs=in_specs,
        out_specs=[pl.BlockSpec((1, t, wd), lambda b, n, i: (b, tidx(i), n)),
                   pl.BlockSpec((1, npp, LANES, LANES), lambda b, n, i: (b, n, 0, 0))],
        scratch_shapes=[pltpu.VMEM((npp, LANES, LANES), F32), pltpu.VMEM((1, wd), F32),
                        pltpu.VMEM((1, wd), F32), pltpu.VMEM((1, wd), F32),
                        pltpu.VMEM((1, 256), F32)],
        compiler_params=_cparams(3),
        name="rwkv_bwd" if rev else "rwkv_fwd",
    )(*args)
    return out, st


def _rwkv_mixer(p, prm, s0s):
    bsz = p.shape[0]
    if s0s is None:
        s0s = [jnp.zeros((bsz, 4, LANES, LANES), F32)] * 2
    o_f, s_f = _rwkv_pass(p, prm, 0, s0s[0], None)
    out, s_b = _rwkv_pass(p, prm, 1, s0s[1], o_f)
    return out, [s_f, s_b]


def _dft_tables(length):
    n = 2 * length
    nfp = -(-(length + 1) // LANES) * LANES
    kf = jnp.arange(nfp, dtype=jnp.int32)[:, None]
    s1 = jnp.arange(n // 64, dtype=jnp.int32)[None, :]
    s0 = jnp.arange(64, dtype=jnp.int32)[None, :]
    ang_a = (2.0 * math.pi / n) * ((kf * s1 * 64) % n).astype(F32)
    ang_b = (2.0 * math.pi / n) * ((kf * s0) % n).astype(F32)
    ok = (kf <= length).astype(F32)
    ca, sa = (jnp.cos(ang_a) * ok)[:, :, None], (jnp.sin(ang_a) * ok)[:, :, None]
    cb, sb = jnp.cos(ang_b)[:, None, :], jnp.sin(ang_b)[:, None, :]
    tab_c = (ca * cb - sa * sb).reshape(nfp, n).astype(BF16)
    tab_s = (-(sa * cb + ca * sb)).reshape(nfp, n).astype(BF16)
    kk = kf[:, 0]
    wk = jnp.where((kk == 0) | (kk == length), 1.0, 2.0) * (kk <= length) / n
    return tab_c, tab_s, jnp.broadcast_to(wk[:, None], (nfp, LANES)).astype(F32)


def _hyena_filter(length, w1, b1, w2, b2, w3, freq):
    t = jnp.arange(length, dtype=F32)
    z = t / max(length - 1, 1)
    bands = jnp.linspace(1e-4, HY_BANDS - 1, HY_BANDS, dtype=F32)
    ang = (2.0 * math.pi / length) * t[:, None] * bands[None, :]
    feat = jnp.concatenate([z[:, None], jnp.cos(ang), -jnp.sin(ang)], axis=-1)
    h = jnp.sin(freq[0] * (feat @ w1 + b1))
    h = jnp.sin(freq[1] * (h @ w2 + b2))
    h = (h @ w3).astype(F32)
    deltas = jnp.abs(jnp.linspace(math.log(1e-2) / 1.5, math.log(1e-2) / 0.3, HY_W, dtype=F32))
    h = h * jnp.exp(-z[:, None] * jnp.tile(deltas, 2)[None, :])
    h_fwd = h[:, :HY_W]
    h_bwd = jnp.where(t[:, None] > 0, h[:, HY_W:], 0.0)
    norm = jnp.sum(jnp.abs(h_fwd) + jnp.abs(h_bwd), axis=0, keepdims=True)
    return (h_fwd + h_bwd) / norm, (h_fwd - h_bwd) / norm


def _hy_zin_kernel(v_ref, x1_ref, cw_ref, o_ref, *, row_len):
    vc = _short_conv(v_ref[0], cw_ref[:, 0:512], row_len, 1)
    x1c = _short_conv(x1_ref[0], cw_ref[:, 1024:1536], row_len, 1)
    o_ref[0] = (x1c * vc).astype(o_ref.dtype)


def _hy_zin(p, conv_w, row_len):
    bsz, length, _ = p.shape
    t = min(length, 256)
    return pl.pallas_call(
        functools.partial(_hy_zin_kernel, row_len=row_len),
        out_shape=jax.ShapeDtypeStruct((bsz, length, HY_W), BF16),
        grid=(bsz, length // t),
        in_specs=[pl.BlockSpec((1, t, 512), lambda b, i: (b, i, 0)),
                  pl.BlockSpec((1, t, 512), lambda b, i: (b, i, 2)),
                  pl.BlockSpec((3, 1536), lambda b, i: (0, 0))],
        out_specs=pl.BlockSpec((1, t, 512), lambda b, i: (b, i, 0)),
        compiler_params=_cparams(2),
        name="hyena_zin",
    )(p, p, conv_w)


def _dft_fwd_kernel(*refs, mult):
    if mult:
        c_ref, s_ref, z_ref, fr_ref, fi_ref, yr_ref, yi_ref = refs
        z_cos = z_sin = z_ref[0].astype(BF16)
    else:
        c_ref, s_ref, zc_ref, zs_ref, wk_ref, yr_ref, yi_ref = refs
        z_cos, z_sin = zc_ref[0].astype(BF16), zs_ref[0].astype(BF16)
    zr = jnp.dot(c_ref[...], z_cos, preferred_element_type=F32)
    zi = jnp.dot(s_ref[...], z_sin, preferred_element_type=F32)
    if mult:
        fr, fi = fr_ref[...], fi_ref[...]
        zr, zi = zr * fr - zi * fi, zr * fi + zi * fr
    else:
        zr, zi = zr * wk_ref[:, 0:1], zi * wk_ref[:, 0:1]
    yr_ref[0] = zr.astype(yr_ref.dtype)
    yi_ref[0] = zi.astype(yi_ref.dtype)


def _dft_fwd(tab_c, tab_s, z, spec, wk=None):
    mult = spec is not None
    zs = (z,) if mult else z
    bsz, klen, _ = zs[0].shape
    nfp = tab_c.shape[0]
    tm = 384 if nfp % 384 == 0 else LANES
    in_specs = [pl.BlockSpec((tm, klen), lambda b, i: (i, 0)),
                pl.BlockSpec((tm, klen), lambda b, i: (i, 0))]
    in_specs += [pl.BlockSpec((1, klen, 512), lambda b, i: (b, 0, 0))] * len(zs)
    args = [tab_c, tab_s, *zs]
    if mult:
        in_specs += [pl.BlockSpec((tm, 512), lambda b, i: (i, 0))] * 2
        args += list(spec)
    else:
        in_specs.append(pl.BlockSpec((tm, LANES), lambda b, i: (i, 0)))
        args.append(wk)
    odt = BF16 if mult else F32
    return pl.pallas_call(
        functools.partial(_dft_fwd_kernel, mult=mult),
        out_shape=[jax.ShapeDtypeStruct((bsz, nfp, 512), odt)] * 2,
        grid=(bsz, nfp // tm),
        in_specs=in_specs,
        out_specs=[pl.BlockSpec((1, tm, 512), lambda b, i: (b, i, 0))] * 2,
        compiler_params=_cparams(2),
        name="hyena_dft_mul" if mult else "hyena_dft_filter",
    )(*args)


def _dft_inv_kernel(ci_ref, si_ref, yr_ref, yi_ref, v_ref, x0_ref, x1_ref, cw_ref, skip_ref, o_ref, *, row_len):
    y = jnp.dot(ci_ref[...], yr_ref[0], preferred_element_type=F32)
    y += jnp.dot(si_ref[...], yi_ref[0], preferred_element_type=F32)
    vc = _short_conv(v_ref[0], cw_ref[:, 0:512], row_len, 1)
    x0c = _short_conv(x0_ref[0], cw_ref[:, 512:1024], row_len, 1)
    x1c = _short_conv(x1_ref[0], cw_ref[:, 1024:1536], row_len, 1)
    zin = x1c * vc
    o_ref[0] = (x0c * (y + zin * skip_ref[...])).astype(o_ref.dtype)


def _dft_inv(tab_c, tab_s, yr, yi, p, conv_w, skip, row_len):
    bsz, length, _ = p.shape
    nfp = tab_c.shape[0]
    t = min(length, 256)
    pcol = lambda c: pl.BlockSpec((1, t, 512), lambda b, i: (b, i, c))
    return pl.pallas_call(
        functools.partial(_dft_inv_kernel, row_len=row_len),
        out_shape=jax.ShapeDtypeStruct((bsz, length, HY_W), BF16),
        grid=(bsz, length // t),
        in_specs=[pl.BlockSpec((t, nfp), lambda b, i: (i, 0)),
                  pl.BlockSpec((t, nfp), lambda b, i: (i, 0)),
                  pl.BlockSpec((1, nfp, 512), lambda b, i: (b, 0, 0)),
                  pl.BlockSpec((1, nfp, 512), lambda b, i: (b, 0, 0)),
                  pcol(0), pcol(1), pcol(2),
                  pl.BlockSpec((3, 1536), lambda b, i: (0, 0)),
                  pl.BlockSpec((1, 512), lambda b, i: (0, 0))],
        out_specs=pl.BlockSpec((1, t, 512), lambda b, i: (b, i, 0)),
        compiler_params=_cparams(2),
        name="hyena_idft_gate",
    )(tab_c, tab_s, yr, yi, p, p, p, conv_w, skip.reshape(1, HY_W))


def _hyena_mixer(p, prm, tables, row_len):
    conv_w, w1, b1, w2, b2, w3, freq, skip = prm
    length = p.shape[1]
    tab_c, tab_s, wk = tables
    f_cos, f_sin = _hyena_filter(length, w1, b1, w2, b2, w3, freq)
    spec = _dft_fwd(tab_c, tab_s, (f_cos[None], f_sin[None]), None, wk)
    spec = (spec[0][0], spec[1][0])
    zin = _hy_zin(p, conv_w, row_len)
    yr, yi = _dft_fwd(tab_c, tab_s, zin, spec)
    return _dft_inv(tab_c, tab_s, yr, yi, p, conv_w, skip, row_len)


def _permute_w_in(w_in):
    hy = w_in[:, 0:1536]
    rw = w_in[:, 1536:3328]
    gd = w_in[:, 3328:5392]
    rg = w_in[:, 5392:6416]
    pad = jnp.zeros((w_in.shape[0], N_PROJ - 6416), w_in.dtype)
    return jnp.concatenate([hy, rw[:, :1536], gd[:, :2048], rg, rw[:, 1536:], gd[:, 2048:], pad],
                           axis=-1).astype(BF16)


def kernel(x, c, ctx, c_ctx, ada_w, ada_b, norm_mix_g, norm_mlp_g, w_in, w_out,
           hy_conv, hy_w1, hy_b1, hy_w2, hy_b2, hy_w3, hy_freq, hy_skip,
           rw_mu, rw_w0, rw_w_up, rw_a0, rw_a_up, rw_g_up, rw_k_k, rw_k_a, rw_r_k, rw_ln_g, rw_ln_b,
           gd_conv, gd_a_log, gd_dt_bias, gd_norm_g,
           rg_conv, rg_conv_b, rg_wa, rg_ba, rg_wx, rg_bx, rg_lambda,
           mlp_w1, mlp_w2, final_norm_g):
    bsz, seq, _ = x.shape
    ctx_len = ctx.shape[1]
    depth = ada_w.shape[0]
    tables_x = _dft_tables(seq)
    tables_c = _dft_tables(ctx_len)
    cond8 = jnp.concatenate([c, c_ctx[None, :], jnp.zeros((8 - bsz - 1, D_MODEL), F32)], axis=0)
    for l in range(depth):
        last = l == depth - 1
        mod = _modulation(cond8, ada_w, ada_b[l], l)
        mod_x = mod[:bsz].reshape(bsz, 6, D_MODEL)
        mod_c = mod[bsz:bsz + 1].reshape(1, 6, D_MODEL)
        w_in_bf = _permute_w_in(w_in[l])
        w_out_bf = w_out[l].astype(BF16)
        w1_bf = mlp_w1[l].astype(BF16)
        w2_bf = mlp_w2[l].astype(BF16)
        px = _inproj(x, norm_mix_g[l], mod_x, w_in_bf)
        pc = _inproj(ctx, norm_mix_g[l], mod_c, w_in_bf)
        hy_prm = (hy_conv[l], hy_w1[l], hy_b1[l], hy_w2[l], hy_b2[l], hy_w3[l], hy_freq[l], hy_skip[l])
        rw_prm = (rw_mu[l], rw_w0[l], rw_w_up[l], rw_a0[l], rw_a_up[l], rw_g_up[l],
                  rw_k_k[l], rw_k_a[l], rw_r_k[l], rw_ln_g[l], rw_ln_b[l])
        gd_prm = (gd_conv[l], gd_a_log[l], gd_dt_bias[l], gd_norm_g[l])
        rg_prm = (rg_conv[l], rg_conv_b[l], rg_wa[l], rg_ba[l], rg_wx[l], rg_bx[l], rg_lambda[l])
        c_rw, s_rw = _rwkv_mixer(pc, rw_prm, None)
        c_gd, s_gd = _gdn_mixer(pc, gd_prm, None, ctx_len)
        c_rg, s_rg = _rglru_mixer(pc, rg_prm, None, ctx_len)
        x_hy = _hyena_mixer(px, hy_prm, tables_x, GRID_W)
        x_rw, _ = _rwkv_mixer(px, rw_prm, s_rw)
        x_gd, _ = _gdn_mixer(px, gd_prm, s_gd, GRID_W)
        x_rg, _ = _rglru_mixer(px, rg_prm, s_rg, GRID_W)
        x_new = _outproj(x, mod_x, (x_hy, x_rw, x_gd, x_rg), w_out_bf)
        x_new = _mlp(x_new, norm_mlp_g[l], mod_x, w1_bf, w2_bf, final_norm_g if last else None)
        if not last:
            c_hy = _hyena_mixer(pc, hy_prm, tables_c, ctx_len)
            ctx_new = _outproj(ctx, mod_c, (c_hy, c_rw, c_gd, c_rg), w_out_bf)
            ctx = _mlp(ctx_new, norm_mlp_g[l], mod_c, w1_bf, w2_bf, None)
        x = x_new
    return x
```

```python
import functools
import math

import jax
import jax.numpy as jnp
from jax import lax
from jax.experimental import pallas as pl
from jax.experimental.pallas import tpu as pltpu

F32 = jnp.float32
BF16 = jnp.bfloat16
HIGHEST = lax.Precision.HIGHEST

D_MODEL = 2048
GRID_W = 64
HY_W = RW_W = GD_W = RG_W = 512
D_FF = 4 * D_MODEL
NORM_EPS = 1e-6
HY_EMB = 33
HY_BANDS = 16
RW_HEAD = 64
RW_GN_EPS = 64e-5
GD_HEAD = 128
GD_HEADS = 4
CHUNK = 64
RG_C = 8.0
LANES = 128

COL_HY = 0
COL_RW = 1536
COL_GD = 3072
COL_RG = 5120
COL_RW_LORA = 6144
COL_GD_GB = 6400
N_PROJ = 6656

VMEM_LIMIT = 56 * 1024 * 1024


def _cparams(n_axes):
    return pltpu.CompilerParams(dimension_semantics=("arbitrary",) * n_axes,
                                vmem_limit_bytes=VMEM_LIMIT)


def _dot(a, b):
    return jnp.dot(a.astype(BF16), b.astype(BF16), preferred_element_type=F32)


def _dot_nt(a, b):
    return lax.dot_general(a.astype(BF16), b.astype(BF16), (((1,), (1,)), ((), ())),
                           preferred_element_type=F32)


def _dot_tn(a, b):
    return lax.dot_general(a.astype(BF16), b.astype(BF16), (((0,), (0,)), ((), ())),
                           preferred_element_type=F32)


def _dot_hp(a, b):
    return jnp.dot(a, b, precision=HIGHEST, preferred_element_type=F32)


def _dot_nt_hp(a, b):
    return lax.dot_general(a, b, (((1,), (1,)), ((), ())), precision=HIGHEST,
                           preferred_element_type=F32)


def _sigmoid(x):
    return 1.0 / (1.0 + jnp.exp(-x))


def _silu(x):
    return x * _sigmoid(x)


def _softplus(x):
    return jnp.maximum(x, 0.0) + jnp.log1p(jnp.exp(-jnp.abs(x)))


def _short_conv(u, w, row_len, pad_left):
    t_len = u.shape[0]
    pos = lax.broadcasted_iota(jnp.int32, u.shape, 0) & (row_len - 1)
    y = None
    for j in range(w.shape[0]):
        off = j - pad_left
        if off == 0:
            term = u * w[j:j + 1, :]
        else:
            sh = pltpu.roll(u, (-off) % t_len, 0)
            ok = (pos + off >= 0) & (pos + off < row_len)
            term = jnp.where(ok, sh, 0.0) * w[j:j + 1, :]
        y = term if y is None else y + term
    return y


def _mod_kernel(c_ref, w_ref, b_ref, o_ref):
    o_ref[...] = _dot(_silu(c_ref[...]), w_ref[0]) + b_ref[...]


def _modulation(cond8, ada_w, ada_b, layer):
    n = ada_w.shape[2]
    tn = 512
    return pl.pallas_call(
        _mod_kernel,
        out_shape=jax.ShapeDtypeStruct((8, n), F32),
        grid=(n // tn,),
        in_specs=[pl.BlockSpec((8, D_MODEL), lambda j: (0, 0)),
                  pl.BlockSpec((1, D_MODEL, tn), lambda j: (layer, 0, j)),
                  pl.BlockSpec((1, tn), lambda j: (0, j))],
        out_specs=pl.BlockSpec((8, tn), lambda j: (0, j)),
        compiler_params=_cparams(1),
        name="adaln_mod",
    )(cond8, ada_w, ada_b.reshape(1, n))


def _norm_mod(x, g, shift, scale):
    y = x * lax.rsqrt(jnp.mean(x * x, axis=-1, keepdims=True) + NORM_EPS) * g
    return y * (1.0 + scale) + shift


def _inproj_kernel(x_ref, g_ref, mod_ref, w_ref, o_ref, h_scr):
    @pl.when(pl.program_id(2) == 0)
    def _():
        h = _norm_mod(x_ref[0], g_ref[...], mod_ref[0, 0:1, :], mod_ref[0, 1:2, :])
        h_scr[...] = h.astype(BF16)

    o_ref[0] = jnp.dot(h_scr[...], w_ref[0], preferred_element_type=F32)


INPROJ_TN = 512
MLP_TF = 1024


def _tile_major(w, tn):
    k, n = w.shape
    return w.reshape(k, n // tn, tn).transpose(1, 0, 2)


def _inproj(x, g, mod, w_bf):
    bsz, length, _ = x.shape
    tm = min(length, 1024)
    tn = INPROJ_TN
    per_batch = mod.shape[0] == bsz
    return pl.pallas_call(
        _inproj_kernel,
        out_shape=jax.ShapeDtypeStruct((bsz, length, N_PROJ), F32),
        grid=(bsz, length // tm, N_PROJ // tn),
        in_specs=[pl.BlockSpec((1, tm, D_MODEL), lambda b, i, j: (b, i, 0)),
                  pl.BlockSpec((1, D_MODEL), lambda b, i, j: (0, 0)),
                  pl.BlockSpec((1, 6, D_MODEL), (lambda b, i, j: (b, 0, 0)) if per_batch
                               else (lambda b, i, j: (0, 0, 0))),
                  pl.BlockSpec((1, D_MODEL, tn), lambda b, i, j: (j, 0, 0))],
        out_specs=pl.BlockSpec((1, tm, tn), lambda b, i, j: (b, i, j)),
        scratch_shapes=[pltpu.VMEM((tm, D_MODEL), BF16)],
        compiler_params=_cparams(3),
        name="inproj",
    )(x, g.reshape(1, D_MODEL), mod, w_bf)


def _outproj_kernel(x_ref, mod_ref, m0_ref, m1_ref, m2_ref, m3_ref, w_ref, o_ref):
    acc = jnp.dot(m0_ref[0], w_ref[0:512, :], preferred_element_type=F32)
    acc += jnp.dot(m1_ref[0], w_ref[512:1024, :], preferred_element_type=F32)
    acc += jnp.dot(m2_ref[0], w_ref[1024:1536, :], preferred_element_type=F32)
    acc += jnp.dot(m3_ref[0], w_ref[1536:2048, :], preferred_element_type=F32)
    o_ref[0] = x_ref[0] + mod_ref[0, 2:3, :] * acc


def _outproj(x, mod, mixers, w_bf):
    bsz, length, _ = x.shape
    tm = min(length, 512)
    per_batch = mod.shape[0] == bsz
    mspec = pl.BlockSpec((1, tm, 512), lambda b, i: (b, i, 0))
    return pl.pallas_call(
        _outproj_kernel,
        out_shape=jax.ShapeDtypeStruct((bsz, length, D_MODEL), F32),
        grid=(bsz, length // tm),
        in_specs=[pl.BlockSpec((1, tm, D_MODEL), lambda b, i: (b, i, 0)),
                  pl.BlockSpec((1, 6, D_MODEL), (lambda b, i: (b, 0, 0)) if per_batch
                               else (lambda b, i: (0, 0, 0))),
                  mspec, mspec, mspec, mspec,
                  pl.BlockSpec((D_MODEL, D_MODEL), lambda b, i: (0, 0))],
        out_specs=pl.BlockSpec((1, tm, D_MODEL), lambda b, i: (b, i, 0)),
        compiler_params=_cparams(2),
        name="outproj",
    )(x, mod, *mixers, w_bf)


def _mlp_kernel(x_ref, g_ref, mod_ref, w1_ref, w2_ref, fg_ref, o_ref, h_scr, acc_scr, *, final_norm):
    f = pl.program_id(2)

    @pl.when(f == 0)
    def _():
        h = _norm_mod(x_ref[0], g_ref[...], mod_ref[0, 3:4, :], mod_ref[0, 4:5, :])
        h_scr[...] = h.astype(BF16)
        acc_scr[...] = jnp.zeros_like(acc_scr)

    a = jnp.dot(h_scr[...], w1_ref[0], preferred_element_type=F32)
    a = jnp.square(jnp.maximum(a, 0.0)).astype(BF16)
    acc_scr[...] += jnp.dot(a, w2_ref[...], preferred_element_type=F32)

    @pl.when(f == pl.num_programs(2) - 1)
    def _():
        y = x_ref[0] + mod_ref[0, 5:6, :] * acc_scr[...]
        if final_norm:
            y = y * lax.rsqrt(jnp.mean(y * y, axis=-1, keepdims=True) + NORM_EPS) * fg_ref[...]
        o_ref[0] = y


def _mlp(x, g, mod, w1_bf, w2_bf, final_g):
    bsz, length, _ = x.shape
    tm = min(length, 512)
    tf = MLP_TF
    per_batch = mod.shape[0] == bsz
    final_norm = final_g is not None
    fg = (final_g if final_norm else jnp.ones((D_MODEL,), F32)).reshape(1, D_MODEL)
    return pl.pallas_call(
        functools.partial(_mlp_kernel, final_norm=final_norm),
        out_shape=jax.ShapeDtypeStruct((bsz, length, D_MODEL), F32),
        grid=(bsz, length // tm, D_FF // tf),
        in_specs=[pl.BlockSpec((1, tm, D_MODEL), lambda b, i, f: (b, i, 0)),
                  pl.BlockSpec((1, D_MODEL), lambda b, i, f: (0, 0)),
                  pl.BlockSpec((1, 6, D_MODEL), (lambda b, i, f: (b, 0, 0)) if per_batch
                               else (lambda b, i, f: (0, 0, 0))),
                  pl.BlockSpec((1, D_MODEL, tf), lambda b, i, f: (f, 0, 0)),
                  pl.BlockSpec((tf, D_MODEL), lambda b, i, f: (f, 0)),
                  pl.BlockSpec((1, D_MODEL), lambda b, i, f: (0, 0))],
        out_specs=pl.BlockSpec((1, tm, D_MODEL), lambda b, i, f: (b, i, 0)),
        scratch_shapes=[pltpu.VMEM((tm, D_MODEL), BF16), pltpu.VMEM((tm, D_MODEL), F32)],
        compiler_params=_cparams(3),
        name="mlp",
    )(x, g.reshape(1, D_MODEL), mod, w1_bf, w2_bf, fg)


def _lin_scan(a, b, carry, rev):
    t_len, width = a.shape
    n_groups = t_len // 8
    a = a.reshape(n_groups, 8, width)
    b = b.reshape(n_groups, 8, width)
    sub = lax.broadcasted_iota(jnp.int32, a.shape, 1)
    for s in (1, 2, 4):
        if rev:
            a_sh = pltpu.roll(a, 8 - s, 1)
            b_sh = pltpu.roll(b, 8 - s, 1)
            ok = sub < 8 - s
        else:
            a_sh = pltpu.roll(a, s, 1)
            b_sh = pltpu.roll(b, s, 1)
            ok = sub >= s
        b = a * jnp.where(ok, b_sh, 0.0) + b
        a = a * jnp.where(ok, a_sh, 1.0)
    a = a.reshape(t_len, width)
    b = b.reshape(t_len, width)
    hs = [None] * n_groups
    for g in (range(n_groups - 1, -1, -1) if rev else range(n_groups)):
        h_g = a[8 * g:8 * g + 8, :] * carry + b[8 * g:8 * g + 8, :]
        hs[g] = h_g
        carry = h_g[0:1, :] if rev else h_g[7:8, :]
    return jnp.concatenate(hs, axis=0), carry


def _rglru_kernel(*refs, rev, row_len, final):
    if final:
        (x_ref, gate_ref, cw_ref, cb_ref, wa_ref, ba_ref, wx_ref, bx_ref, lam_ref, h0_ref, prev_ref,
         o_ref, st_ref, carry) = refs
    else:
        (x_ref, cw_ref, cb_ref, wa_ref, ba_ref, wx_ref, bx_ref, lam_ref, h0_ref,
         o_ref, st_ref, carry) = refs
    i = pl.program_id(1)

    @pl.when(i == 0)
    def _():
        carry[...] = h0_ref[0]

    xc = _short_conv(x_ref[0], cw_ref[...], row_len, 2) + cb_ref[...]
    xb = xc.astype(BF16)
    blocks = [slice(n * LANES, (n + 1) * LANES) for n in range(RG_W // LANES)]
    gate_r = jnp.concatenate([jnp.dot(xb[:, bs], wa_ref[n], preferred_element_type=F32)
                              for n, bs in enumerate(blocks)], axis=-1)
    gate_i = jnp.concatenate([jnp.dot(xb[:, bs], wx_ref[n], preferred_element_type=F32)
                              for n, bs in enumerate(blocks)], axis=-1)
    gate_r = _sigmoid(gate_r + ba_ref[...])
    gate_i = _sigmoid(gate_i + bx_ref[...])
    log_a = -RG_C * gate_r * _softplus(-lam_ref[...])
    a = jnp.exp(log_a)
    b = jnp.sqrt(-jnp.tanh(log_a) * (a * a + 1.0)) * (gate_i * xc)
    h, last = _lin_scan(a, b, carry[...], rev)
    carry[...] = last

    @pl.when(i == pl.num_programs(1) - 1)
    def _():
        st_ref[0] = last

    if final:
        gate = gate_ref[0]
        gelu = 0.5 * gate * (1.0 + jnp.tanh(math.sqrt(2.0 / math.pi) * (gate + 0.044715 * gate * gate * gate)))
        o_ref[0] = (gelu * (prev_ref[0] + h)).astype(o_ref.dtype)
    else:
        o_ref[0] = h


def _rglru_pass(p, prm, d, h0, prev, row_len):
    bsz, length, _ = p.shape
    conv_w, conv_b, wa, ba, wx, bx, lam = prm
    final = prev is not None
    rev = d == 1
    t = min(length, 256)
    nt = length // t
    tidx = (lambda i: nt - 1 - i) if rev else (lambda i: i)
    cx = COL_RG // RG_W
    col = lambda c0: pl.BlockSpec((1, t, RG_W), lambda b, i: (b, tidx(i), c0))
    vec = pl.BlockSpec((1, RG_W), lambda b, i: (0, 0))
    mat = pl.BlockSpec((RG_W // LANES, LANES, LANES), lambda b, i: (0, 0, 0))
    in_specs = [col(cx)]
    args = [p]
    if final:
        in_specs.append(col(cx + 1))
        args.append(p)
    in_specs += [pl.BlockSpec((4, RG_W), lambda b, i: (0, 0)), vec, mat, vec, mat, vec, vec,
                 pl.BlockSpec((1, 1, RG_W), lambda b, i: (b, 0, 0))]
    args += [conv_w, conv_b.reshape(1, RG_W), wa[d].astype(BF16), ba[d].reshape(1, RG_W),
             wx[d].astype(BF16), bx[d].reshape(1, RG_W), lam[d].reshape(1, RG_W), h0]
    if final:
        in_specs.append(pl.BlockSpec((1, t, RG_W), lambda b, i: (b, tidx(i), 0)))
        args.append(prev)
    out, st = pl.pallas_call(
        functools.partial(_rglru_kernel, rev=rev, row_len=row_len, final=final),
        out_shape=[jax.ShapeDtypeStruct((bsz, length, RG_W), BF16 if final else F32),
                   jax.ShapeDtypeStruct((bsz, 1, RG_W), F32)],
        grid=(bsz, nt),
        in_specs=in_specs,
        out_specs=[pl.BlockSpec((1, t, RG_W), lambda b, i: (b, tidx(i), 0)),
                   pl.BlockSpec((1, 1, RG_W), lambda b, i: (b, 0, 0))],
        scratch_shapes=[pltpu.VMEM((1, RG_W), F32)],
        compiler_params=_cparams(2),
        name="rglru_bwd" if rev else "rglru_fwd",
    )(*args)
    return out, st


def _rglru_mixer(p, prm, h0s, row_len):
    bsz = p.shape[0]
    if h0s is None:
        h0s = [jnp.zeros((bsz, 1, RG_W), F32)] * 2
    h_f, s_f = _rglru_pass(p, prm, 0, h0s[0], None, row_len)
    out, s_b = _rglru_pass(p, prm, 1, h0s[1], h_f, row_len)
    return out, [s_f, s_b]


def _tri_masks(n, rev):
    row = lax.broadcasted_iota(jnp.int32, (n, n), 0)
    col = lax.broadcasted_iota(jnp.int32, (n, n), 1)
    if rev:
        return row <= col, row < col
    return row >= col, row > col


def _dot_x3(a, b):
    a_hi = a.astype(BF16)
    a_lo = (a - a_hi.astype(F32)).astype(BF16)
    b_hi = b.astype(BF16)
    b_lo = (b - b_hi.astype(F32)).astype(BF16)
    mm = lambda p, q: jnp.dot(p, q, preferred_element_type=F32)
    return mm(a_hi, b_hi) + (mm(a_hi, b_lo) + mm(a_lo, b_hi))


GD_SOLVE_DOT = _dot_x3


def _unit_solve(neg_n, rhs, steps, dot):
    x = rhs
    m = neg_n
    for s in range(steps):
        x = x + dot(m, x)
        if s + 1 < steps:
            m = dot(m, m)
    return x


def _chunk_cumsum(x, rev):
    t_len = x.shape[0]
    pos = lax.broadcasted_iota(jnp.int32, x.shape, 0) & (CHUNK - 1)
    s = 1
    while s < CHUNK:
        if rev:
            x = x + jnp.where(pos < CHUNK - s, pltpu.roll(x, t_len - s, 0), 0.0)
        else:
            x = x + jnp.where(pos >= s, pltpu.roll(x, s, 0), 0.0)
        s *= 2
    return x


def _lane_form(x):
    sel = (lax.broadcasted_iota(jnp.int32, x.shape, 1) == 0).astype(BF16)
    p1 = x.astype(BF16)
    r1 = x - p1.astype(F32)
    p2 = r1.astype(BF16)
    p3 = (r1 - p2.astype(F32)).astype(BF16)
    nt = lambda p: lax.dot_general(sel, p, (((1,), (1,)), ((), ())), preferred_element_type=F32)
    return nt(p1) + nt(p2) + nt(p3)


def _dot_exact_rhs(a, b_exact):
    a_hi = a.astype(BF16)
    a_lo = (a - a_hi.astype(F32)).astype(BF16)
    b = b_exact.astype(BF16)
    return (jnp.dot(a_hi, b, preferred_element_type=F32) + jnp.dot(a_lo, b, preferred_element_type=F32))


def _gdn_kernel(*refs, rev, row_len, final, d):
    if final:
        (q_ref, k_ref, v_ref, z_ref, gb_ref, cq_ref, ck_ref, cv_ref, alog_ref, dtb_ref, ng_ref, s0_ref,
         prev_ref, o_ref, st_ref, state) = refs
    else:
        (q_ref, k_ref, v_ref, gb_ref, cq_ref, ck_ref, cv_ref, alog_ref, dtb_ref, ng_ref, s0_ref,
         o_ref, st_ref, state) = refs
    grp = pl.program_id(1)
    i = pl.program_id(2)
    n_heads = q_ref.shape[2] // GD_HEAD

    @pl.when(i == 0)
    def _():
        state[...] = s0_ref[0]

    qc = _silu(_short_conv(q_ref[0], cq_ref[...], row_len, 2))
    kc = _silu(_short_conv(k_ref[0], ck_ref[...], row_len, 2))
    vc = _silu(_short_conv(v_ref[0], cv_ref[...], row_len, 2))
    gbb = gb_ref[0]
    lane = lax.broadcasted_iota(jnp.int32, gbb.shape, 1)
    t_len = qc.shape[0]
    n_chunks = t_len // CHUNK
    incl, strict = _tri_masks(CHUNK, rev)
    order = list(range(n_chunks - 1, -1, -1) if rev else range(n_chunks))
    bf = lambda x: x.astype(BF16)

    heads = []
    for hh in range(n_heads):
        hs = slice(hh * GD_HEAD, (hh + 1) * GD_HEAD)
        head = grp * n_heads + hh
        q_h, k_h = qc[:, hs], kc[:, hs]
        q_h = q_h * lax.rsqrt(jnp.sum(q_h * q_h, axis=-1, keepdims=True) + 1e-6) * (GD_HEAD ** -0.5)
        k_h = k_h * lax.rsqrt(jnp.sum(k_h * k_h, axis=-1, keepdims=True) + 1e-6)
        g_raw = jnp.sum(jnp.where(lane == d * GD_HEADS + head, gbb, 0.0), axis=-1, keepdims=True)
        b_raw = jnp.sum(jnp.where(lane == (2 + d) * GD_HEADS + head, gbb, 0.0), axis=-1, keepdims=True)
        g = -jnp.exp(alog_ref[hh]) * _softplus(g_raw + dtb_ref[hh])
        heads.append(dict(q=q_h, k=k_h, v=vc[:, hs], beta=_sigmoid(b_raw), gc=_chunk_cumsum(g, rev)))

    pre = {}
    s_cur = [state[hh] for hh in range(n_heads)]
    os = {}

    def decay_tiles(c):
        for hh in range(n_heads):
            hd = heads[hh]
            sl = slice(c * CHUNK, (c + 1) * CHUNK)
            q_c, k_c, v_c, b_c, gc = hd["q"][sl], hd["k"][sl], hd["v"][sl], hd["beta"][sl], hd["gc"][sl]
            g_row = _lane_form(gc)
            kb = k_c * b_c
            e_gc = jnp.exp(gc)
            g_last = gc[0:1, :] if rev else gc[CHUNK - 1:CHUNK, :]
            pre[hh, c] = dict(decay_in=jnp.exp(jnp.where(incl, gc[:, :CHUNK] - g_row, -jnp.inf)),
                              q=q_c, k=k_c, kb=kb, x=jnp.concatenate([v_c * b_c, kb * e_gc], axis=-1),
                              qg=bf(q_c * e_gc), k_dec=bf(k_c * jnp.exp(g_last - gc)), dec=jnp.exp(g_last))

    def score_tiles(c):
        for hh in range(n_heads):
            p = pre[hh, c]
            p["m"] = -(_dot_nt(p["kb"], p["k"]) * jnp.where(strict, p["decay_in"], 0.0))
            p["a_qk"] = bf(_dot_nt(p["q"], p["k"]) * p["decay_in"])

    def solve_step(c, s):
        for hh in range(n_heads):
            p = pre[hh, c]
            p["x"] = p["x"] + GD_SOLVE_DOT(p["m"], p["x"])
            if s < 5:
                p["m"] = GD_SOLVE_DOT(p["m"], p["m"])

    def recur(c):
        for hh in range(n_heads):
            p = pre.pop((hh, c))
            x = p["x"]
            t2 = _dot(jnp.concatenate([bf(x[:, GD_HEAD:]), p["qg"]], axis=0), s_cur[hh])
            v_new = bf(x[:, :GD_HEAD] - t2[:CHUNK])
            os[hh, c] = t2[CHUNK:] + _dot(p["a_qk"], v_new)
            s_cur[hh] = s_cur[hh] * p["dec"] + _dot_tn(p["k_dec"], v_new)

    phases = [decay_tiles, score_tiles] + [functools.partial(solve_step, s=s) for s in range(6)] + [recur]
    for slot in range(n_chunks + len(phases) - 1):
        for ph in range(len(phases) - 1, -1, -1):
            if 0 <= slot - ph < n_chunks:
                phases[ph](order[slot - ph])

    for hh in range(n_heads):
        hs = slice(hh * GD_HEAD, (hh + 1) * GD_HEAD)
        o_h = jnp.concatenate([os[hh, c] for c in range(n_chunks)], axis=0)
        if final:
            o_t = prev_ref[0, :, hs] + o_h
            o_t = o_t * lax.rsqrt(jnp.mean(o_t * o_t, axis=-1, keepdims=True) + NORM_EPS) * ng_ref[...]
            o_ref[0, :, hs] = (o_t * _silu(z_ref[0, :, hs])).astype(o_ref.dtype)
        else:
            o_ref[0, :, hs] = o_h
        state[hh] = s_cur[hh]

    @pl.when(i == pl.num_programs(2) - 1)
    def _():
        for hh in range(n_heads):
            st_ref[0, hh] = s_cur[hh]


GD_HEADS_PER_STEP = 2
GD_TILE = 512


def _gdn_pass(p, prm, d, s0, prev, row_len):
    bsz, length, _ = p.shape
    conv_w, a_log, dt_bias, norm_g = prm
    final = prev is not None
    rev = d == 1
    t = min(length, GD_TILE)
    nt = length // t
    tidx = (lambda i: nt - 1 - i) if rev else (lambda i: i)
    nh = GD_HEADS_PER_STEP
    wd = nh * GD_HEAD
    ng = GD_W // wd
    c0 = COL_GD // wd
    col = lambda off: pl.BlockSpec((1, t, wd), lambda b, h, i: (b, tidx(i), c0 + off * ng + h))
    cw = lambda off: pl.BlockSpec((4, wd), lambda b, h, i: (0, off * ng + h))
    hvec = pl.BlockSpec((nh, 1, LANES), lambda b, h, i: (h, 0, 0))
    in_specs = [col(0), col(1), col(2)]
    args = [p, p, p]
    if final:
        in_specs.append(col(3))
        args.append(p)
    in_specs += [pl.BlockSpec((1, t, LANES), lambda b, h, i: (b, tidx(i), COL_GD_GB // LANES)),
                 cw(0), cw(1), cw(2), hvec, hvec,
                 pl.BlockSpec((1, LANES), lambda b, h, i: (0, 0)),
                 pl.BlockSpec((1, nh, GD_HEAD, GD_HEAD), lambda b, h, i: (b, h, 0, 0))]
    bcast = lambda v: jnp.broadcast_to(v.reshape(GD_HEADS, 1, 1), (GD_HEADS, 1, LANES))
    args += [p, conv_w, conv_w, conv_w, bcast(a_log[d]), bcast(dt_bias[d]), norm_g.reshape(1, GD_HEAD), s0]
    if final:
        in_specs.append(pl.BlockSpec((1, t, wd), lambda b, h, i: (b, tidx(i), h)))
        args.append(prev)
    out, st = pl.pallas_call(
        functools.partial(_gdn_kernel, rev=rev, row_len=row_len, final=final, d=d),
        out_shape=[jax.ShapeDtypeStruct((bsz, length, GD_W), BF16 if final else F32),
                   jax.ShapeDtypeStruct((bsz, GD_HEADS, GD_HEAD, GD_HEAD), F32)],
        grid=(bsz, ng, nt),
        in_specs=in_specs,
        out_specs=[pl.BlockSpec((1, t, wd), lambda b, h, i: (b, tidx(i), h)),
                   pl.BlockSpec((1, nh, GD_HEAD, GD_HEAD), lambda b, h, i: (b, h, 0, 0))],
        scratch_shapes=[pltpu.VMEM((nh, GD_HEAD, GD_HEAD), F32)],
        compiler_params=_cparams(3),
        name="gdn_bwd" if rev else "gdn_fwd",
    )(*args)
    return out, st


def _gdn_mixer(p, prm, s0s, row_len):
    bsz = p.shape[0]
    if s0s is None:
        s0s = [jnp.zeros((bsz, GD_HEADS, GD_HEAD, GD_HEAD), F32)] * 2
    o_f, s_f = _gdn_pass(p, prm, 0, s0s[0], None, row_len)
    out, s_b = _gdn_pass(p, prm, 1, s0s[1], o_f, row_len)
    return out, [s_f, s_b]


def _stack_heads(x, lo):
    return jnp.concatenate([jnp.where(lo, x, 0.0), jnp.where(lo, 0.0, x)], axis=0)


def _rwkv_kernel(*refs, rev, final):
    if final:
        (r_ref, k_ref, v_ref, lo_ref, mur_ref, muk_ref, muv_ref, mul_ref, w0_ref, wup_ref, a0_ref, aup_ref,
         gup_ref, kk_ref, ka_ref, rk_ref, lng_ref, lnb_ref, s0_ref, prev_ref,
         o_ref, st_ref, state, c_r, c_k, c_v, c_l) = refs
    else:
        (r_ref, k_ref, v_ref, lo_ref, mur_ref, muk_ref, muv_ref, mul_ref, w0_ref, wup_ref, a0_ref, aup_ref,
         gup_ref, kk_ref, ka_ref, rk_ref, lng_ref, lnb_ref, s0_ref,
         o_ref, st_ref, state, c_r, c_k, c_v, c_l) = refs
    i = pl.program_id(2)

    @pl.when(i == 0)
    def _():
        state[...] = s0_ref[0]
        c_r[...] = jnp.zeros_like(c_r)
        c_k[...] = jnp.zeros_like(c_k)
        c_v[...] = jnp.zeros_like(c_v)
        c_l[...] = jnp.zeros_like(c_l)

    t_len = r_ref.shape[1]

    def shifted(x_ref, carry, mu_ref):
        x = x_ref[0]
        row = lax.broadcasted_iota(jnp.int32, x.shape, 0)
        if rev:
            prev = jnp.where(row == t_len - 1, carry[...], pltpu.roll(x, t_len - 1, 0))
            carry[...] = x[0:1, :]
        else:
            prev = jnp.where(row == 0, carry[...], pltpu.roll(x, 1, 0))
            carry[...] = x[t_len - 1:t_len, :]
        return x + (prev - x) * mu_ref[...]

    r = shifted(r_ref, c_r, mur_ref)
    k = shifted(k_ref, c_k, muk_ref)
    v = shifted(v_ref, c_v, muv_ref)
    lora = shifted(lo_ref, c_l, mul_ref)
    xw, xa, xg = lora[:, 0:64], lora[:, 64:128], lora[:, 128:256]

    width = r_ref.shape[2]
    n_pairs = width // LANES
    lane = lax.broadcasted_iota(jnp.int32, (1, LANES), 1)
    lo = lane < RW_HEAD
    head_sum = (lax.broadcasted_iota(jnp.int32, (width, width), 0) // RW_HEAD
                == lax.broadcasted_iota(jnp.int32, (width, width), 1) // RW_HEAD).astype(F32)
    rowh = lax.broadcasted_iota(jnp.int32, (LANES, LANES), 0) // RW_HEAD
    colh = lax.broadcasted_iota(jnp.int32, (LANES, LANES), 1) // RW_HEAD

    lw = -math.exp(-0.5) * _sigmoid(w0_ref[...] + _dot(jnp.tanh(xw), wup_ref[...]))
    a = _sigmoid(a0_ref[...] + _dot(xa, aup_ref[...]))
    kk = k * kk_ref[...]
    kappa = kk / jnp.maximum(jnp.sqrt(_dot_exact_rhs(kk * kk, head_sum)), 1e-12)
    kt = k * (1.0 + (a - 1.0) * ka_ref[...])
    gate = _dot(_sigmoid(xg), gup_ref[...])
    bonus = _dot_exact_rhs(r * kt * rk_ref[...], head_sum) * v

    n_chunks = t_len // CHUNK
    lg_all = _chunk_cumsum(lw, rev)
    rowt = lax.broadcasted_iota(jnp.int32, (LANES, LANES), 0) & (CHUNK - 1)
    colt = lax.broadcasted_iota(jnp.int32, (LANES, LANES), 1) & (CHUNK - 1)
    same = rowh == colh
    incl2 = ((rowt <= colt) if rev else (rowt >= colt)) & same
    strict2 = ((rowt < colt) if rev else (rowt > colt)) & same
    order = list(range(n_chunks - 1, -1, -1) if rev else range(n_chunks))
    bf = lambda x: x.astype(BF16)

    pre = {}
    s_cur = [state[pp] for pp in range(n_pairs)]
    ys = {}

    def scores(c):
        for pp in range(n_pairs):
            sl = (slice(c * CHUNK, (c + 1) * CHUNK), slice(pp * LANES, (pp + 1) * LANES))
            lw_c, kap_c, a_c = lw[sl], kappa[sl], a[sl]
            lg = lg_all[sl]
            lg_tot = lg[0:1, :] if rev else lg[CHUNK - 1:CHUNK, :]
            e_neg = jnp.exp(-lg)
            e_rem = jnp.exp(lg_tot - lg)
            p_raw = -(kap_c * a_c)
            q2 = bf(_stack_heads(kap_c * jnp.exp(lg - lw_c), lo))
            p2 = bf(_stack_heads(p_raw * e_neg, lo))
            k2 = bf(_stack_heads(kt[sl] * e_neg, lo))
            r2 = bf(_stack_heads(r[sl] * jnp.exp(lg), lo))
            pre[pp, c] = dict(
                q2=q2, r2=r2, v2=bf(_stack_heads(v[sl], lo)),
                pt2=bf(_stack_heads(p_raw * e_rem, lo)), kt2=bf(_stack_heads(kt[sl] * e_rem, lo)),
                m=jnp.where(strict2, _dot_nt(q2, p2), 0.0),
                a_qk=jnp.where(strict2, _dot_nt(q2, k2), 0.0),
                a_rp=bf(jnp.where(incl2, _dot_nt(r2, p2), 0.0)),
                a_rk=jnp.where(incl2, _dot_nt(r2, k2), 0.0),
                dec=jnp.broadcast_to(jnp.exp(lg_tot), (LANES, LANES)).T)

    def local_terms(c):
        for pp in range(n_pairs):
            p = pre[pp, c]
            p["x"] = jnp.concatenate([p["q2"].astype(F32), _dot(p["a_qk"], p["v2"])], axis=-1)
            p["y_loc"] = _dot(p["a_rk"], p["v2"])
            p["s_loc"] = _dot_tn(p["kt2"], p["v2"])

    def solve_step(c, s):
        for pp in range(n_pairs):
            p = pre[pp, c]
            p["x"] = p["x"] + _dot(p["m"], p["x"])
            if s < 5:
                p["m"] = _dot(p["m"], p["m"])

    def recur(c):
        for pp in range(n_pairs):
            p = pre.pop((pp, c))
            x = p["x"]
            t2 = _dot(jnp.concatenate([bf(x[:, :LANES]), p["r2"]], axis=0), s_cur[pp])
            u2 = bf(t2[:LANES] + x[:, LANES:])
            y2 = t2[LANES:] + _dot(p["a_rp"], u2) + p["y_loc"]
            s_cur[pp] = s_cur[pp] * p["dec"] + _dot_tn(p["pt2"], u2) + p["s_loc"]
            ys[pp, c] = y2[:CHUNK] + y2[CHUNK:]

    phases = [scores, local_terms] + [functools.partial(solve_step, s=s) for s in range(6)] + [recur]
    for slot in range(n_chunks + len(phases) - 1):
        for ph in range(len(phases) - 1, -1, -1):
            if 0 <= slot - ph < n_chunks:
                phases[ph](order[slot - ph])

    y = jnp.concatenate([jnp.concatenate([ys[pp, c] for c in range(n_chunks)], axis=0)
                         for pp in range(n_pairs)], axis=1)
    mean = _dot_exact_rhs(y, head_sum) * (1.0 / RW_HEAD)
    yc = y - mean
    var = _dot_exact_rhs(yc * yc, head_sum) * (1.0 / RW_HEAD)
    yn = yc * lax.rsqrt(var + RW_GN_EPS) * lng_ref[...] + lnb_ref[...]
    out = (yn + bonus) * gate
    if final:
        o_ref[0] = (prev_ref[0] + out).astype(o_ref.dtype)
    else:
        o_ref[0] = out
    for pp in range(n_pairs):
        state[pp] = s_cur[pp]

    @pl.when(i == pl.num_programs(2) - 1)
    def _():
        for pp in range(n_pairs):
            st_ref[0, pp] = s_cur[pp]


RW_PAIRS_PER_STEP = 2
RW_TILE = 512


def _rwkv_pass(p, prm, d, s0, prev):
    bsz, length, _ = p.shape
    mu, w0, w_up, a0, a_up, g_up, k_k, k_a, r_k, ln_g, ln_b = [t[d] for t in prm]
    final = prev is not None
    rev = d == 1
    t = min(length, RW_TILE)
    nt = length // t
    tidx = (lambda i: nt - 1 - i) if rev else (lambda i: i)
    npp = RW_PAIRS_PER_STEP
    wd = npp * LANES
    c0 = COL_RW // wd
    ng = RW_W // wd
    col = lambda off: pl.BlockSpec((1, t, wd), lambda b, n, i: (b, tidx(i), c0 + off * ng + n))
    vec = pl.BlockSpec((1, wd), lambda b, n, i: (0, n))
    whole = lambda shape: pl.BlockSpec(shape, lambda b, n, i: (0,) * len(shape))
    row = lambda x: x.reshape(1, -1)
    in_specs = [col(0), col(1), col(2),
                pl.BlockSpec((1, t, 256), lambda b, n, i: (b, tidx(i), COL_RW_LORA // 256)),
                vec, vec, vec, whole((1, 256)),
                vec, pl.BlockSpec((64, wd), lambda b, n, i: (0, n)),
                vec, pl.BlockSpec((64, wd), lambda b, n, i: (0, n)),
                pl.BlockSpec((LANES, wd), lambda b, n, i: (0, n)),
                vec, vec, vec, vec, vec,
                pl.BlockSpec((1, npp, LANES, LANES), lambda b, n, i: (b, n, 0, 0))]
    args = [p, p, p, p,
            row(mu[0:512]), row(mu[512:1024]), row(mu[1024:1536]), row(mu[1536:1792]),
            row(w0), w_up.astype(BF16), row(a0), a_up.astype(BF16), g_up.astype(BF16),
            row(k_k), row(k_a), row(r_k), row(ln_g), row(ln_b), s0]
    if final:
        in_specs.append(pl.BlockSpec((1, t, wd), lambda b, n, i: (b, tidx(i), n)))
        args.append(prev)
    out, st = pl.pallas_call(
        functools.partial(_rwkv_kernel, rev=rev, final=final),
        out_shape=[jax.ShapeDtypeStruct((bsz, length, RW_W), BF16 if final else F32),
                   jax.ShapeDtypeStruct((bsz, 4, LANES, LANES), F32)],
        grid=(bsz, ng, nt),
        in_specs=in_specs,
        out_specs=[pl.BlockSpec((1, t, wd), lambda b, n, i: (b, tidx(i), n)),
                   pl.BlockSpec((1, npp, LANES, LANES), lambda b, n, i: (b, n, 0, 0))],
        scratch_shapes=[pltpu.VMEM((npp, LANES, LANES), F32), pltpu.VMEM((1, wd), F32),
                        pltpu.VMEM((1, wd), F32), pltpu.VMEM((1, wd), F32),
                        pltpu.VMEM((1, 256), F32)],
        compiler_params=_cparams(3),
        name="rwkv_bwd" if rev else "rwkv_fwd",
    )(*args)
    return out, st


def _rwkv_mixer(p, prm, s0s):
    bsz = p.shape[0]
    if s0s is None:
        s0s = [jnp.zeros((bsz, 4, LANES, LANES), F32)] * 2
    o_f, s_f = _rwkv_pass(p, prm, 0, s0s[0], None)
    out, s_b = _rwkv_pass(p, prm, 1, s0s[1], o_f)
    return out, [s_f, s_b]


def _dft_table_kernel(ca_ref, sa_ref, cb_ref, sb_ref, c_ref, s_ref):
    cb, sb = cb_ref[...], sb_ref[...]
    for j in range(ca_ref.shape[1]):
        ca, sa = ca_ref[:, j:j + 1], sa_ref[:, j:j + 1]
        c_ref[:, j * LANES:(j + 1) * LANES] = (ca * cb - sa * sb).astype(c_ref.dtype)
        s_ref[:, j * LANES:(j + 1) * LANES] = (-(sa * cb + ca * sb)).astype(s_ref.dtype)


def _dft_tables(length):
    n = 2 * length
    nfp = -(-(length + 1) // LANES) * LANES
    kf = jnp.arange(nfp, dtype=jnp.int32)[:, None]
    s1 = jnp.arange(n // LANES, dtype=jnp.int32)[None, :]
    s0 = jnp.arange(LANES, dtype=jnp.int32)[None, :]
    ang_a = (2.0 * math.pi / n) * ((kf * s1 * LANES) % n).astype(F32)
    ang_b = (2.0 * math.pi / n) * ((kf * s0) % n).astype(F32)
    ok = (kf <= length).astype(F32)
    tm = 384 if nfp % 384 == 0 else LANES
    rows = lambda w: pl.BlockSpec((tm, w), lambda i: (i, 0))
    tab_c, tab_s = pl.pallas_call(
        _dft_table_kernel,
        out_shape=[jax.ShapeDtypeStruct((nfp, n), BF16)] * 2,
        grid=(nfp // tm,),
        in_specs=[rows(n // LANES), rows(n // LANES), rows(LANES), rows(LANES)],
        out_specs=[rows(n), rows(n)],
        compiler_params=_cparams(1),
        name="hyena_dft_tables",
    )(jnp.cos(ang_a) * ok, jnp.sin(ang_a) * ok, jnp.cos(ang_b), jnp.sin(ang_b))
    kk = kf[:, 0]
    wk = jnp.where((kk == 0) | (kk == length), 1.0, 2.0) * (kk <= length) / n
    return tab_c, tab_s, jnp.broadcast_to(wk[:, None], (nfp, LANES)).astype(F32)


def _hyena_filter(length, w1, b1, w2, b2, w3, freq):
    t = jnp.arange(length, dtype=F32)
    z = t / max(length - 1, 1)
    bands = jnp.linspace(1e-4, HY_BANDS - 1, HY_BANDS, dtype=F32)
    ang = (2.0 * math.pi / length) * t[:, None] * bands[None, :]
    feat = jnp.concatenate([z[:, None], jnp.cos(ang), -jnp.sin(ang)], axis=-1)
    h = jnp.sin(freq[0] * (feat @ w1 + b1))
    h = jnp.sin(freq[1] * (h @ w2 + b2))
    h = (h @ w3).astype(F32)
    deltas = jnp.abs(jnp.linspace(math.log(1e-2) / 1.5, math.log(1e-2) / 0.3, HY_W, dtype=F32))
    h = h * jnp.exp(-z[:, None] * jnp.tile(deltas, 2)[None, :])
    h_fwd = h[:, :HY_W]
    h_bwd = jnp.where(t[:, None] > 0, h[:, HY_W:], 0.0)
    norm = jnp.sum(jnp.abs(h_fwd) + jnp.abs(h_bwd), axis=0, keepdims=True)
    return (h_fwd + h_bwd) / norm, (h_fwd - h_bwd) / norm


def _hy_zin_kernel(v_ref, x1_ref, cw_ref, o_ref, *, row_len):
    vc = _short_conv(v_ref[0], cw_ref[:, 0:512], row_len, 1)
    x1c = _short_conv(x1_ref[0], cw_ref[:, 1024:1536], row_len, 1)
    o_ref[0] = (x1c * vc).astype(o_ref.dtype)


def _hy_zin(p, conv_w, row_len):
    bsz, length, _ = p.shape
    t = min(length, 256)
    return pl.pallas_call(
        functools.partial(_hy_zin_kernel, row_len=row_len),
        out_shape=jax.ShapeDtypeStruct((bsz, length, HY_W), BF16),
        grid=(bsz, length // t),
        in_specs=[pl.BlockSpec((1, t, 512), lambda b, i: (b, i, 0)),
                  pl.BlockSpec((1, t, 512), lambda b, i: (b, i, 2)),
                  pl.BlockSpec((3, 1536), lambda b, i: (0, 0))],
        out_specs=pl.BlockSpec((1, t, 512), lambda b, i: (b, i, 0)),
        compiler_params=_cparams(2),
        name="hyena_zin",
    )(p, p, conv_w)


def _dft_fwd_kernel(*refs, mult):
    if mult:
        c_ref, s_ref, z_ref, fr_ref, fi_ref, yr_ref, yi_ref = refs
        z_cos = z_sin = z_ref[0].astype(BF16)
    else:
        c_ref, s_ref, zc_ref, zs_ref, wk_ref, yr_ref, yi_ref = refs
        z_cos, z_sin = zc_ref[0].astype(BF16), zs_ref[0].astype(BF16)
    zr = jnp.dot(c_ref[...], z_cos, preferred_element_type=F32)
    zi = jnp.dot(s_ref[...], z_sin, preferred_element_type=F32)
    if mult:
        fr, fi = fr_ref[...], fi_ref[...]
        zr, zi = zr * fr - zi * fi, zr * fi + zi * fr
    else:
        zr, zi = zr * wk_ref[:, 0:1], zi * wk_ref[:, 0:1]
    yr_ref[0] = zr.astype(yr_ref.dtype)
    yi_ref[0] = zi.astype(yi_ref.dtype)


def _dft_fwd(tab_c, tab_s, z, spec, wk=None):
    mult = spec is not None
    zs = (z,) if mult else z
    bsz, klen, _ = zs[0].shape
    nfp = tab_c.shape[0]
    tm = 384 if nfp % 384 == 0 else LANES
    in_specs = [pl.BlockSpec((tm, klen), lambda b, i: (i, 0)),
                pl.BlockSpec((tm, klen), lambda b, i: (i, 0))]
    in_specs += [pl.BlockSpec((1, klen, 512), lambda b, i: (b, 0, 0))] * len(zs)
    args = [tab_c, tab_s, *zs]
    if mult:
        in_specs += [pl.BlockSpec((tm, 512), lambda b, i: (i, 0))] * 2
        args += list(spec)
    else:
        in_specs.append(pl.BlockSpec((tm, LANES), lambda b, i: (i, 0)))
        args.append(wk)
    odt = BF16 if mult else F32
    return pl.pallas_call(
        functools.partial(_dft_fwd_kernel, mult=mult),
        out_shape=[jax.ShapeDtypeStruct((bsz, nfp, 512), odt)] * 2,
        grid=(bsz, nfp // tm),
        in_specs=in_specs,
        out_specs=[pl.BlockSpec((1, tm, 512), lambda b, i: (b, i, 0))] * 2,
        compiler_params=_cparams(2),
        name="hyena_dft_mul" if mult else "hyena_dft_filter",
    )(*args)


def _dft_inv_kernel(ci_ref, si_ref, yr_ref, yi_ref, v_ref, x0_ref, x1_ref, cw_ref, skip_ref, o_ref, *, row_len):
    y = jnp.dot(ci_ref[...], yr_ref[0], preferred_element_type=F32)
    y += jnp.dot(si_ref[...], yi_ref[0], preferred_element_type=F32)
    vc = _short_conv(v_ref[0], cw_ref[:, 0:512], row_len, 1)
    x0c = _short_conv(x0_ref[0], cw_ref[:, 512:1024], row_len, 1)
    x1c = _short_conv(x1_ref[0], cw_ref[:, 1024:1536], row_len, 1)
    zin = x1c * vc
    o_ref[0] = (x0c * (y + zin * skip_ref[...])).astype(o_ref.dtype)


def _dft_inv(tab_c, tab_s, yr, yi, p, conv_w, skip, row_len):
    bsz, length, _ = p.shape
    nfp = tab_c.shape[0]
    t = min(length, 256)
    pcol = lambda c: pl.BlockSpec((1, t, 512), lambda b, i: (b, i, c))
    return pl.pallas_call(
        functools.partial(_dft_inv_kernel, row_len=row_len),
        out_shape=jax.ShapeDtypeStruct((bsz, length, HY_W), BF16),
        grid=(bsz, length // t),
        in_specs=[pl.BlockSpec((t, nfp), lambda b, i: (i, 0)),
                  pl.BlockSpec((t, nfp), lambda b, i: (i, 0)),
                  pl.BlockSpec((1, nfp, 512), lambda b, i: (b, 0, 0)),
                  pl.BlockSpec((1, nfp, 512), lambda b, i: (b, 0, 0)),
                  pcol(0), pcol(1), pcol(2),
                  pl.BlockSpec((3, 1536), lambda b, i: (0, 0)),
                  pl.BlockSpec((1, 512), lambda b, i: (0, 0))],
        out_specs=pl.BlockSpec((1, t, 512), lambda b, i: (b, i, 0)),
        compiler_params=_cparams(2),
        name="hyena_idft_gate",
    )(tab_c, tab_s, yr, yi, p, p, p, conv_w, skip.reshape(1, HY_W))


def _hyena_mixer(p, prm, tables, row_len):
    conv_w, w1, b1, w2, b2, w3, freq, skip = prm
    length = p.shape[1]
    tab_c, tab_s, wk = tables
    f_cos, f_sin = _hyena_filter(length, w1, b1, w2, b2, w3, freq)
    spec = _dft_fwd(tab_c, tab_s, (f_cos[None], f_sin[None]), None, wk)
    spec = (spec[0][0], spec[1][0])
    zin = _hy_zin(p, conv_w, row_len)
    yr, yi = _dft_fwd(tab_c, tab_s, zin, spec)
    return _dft_inv(tab_c, tab_s, yr, yi, p, conv_w, skip, row_len)


def _permute_w_in(w_in):
    hy = w_in[:, 0:1536]
    rw = w_in[:, 1536:3328]
    gd = w_in[:, 3328:5392]
    rg = w_in[:, 5392:6416]
    pad = jnp.zeros((w_in.shape[0], N_PROJ - 6416), w_in.dtype)
    w = jnp.concatenate([hy, rw[:, :1536], gd[:, :2048], rg, rw[:, 1536:], gd[:, 2048:], pad], axis=-1)
    return _tile_major(w.astype(BF16), INPROJ_TN)


def kernel(x, c, ctx, c_ctx, ada_w, ada_b, norm_mix_g, norm_mlp_g, w_in, w_out,
           hy_conv, hy_w1, hy_b1, hy_w2, hy_b2, hy_w3, hy_freq, hy_skip,
           rw_mu, rw_w0, rw_w_up, rw_a0, rw_a_up, rw_g_up, rw_k_k, rw_k_a, rw_r_k, rw_ln_g, rw_ln_b,
           gd_conv, gd_a_log, gd_dt_bias, gd_norm_g,
           rg_conv, rg_conv_b, rg_wa, rg_ba, rg_wx, rg_bx, rg_lambda,
           mlp_w1, mlp_w2, final_norm_g):
    bsz, seq, _ = x.shape
    ctx_len = ctx.shape[1]
    depth = ada_w.shape[0]
    tables_x = _dft_tables(seq)
    tables_c = _dft_tables(ctx_len)
    cond8 = jnp.concatenate([c, c_ctx[None, :], jnp.zeros((8 - bsz - 1, D_MODEL), F32)], axis=0)
    for l in range(depth):
        last = l == depth - 1
        mod = _modulation(cond8, ada_w, ada_b[l], l)
        mod_x = mod[:bsz].reshape(bsz, 6, D_MODEL)
        mod_c = mod[bsz:bsz + 1].reshape(1, 6, D_MODEL)
        w_in_bf = _permute_w_in(w_in[l])
        w_out_bf = w_out[l].astype(BF16)
        w1_bf = _tile_major(mlp_w1[l].astype(BF16), MLP_TF)
        w2_bf = mlp_w2[l].astype(BF16)
        px = _inproj(x, norm_mix_g[l], mod_x, w_in_bf)
        pc = _inproj(ctx, norm_mix_g[l], mod_c, w_in_bf)
        hy_prm = (hy_conv[l], hy_w1[l], hy_b1[l], hy_w2[l], hy_b2[l], hy_w3[l], hy_freq[l], hy_skip[l])
        rw_prm = (rw_mu[l], rw_w0[l], rw_w_up[l], rw_a0[l], rw_a_up[l], rw_g_up[l],
                  rw_k_k[l], rw_k_a[l], rw_r_k[l], rw_ln_g[l], rw_ln_b[l])
        gd_prm = (gd_conv[l], gd_a_log[l], gd_dt_bias[l], gd_norm_g[l])
        rg_prm = (rg_conv[l], rg_conv_b[l], rg_wa[l], rg_ba[l], rg_wx[l], rg_bx[l], rg_lambda[l])
        c_rw, s_rw = _rwkv_mixer(pc, rw_prm, None)
        c_gd, s_gd = _gdn_mixer(pc, gd_prm, None, ctx_len)
        c_rg, s_rg = _rglru_mixer(pc, rg_prm, None, ctx_len)
        x_hy = _hyena_mixer(px, hy_prm, tables_x, GRID_W)
        x_rw, _ = _rwkv_mixer(px, rw_prm, s_rw)
        x_gd, _ = _gdn_mixer(px, gd_prm, s_gd, GRID_W)
        x_rg, _ = _rglru_mixer(px, rg_prm, s_rg, GRID_W)
        x_new = _outproj(x, mod_x, (x_hy, x_rw, x_gd, x_rg), w_out_bf)
        x_new = _mlp(x_new, norm_mlp_g[l], mod_x, w1_bf, w2_bf, final_norm_g if last else None)
        if not last:
            c_hy = _hyena_mixer(pc, hy_prm, tables_c, ctx_len)
            ctx_new = _outproj(ctx, mod_c, (c_hy, c_rw, c_gd, c_rg), w_out_bf)
            ctx = _mlp(ctx_new, norm_mlp_g[l], mod_c, w1_bf, w2_bf, None)
        x = x_new
    return x
```

```python
import functools
import math

import jax
import jax.numpy as jnp
from jax import lax
from jax.experimental import pallas as pl
from jax.experimental.pallas import tpu as pltpu

F32 = jnp.float32
BF16 = jnp.bfloat16
HIGHEST = lax.Precision.HIGHEST

D_MODEL = 2048
GRID_W = 64
HY_W = RW_W = GD_W = RG_W = 512
D_FF = 4 * D_MODEL
NORM_EPS = 1e-6
HY_EMB = 33
HY_BANDS = 16
RW_HEAD = 64
RW_GN_EPS = 64e-5
GD_HEAD = 128
GD_HEADS = 4
CHUNK = 64
RG_C = 8.0
LANES = 128

COL_HY = 0
COL_RW = 1536
COL_GD = 3072
COL_RG = 5120
COL_RW_LORA = 6144
COL_GD_GB = 6400
N_PROJ = 6656

VMEM_LIMIT = 56 * 1024 * 1024


def _cparams(n_axes):
    return pltpu.CompilerParams(dimension_semantics=("arbitrary",) * n_axes,
                                vmem_limit_bytes=VMEM_LIMIT)


def _dot(a, b):
    return jnp.dot(a.astype(BF16), b.astype(BF16), preferred_element_type=F32)


def _dot_nt(a, b):
    return lax.dot_general(a.astype(BF16), b.astype(BF16), (((1,), (1,)), ((), ())),
                           preferred_element_type=F32)


def _dot_tn(a, b):
    return lax.dot_general(a.astype(BF16), b.astype(BF16), (((0,), (0,)), ((), ())),
                           preferred_element_type=F32)


def _dot_hp(a, b):
    return jnp.dot(a, b, precision=HIGHEST, preferred_element_type=F32)


def _dot_nt_hp(a, b):
    return lax.dot_general(a, b, (((1,), (1,)), ((), ())), precision=HIGHEST,
                           preferred_element_type=F32)


def _sigmoid(x):
    return 1.0 / (1.0 + jnp.exp(-x))


def _silu(x):
    return x * _sigmoid(x)


def _softplus(x):
    return jnp.maximum(x, 0.0) + jnp.log1p(jnp.exp(-jnp.abs(x)))


def _short_conv(u, w, row_len, pad_left):
    t_len = u.shape[0]
    pos = lax.broadcasted_iota(jnp.int32, u.shape, 0) & (row_len - 1)
    y = None
    for j in range(w.shape[0]):
        off = j - pad_left
        if off == 0:
            term = u * w[j:j + 1, :]
        else:
            sh = pltpu.roll(u, (-off) % t_len, 0)
            ok = (pos + off >= 0) & (pos + off < row_len)
            term = jnp.where(ok, sh, 0.0) * w[j:j + 1, :]
        y = term if y is None else y + term
    return y


def _mod_kernel(c_ref, w_ref, b_ref, o_ref):
    o_ref[...] = _dot(_silu(c_ref[...]), w_ref[0]) + b_ref[...]


def _modulation(cond8, ada_w, ada_b, layer):
    n = ada_w.shape[2]
    tn = 512
    return pl.pallas_call(
        _mod_kernel,
        out_shape=jax.ShapeDtypeStruct((8, n), F32),
        grid=(n // tn,),
        in_specs=[pl.BlockSpec((8, D_MODEL), lambda j: (0, 0)),
                  pl.BlockSpec((1, D_MODEL, tn), lambda j: (layer, 0, j)),
                  pl.BlockSpec((1, tn), lambda j: (0, j))],
        out_specs=pl.BlockSpec((8, tn), lambda j: (0, j)),
        compiler_params=_cparams(1),
        name="adaln_mod",
    )(cond8, ada_w, ada_b.reshape(1, n))


def _norm_mod(x, g, shift, scale):
    y = x * lax.rsqrt(jnp.mean(x * x, axis=-1, keepdims=True) + NORM_EPS) * g
    return y * (1.0 + scale) + shift


def _inproj_kernel(x_ref, g_ref, mod_ref, w_ref, o_ref, h_scr):
    @pl.when(pl.program_id(2) == 0)
    def _():
        h = _norm_mod(x_ref[0], g_ref[...], mod_ref[0, 0:1, :], mod_ref[0, 1:2, :])
        h_scr[...] = h.astype(BF16)

    o_ref[0] = jnp.dot(h_scr[...], w_ref[...], preferred_element_type=F32)


def _inproj(x, g, mod, w_bf):
    bsz, length, _ = x.shape
    tm = min(length, 1024)
    tn = 512
    per_batch = mod.shape[0] == bsz
    return pl.pallas_call(
        _inproj_kernel,
        out_shape=jax.ShapeDtypeStruct((bsz, length, N_PROJ), F32),
        grid=(bsz, length // tm, N_PROJ // tn),
        in_specs=[pl.BlockSpec((1, tm, D_MODEL), lambda b, i, j: (b, i, 0)),
                  pl.BlockSpec((1, D_MODEL), lambda b, i, j: (0, 0)),
                  pl.BlockSpec((1, 6, D_MODEL), (lambda b, i, j: (b, 0, 0)) if per_batch
                               else (lambda b, i, j: (0, 0, 0))),
                  pl.BlockSpec((D_MODEL, tn), lambda b, i, j: (0, j))],
        out_specs=pl.BlockSpec((1, tm, tn), lambda b, i, j: (b, i, j)),
        scratch_shapes=[pltpu.VMEM((tm, D_MODEL), BF16)],
        compiler_params=_cparams(3),
        name="inproj",
    )(x, g.reshape(1, D_MODEL), mod, w_bf)


def _outproj_kernel(x_ref, mod_ref, m0_ref, m1_ref, m2_ref, m3_ref, w_ref, o_ref):
    acc = jnp.dot(m0_ref[0], w_ref[0:512, :], preferred_element_type=F32)
    acc += jnp.dot(m1_ref[0], w_ref[512:1024, :], preferred_element_type=F32)
    acc += jnp.dot(m2_ref[0], w_ref[1024:1536, :], preferred_element_type=F32)
    acc += jnp.dot(m3_ref[0], w_ref[1536:2048, :], preferred_element_type=F32)
    o_ref[0] = x_ref[0] + mod_ref[0, 2:3, :] * acc


def _outproj(x, mod, mixers, w_bf):
    bsz, length, _ = x.shape
    tm = min(length, 512)
    per_batch = mod.shape[0] == bsz
    mspec = pl.BlockSpec((1, tm, 512), lambda b, i: (b, i, 0))
    return pl.pallas_call(
        _outproj_kernel,
        out_shape=jax.ShapeDtypeStruct((bsz, length, D_MODEL), F32),
        grid=(bsz, length // tm),
        in_specs=[pl.BlockSpec((1, tm, D_MODEL), lambda b, i: (b, i, 0)),
                  pl.BlockSpec((1, 6, D_MODEL), (lambda b, i: (b, 0, 0)) if per_batch
                               else (lambda b, i: (0, 0, 0))),
                  mspec, mspec, mspec, mspec,
                  pl.BlockSpec((D_MODEL, D_MODEL), lambda b, i: (0, 0))],
        out_specs=pl.BlockSpec((1, tm, D_MODEL), lambda b, i: (b, i, 0)),
        compiler_params=_cparams(2),
        name="outproj",
    )(x, mod, *mixers, w_bf)


def _mlp_kernel(x_ref, g_ref, mod_ref, w1_ref, w2_ref, fg_ref, o_ref, h_scr, acc_scr, *, final_norm):
    f = pl.program_id(2)

    @pl.when(f == 0)
    def _():
        h = _norm_mod(x_ref[0], g_ref[...], mod_ref[0, 3:4, :], mod_ref[0, 4:5, :])
        h_scr[...] = h.astype(BF16)
        acc_scr[...] = jnp.zeros_like(acc_scr)

    a = jnp.dot(h_scr[...], w1_ref[...], preferred_element_type=F32)
    a = jnp.square(jnp.maximum(a, 0.0)).astype(BF16)
    acc_scr[...] += jnp.dot(a, w2_ref[...], preferred_element_type=F32)

    @pl.when(f == pl.num_programs(2) - 1)
    def _():
        y = x_ref[0] + mod_ref[0, 5:6, :] * acc_scr[...]
        if final_norm:
            y = y * lax.rsqrt(jnp.mean(y * y, axis=-1, keepdims=True) + NORM_EPS) * fg_ref[...]
        o_ref[0] = y


def _mlp(x, g, mod, w1_bf, w2_bf, final_g):
    bsz, length, _ = x.shape
    tm = min(length, 512)
    tf = 1024
    per_batch = mod.shape[0] == bsz
    final_norm = final_g is not None
    fg = (final_g if final_norm else jnp.ones((D_MODEL,), F32)).reshape(1, D_MODEL)
    return pl.pallas_call(
        functools.partial(_mlp_kernel, final_norm=final_norm),
        out_shape=jax.ShapeDtypeStruct((bsz, length, D_MODEL), F32),
        grid=(bsz, length // tm, D_FF // tf),
        in_specs=[pl.BlockSpec((1, tm, D_MODEL), lambda b, i, f: (b, i, 0)),
                  pl.BlockSpec((1, D_MODEL), lambda b, i, f: (0, 0)),
                  pl.BlockSpec((1, 6, D_MODEL), (lambda b, i, f: (b, 0, 0)) if per_batch
                               else (lambda b, i, f: (0, 0, 0))),
                  pl.BlockSpec((D_MODEL, tf), lambda b, i, f: (0, f)),
                  pl.BlockSpec((tf, D_MODEL), lambda b, i, f: (f, 0)),
                  pl.BlockSpec((1, D_MODEL), lambda b, i, f: (0, 0))],
        out_specs=pl.BlockSpec((1, tm, D_MODEL), lambda b, i, f: (b, i, 0)),
        scratch_shapes=[pltpu.VMEM((tm, D_MODEL), BF16), pltpu.VMEM((tm, D_MODEL), F32)],
        compiler_params=_cparams(3),
        name="mlp",
    )(x, g.reshape(1, D_MODEL), mod, w1_bf, w2_bf, fg)


def _lin_scan(a, b, carry, rev):
    t_len, width = a.shape
    n_groups = t_len // 8
    a = a.reshape(n_groups, 8, width)
    b = b.reshape(n_groups, 8, width)
    sub = lax.broadcasted_iota(jnp.int32, a.shape, 1)
    for s in (1, 2, 4):
        if rev:
            a_sh = pltpu.roll(a, 8 - s, 1)
            b_sh = pltpu.roll(b, 8 - s, 1)
            ok = sub < 8 - s
        else:
            a_sh = pltpu.roll(a, s, 1)
            b_sh = pltpu.roll(b, s, 1)
            ok = sub >= s
        b = a * jnp.where(ok, b_sh, 0.0) + b
        a = a * jnp.where(ok, a_sh, 1.0)
    a = a.reshape(t_len, width)
    b = b.reshape(t_len, width)
    hs = [None] * n_groups
    for g in (range(n_groups - 1, -1, -1) if rev else range(n_groups)):
        h_g = a[8 * g:8 * g + 8, :] * carry + b[8 * g:8 * g + 8, :]
        hs[g] = h_g
        carry = h_g[0:1, :] if rev else h_g[7:8, :]
    return jnp.concatenate(hs, axis=0), carry


def _rglru_kernel(*refs, rev, row_len, final):
    if final:
        (x_ref, gate_ref, cw_ref, cb_ref, wa_ref, ba_ref, wx_ref, bx_ref, lam_ref, h0_ref, prev_ref,
         o_ref, st_ref, carry) = refs
    else:
        (x_ref, cw_ref, cb_ref, wa_ref, ba_ref, wx_ref, bx_ref, lam_ref, h0_ref,
         o_ref, st_ref, carry) = refs
    i = pl.program_id(1)

    @pl.when(i == 0)
    def _():
        carry[...] = h0_ref[0]

    xc = _short_conv(x_ref[0], cw_ref[...], row_len, 2) + cb_ref[...]
    xb = xc.astype(BF16)
    blocks = [slice(n * LANES, (n + 1) * LANES) for n in range(RG_W // LANES)]
    gate_r = jnp.concatenate([jnp.dot(xb[:, bs], wa_ref[n], preferred_element_type=F32)
                              for n, bs in enumerate(blocks)], axis=-1)
    gate_i = jnp.concatenate([jnp.dot(xb[:, bs], wx_ref[n], preferred_element_type=F32)
                              for n, bs in enumerate(blocks)], axis=-1)
    gate_r = _sigmoid(gate_r + ba_ref[...])
    gate_i = _sigmoid(gate_i + bx_ref[...])
    log_a = -RG_C * gate_r * _softplus(-lam_ref[...])
    a = jnp.exp(log_a)
    b = jnp.sqrt(-jnp.tanh(log_a) * (a * a + 1.0)) * (gate_i * xc)
    h, last = _lin_scan(a, b, carry[...], rev)
    carry[...] = last

    @pl.when(i == pl.num_programs(1) - 1)
    def _():
        st_ref[0] = last

    if final:
        gate = gate_ref[0]
        gelu = 0.5 * gate * (1.0 + jnp.tanh(math.sqrt(2.0 / math.pi) * (gate + 0.044715 * gate * gate * gate)))
        o_ref[0] = (gelu * (prev_ref[0] + h)).astype(o_ref.dtype)
    else:
        o_ref[0] = h


def _rglru_pass(p, prm, d, h0, prev, row_len):
    bsz, length, _ = p.shape
    conv_w, conv_b, wa, ba, wx, bx, lam = prm
    final = prev is not None
    rev = d == 1
    t = min(length, 256)
    nt = length // t
    tidx = (lambda i: nt - 1 - i) if rev else (lambda i: i)
    cx = COL_RG // RG_W
    col = lambda c0: pl.BlockSpec((1, t, RG_W), lambda b, i: (b, tidx(i), c0))
    vec = pl.BlockSpec((1, RG_W), lambda b, i: (0, 0))
    mat = pl.BlockSpec((RG_W // LANES, LANES, LANES), lambda b, i: (0, 0, 0))
    in_specs = [col(cx)]
    args = [p]
    if final:
        in_specs.append(col(cx + 1))
        args.append(p)
    in_specs += [pl.BlockSpec((4, RG_W), lambda b, i: (0, 0)), vec, mat, vec, mat, vec, vec,
                 pl.BlockSpec((1, 1, RG_W), lambda b, i: (b, 0, 0))]
    args += [conv_w, conv_b.reshape(1, RG_W), wa[d].astype(BF16), ba[d].reshape(1, RG_W),
             wx[d].astype(BF16), bx[d].reshape(1, RG_W), lam[d].reshape(1, RG_W), h0]
    if final:
        in_specs.append(pl.BlockSpec((1, t, RG_W), lambda b, i: (b, tidx(i), 0)))
        args.append(prev)
    out, st = pl.pallas_call(
        functools.partial(_rglru_kernel, rev=rev, row_len=row_len, final=final),
        out_shape=[jax.ShapeDtypeStruct((bsz, length, RG_W), BF16 if final else F32),
                   jax.ShapeDtypeStruct((bsz, 1, RG_W), F32)],
        grid=(bsz, nt),
        in_specs=in_specs,
        out_specs=[pl.BlockSpec((1, t, RG_W), lambda b, i: (b, tidx(i), 0)),
                   pl.BlockSpec((1, 1, RG_W), lambda b, i: (b, 0, 0))],
        scratch_shapes=[pltpu.VMEM((1, RG_W), F32)],
        compiler_params=_cparams(2),
        name="rglru_bwd" if rev else "rglru_fwd",
    )(*args)
    return out, st


def _rglru_mixer(p, prm, h0s, row_len):
    bsz = p.shape[0]
    if h0s is None:
        h0s = [jnp.zeros((bsz, 1, RG_W), F32)] * 2
    h_f, s_f = _rglru_pass(p, prm, 0, h0s[0], None, row_len)
    out, s_b = _rglru_pass(p, prm, 1, h0s[1], h_f, row_len)
    return out, [s_f, s_b]


def _tri_masks(n, rev):
    row = lax.broadcasted_iota(jnp.int32, (n, n), 0)
    col = lax.broadcasted_iota(jnp.int32, (n, n), 1)
    if rev:
        return row <= col, row < col
    return row >= col, row > col


def _dot_x3(a, b):
    a_hi = a.astype(BF16)
    a_lo = (a - a_hi.astype(F32)).astype(BF16)
    b_hi = b.astype(BF16)
    b_lo = (b - b_hi.astype(F32)).astype(BF16)
    mm = lambda p, q: jnp.dot(p, q, preferred_element_type=F32)
    n = b.shape[1]
    if n % LANES == 0:
        both = mm(a_hi, jnp.concatenate([b_hi, b_lo], axis=-1))
        return both[:, :n] + (both[:, n:] + mm(a_lo, b_hi))
    return mm(a_hi, b_hi) + (mm(a_hi, b_lo) + mm(a_lo, b_hi))


GD_SOLVE_DOT = _dot_x3


def _unit_solve(neg_n, rhs, steps, dot):
    x = rhs
    m = neg_n
    for s in range(steps):
        x = x + dot(m, x)
        if s + 1 < steps:
            m = dot(m, m)
    return x


def _chunk_cumsum(x, rev):
    t_len = x.shape[0]
    pos = lax.broadcasted_iota(jnp.int32, x.shape, 0) & (CHUNK - 1)
    s = 1
    while s < CHUNK:
        if rev:
            x = x + jnp.where(pos < CHUNK - s, pltpu.roll(x, t_len - s, 0), 0.0)
        else:
            x = x + jnp.where(pos >= s, pltpu.roll(x, s, 0), 0.0)
        s *= 2
    return x


def _lane_form(x):
    sel = (lax.broadcasted_iota(jnp.int32, x.shape, 1) == 0).astype(BF16)
    p1 = x.astype(BF16)
    r1 = x - p1.astype(F32)
    p2 = r1.astype(BF16)
    p3 = (r1 - p2.astype(F32)).astype(BF16)
    nt = lambda p: lax.dot_general(sel, p, (((1,), (1,)), ((), ())), preferred_element_type=F32)
    return nt(p1) + nt(p2) + nt(p3)


def _dot_exact_rhs(a, b_exact):
    a_hi = a.astype(BF16)
    a_lo = (a - a_hi.astype(F32)).astype(BF16)
    b = b_exact.astype(BF16)
    return (jnp.dot(a_hi, b, preferred_element_type=F32) + jnp.dot(a_lo, b, preferred_element_type=F32))


def _gdn_kernel(*refs, rev, row_len, final, d):
    if final:
        (q_ref, k_ref, v_ref, z_ref, gb_ref, cq_ref, ck_ref, cv_ref, alog_ref, dtb_ref, ng_ref, s0_ref,
         prev_ref, o_ref, st_ref, state) = refs
    else:
        (q_ref, k_ref, v_ref, gb_ref, cq_ref, ck_ref, cv_ref, alog_ref, dtb_ref, ng_ref, s0_ref,
         o_ref, st_ref, state) = refs
    grp = pl.program_id(1)
    i = pl.program_id(2)
    n_heads = q_ref.shape[2] // GD_HEAD

    @pl.when(i == 0)
    def _():
        state[...] = s0_ref[0]

    qc = _silu(_short_conv(q_ref[0], cq_ref[...], row_len, 2))
    kc = _silu(_short_conv(k_ref[0], ck_ref[...], row_len, 2))
    vc = _silu(_short_conv(v_ref[0], cv_ref[...], row_len, 2))
    gbb = gb_ref[0]
    lane = lax.broadcasted_iota(jnp.int32, gbb.shape, 1)
    t_len = qc.shape[0]
    n_chunks = t_len // CHUNK
    incl, strict = _tri_masks(CHUNK, rev)
    eye = (lax.broadcasted_iota(jnp.int32, (CHUNK, CHUNK), 0)
           == lax.broadcasted_iota(jnp.int32, (CHUNK, CHUNK), 1)).astype(F32)
    order = list(range(n_chunks - 1, -1, -1) if rev else range(n_chunks))
    bf = lambda x: x.astype(BF16)

    heads = []
    for hh in range(n_heads):
        hs = slice(hh * GD_HEAD, (hh + 1) * GD_HEAD)
        head = grp * n_heads + hh
        q_h, k_h = qc[:, hs], kc[:, hs]
        q_h = q_h * lax.rsqrt(jnp.sum(q_h * q_h, axis=-1, keepdims=True) + 1e-6) * (GD_HEAD ** -0.5)
        k_h = k_h * lax.rsqrt(jnp.sum(k_h * k_h, axis=-1, keepdims=True) + 1e-6)
        g_raw = jnp.sum(jnp.where(lane == d * GD_HEADS + head, gbb, 0.0), axis=-1, keepdims=True)
        b_raw = jnp.sum(jnp.where(lane == (2 + d) * GD_HEADS + head, gbb, 0.0), axis=-1, keepdims=True)
        g = -jnp.exp(alog_ref[hh]) * _softplus(g_raw + dtb_ref[hh])
        heads.append(dict(q=q_h, k=k_h, v=vc[:, hs], beta=_sigmoid(b_raw), gc=_chunk_cumsum(g, rev)))

    pre = {}
    s_cur = [state[hh] for hh in range(n_heads)]
    os = {}

    def decay_tiles(c):
        for hh in range(n_heads):
            hd = heads[hh]
            sl = slice(c * CHUNK, (c + 1) * CHUNK)
            q_c, k_c, v_c, b_c, gc = hd["q"][sl], hd["k"][sl], hd["v"][sl], hd["beta"][sl], hd["gc"][sl]
            g_row = _lane_form(gc)
            kb = k_c * b_c
            e_gc = jnp.exp(gc)
            g_last = gc[0:1, :] if rev else gc[CHUNK - 1:CHUNK, :]
            pre[hh, c] = dict(decay_in=jnp.exp(jnp.where(incl, gc[:, :CHUNK] - g_row, -jnp.inf)),
                              q=q_c, k=k_c, kb=kb, x=jnp.concatenate([v_c * b_c, kb * e_gc], axis=-1),
                              qg=bf(q_c * e_gc), k_dec=bf(k_c * jnp.exp(g_last - gc)), dec=jnp.exp(g_last))

    def score_tiles(c):
        for hh in range(n_heads):
            p = pre[hh, c]
            sc = _dot_nt(jnp.concatenate([p["kb"], p["q"]], axis=0), p["k"])
            p["m"] = -(sc[:CHUNK] * jnp.where(strict, p["decay_in"], 0.0))
            p["a_qk"] = bf(sc[CHUNK:] * p["decay_in"])

    def solve_step(c, s):
        for hh in range(n_heads):
            p = pre[hh, c]
            if s == 0:
                p["t"] = eye + p["m"]
                p["m"] = GD_SOLVE_DOT(p["m"], p["m"])
            elif s < 5:
                w = GD_SOLVE_DOT(p["m"], jnp.concatenate([p["t"], p["m"]], axis=-1))
                p["t"] = p["t"] + w[:, :CHUNK]
                p["m"] = w[:, CHUNK:]
            else:
                p["t"] = p["t"] + GD_SOLVE_DOT(p["m"], p["t"])

    def apply_inverse(c):
        for hh in range(n_heads):
            p = pre[hh, c]
            p["x"] = _dot(p["t"], p["x"])

    def recur(c):
        for hh in range(n_heads):
            p = pre.pop((hh, c))
            x = p["x"]
            t2 = _dot(jnp.concatenate([bf(x[:, GD_HEAD:]), p["qg"]], axis=0), s_cur[hh])
            v_new = bf(x[:, :GD_HEAD] - t2[:CHUNK])
            os[hh, c] = t2[CHUNK:] + _dot(p["a_qk"], v_new)
            s_cur[hh] = s_cur[hh] * p["dec"] + _dot_tn(p["k_dec"], v_new)

    phases = ([decay_tiles, score_tiles] + [functools.partial(solve_step, s=s) for s in range(6)]
              + [apply_inverse, recur])
    for slot in range(n_chunks + len(phases) - 1):
        for ph in range(len(phases) - 1, -1, -1):
            if 0 <= slot - ph < n_chunks:
                phases[ph](order[slot - ph])

    for hh in range(n_heads):
        hs = slice(hh * GD_HEAD, (hh + 1) * GD_HEAD)
        o_h = jnp.concatenate([os[hh, c] for c in range(n_chunks)], axis=0)
        if final:
            o_t = prev_ref[0, :, hs] + o_h
            o_t = o_t * lax.rsqrt(jnp.mean(o_t * o_t, axis=-1, keepdims=True) + NORM_EPS) * ng_ref[...]
            o_ref[0, :, hs] = (o_t * _silu(z_ref[0, :, hs])).astype(o_ref.dtype)
        else:
            o_ref[0, :, hs] = o_h
        state[hh] = s_cur[hh]

    @pl.when(i == pl.num_programs(2) - 1)
    def _():
        for hh in range(n_heads):
            st_ref[0, hh] = s_cur[hh]


GD_HEADS_PER_STEP = 2
GD_TILE = 512


def _gdn_pass(p, prm, d, s0, prev, row_len):
    bsz, length, _ = p.shape
    conv_w, a_log, dt_bias, norm_g = prm
    final = prev is not None
    rev = d == 1
    t = min(length, GD_TILE)
    nt = length // t
    tidx = (lambda i: nt - 1 - i) if rev else (lambda i: i)
    nh = GD_HEADS_PER_STEP
    wd = nh * GD_HEAD
    ng = GD_W // wd
    c0 = COL_GD // wd
    col = lambda off: pl.BlockSpec((1, t, wd), lambda b, h, i: (b, tidx(i), c0 + off * ng + h))
    cw = lambda off: pl.BlockSpec((4, wd), lambda b, h, i: (0, off * ng + h))
    hvec = pl.BlockSpec((nh, 1, LANES), lambda b, h, i: (h, 0, 0))
    in_specs = [col(0), col(1), col(2)]
    args = [p, p, p]
    if final:
        in_specs.append(col(3))
        args.append(p)
    in_specs += [pl.BlockSpec((1, t, LANES), lambda b, h, i: (b, tidx(i), COL_GD_GB // LANES)),
                 cw(0), cw(1), cw(2), hvec, hvec,
                 pl.BlockSpec((1, LANES), lambda b, h, i: (0, 0)),
                 pl.BlockSpec((1, nh, GD_HEAD, GD_HEAD), lambda b, h, i: (b, h, 0, 0))]
    bcast = lambda v: jnp.broadcast_to(v.reshape(GD_HEADS, 1, 1), (GD_HEADS, 1, LANES))
    args += [p, conv_w, conv_w, conv_w, bcast(a_log[d]), bcast(dt_bias[d]), norm_g.reshape(1, GD_HEAD), s0]
    if final:
        in_specs.append(pl.BlockSpec((1, t, wd), lambda b, h, i: (b, tidx(i), h)))
        args.append(prev)
    out, st = pl.pallas_call(
        functools.partial(_gdn_kernel, rev=rev, row_len=row_len, final=final, d=d),
        out_shape=[jax.ShapeDtypeStruct((bsz, length, GD_W), BF16 if final else F32),
                   jax.ShapeDtypeStruct((bsz, GD_HEADS, GD_HEAD, GD_HEAD), F32)],
        grid=(bsz, ng, nt),
        in_specs=in_specs,
        out_specs=[pl.BlockSpec((1, t, wd), lambda b, h, i: (b, tidx(i), h)),
                   pl.BlockSpec((1, nh, GD_HEAD, GD_HEAD), lambda b, h, i: (b, h, 0, 0))],
        scratch_shapes=[pltpu.VMEM((nh, GD_HEAD, GD_HEAD), F32)],
        compiler_params=_cparams(3),
        name="gdn_bwd" if rev else "gdn_fwd",
    )(*args)
    return out, st


def _gdn_mixer(p, prm, s0s, row_len):
    bsz = p.shape[0]
    if s0s is None:
        s0s = [jnp.zeros((bsz, GD_HEADS, GD_HEAD, GD_HEAD), F32)] * 2
    o_f, s_f = _gdn_pass(p, prm, 0, s0s[0], None, row_len)
    out, s_b = _gdn_pass(p, prm, 1, s0s[1], o_f, row_len)
    return out, [s_f, s_b]


def _stack_heads(x, lo):
    return jnp.concatenate([jnp.where(lo, x, 0.0), jnp.where(lo, 0.0, x)], axis=0)


def _rwkv_kernel(*refs, rev, final):
    if final:
        (r_ref, k_ref, v_ref, lo_ref, mur_ref, muk_ref, muv_ref, mul_ref, w0_ref, wup_ref, a0_ref, aup_ref,
         gup_ref, kk_ref, ka_ref, rk_ref, lng_ref, lnb_ref, s0_ref, prev_ref,
         o_ref, st_ref, state, c_r, c_k, c_v, c_l) = refs
    else:
        (r_ref, k_ref, v_ref, lo_ref, mur_ref, muk_ref, muv_ref, mul_ref, w0_ref, wup_ref, a0_ref, aup_ref,
         gup_ref, kk_ref, ka_ref, rk_ref, lng_ref, lnb_ref, s0_ref,
         o_ref, st_ref, state, c_r, c_k, c_v, c_l) = refs
    i = pl.program_id(2)

    @pl.when(i == 0)
    def _():
        state[...] = s0_ref[0]
        c_r[...] = jnp.zeros_like(c_r)
        c_k[...] = jnp.zeros_like(c_k)
        c_v[...] = jnp.zeros_like(c_v)
        c_l[...] = jnp.zeros_like(c_l)

    t_len = r_ref.shape[1]

    def shifted(x_ref, carry, mu_ref):
        x = x_ref[0]
        row = lax.broadcasted_iota(jnp.int32, x.shape, 0)
        if rev:
            prev = jnp.where(row == t_len - 1, carry[...], pltpu.roll(x, t_len - 1, 0))
            carry[...] = x[0:1, :]
        else:
            prev = jnp.where(row == 0, carry[...], pltpu.roll(x, 1, 0))
            carry[...] = x[t_len - 1:t_len, :]
        return x + (prev - x) * mu_ref[...]

    r = shifted(r_ref, c_r, mur_ref)
    k = shifted(k_ref, c_k, muk_ref)
    v = shifted(v_ref, c_v, muv_ref)
    lora = shifted(lo_ref, c_l, mul_ref)
    xw, xa, xg = lora[:, 0:64], lora[:, 64:128], lora[:, 128:256]

    width = r_ref.shape[2]
    n_pairs = width // LANES
    lane = lax.broadcasted_iota(jnp.int32, (1, LANES), 1)
    lo = lane < RW_HEAD
    head_sum = (lax.broadcasted_iota(jnp.int32, (width, width), 0) // RW_HEAD
                == lax.broadcasted_iota(jnp.int32, (width, width), 1) // RW_HEAD).astype(F32)
    rowh = lax.broadcasted_iota(jnp.int32, (LANES, LANES), 0) // RW_HEAD
    colh = lax.broadcasted_iota(jnp.int32, (LANES, LANES), 1) // RW_HEAD

    lw = -math.exp(-0.5) * _sigmoid(w0_ref[...] + _dot(jnp.tanh(xw), wup_ref[...]))
    a = _sigmoid(a0_ref[...] + _dot(xa, aup_ref[...]))
    kk = k * kk_ref[...]
    kappa = kk / jnp.maximum(jnp.sqrt(_dot_exact_rhs(kk * kk, head_sum)), 1e-12)
    kt = k * (1.0 + (a - 1.0) * ka_ref[...])
    gate = _dot(_sigmoid(xg), gup_ref[...])
    bonus = _dot_exact_rhs(r * kt * rk_ref[...], head_sum) * v

    n_chunks = t_len // CHUNK
    lg_all = _chunk_cumsum(lw, rev)
    rowt = lax.broadcasted_iota(jnp.int32, (LANES, LANES), 0) & (CHUNK - 1)
    colt = lax.broadcasted_iota(jnp.int32, (LANES, LANES), 1) & (CHUNK - 1)
    same = rowh == colh
    incl2 = ((rowt <= colt) if rev else (rowt >= colt)) & same
    strict2 = ((rowt < colt) if rev else (rowt > colt)) & same
    eye2 = (lax.broadcasted_iota(jnp.int32, (LANES, LANES), 0)
            == lax.broadcasted_iota(jnp.int32, (LANES, LANES), 1)).astype(F32)
    order = list(range(n_chunks - 1, -1, -1) if rev else range(n_chunks))
    bf = lambda x: x.astype(BF16)

    pre = {}
    s_cur = [state[pp] for pp in range(n_pairs)]
    ys = {}

    def scores(c):
        for pp in range(n_pairs):
            sl = (slice(c * CHUNK, (c + 1) * CHUNK), slice(pp * LANES, (pp + 1) * LANES))
            lw_c, kap_c, a_c = lw[sl], kappa[sl], a[sl]
            lg = lg_all[sl]
            lg_tot = lg[0:1, :] if rev else lg[CHUNK - 1:CHUNK, :]
            e_neg = jnp.exp(-lg)
            e_rem = jnp.exp(lg_tot - lg)
            p_raw = -(kap_c * a_c)
            q2 = bf(_stack_heads(kap_c * jnp.exp(lg - lw_c), lo))
            p2 = bf(_stack_heads(p_raw * e_neg, lo))
            k2 = bf(_stack_heads(kt[sl] * e_neg, lo))
            r2 = bf(_stack_heads(r[sl] * jnp.exp(lg), lo))
            sc = _dot_nt(jnp.concatenate([q2, r2], axis=0), jnp.concatenate([p2, k2], axis=0))
            pre[pp, c] = dict(
                q2=q2, r2=r2, v2=bf(_stack_heads(v[sl], lo)),
                pt2=bf(_stack_heads(p_raw * e_rem, lo)), kt2=bf(_stack_heads(kt[sl] * e_rem, lo)),
                m=jnp.where(strict2, sc[:LANES, :LANES], 0.0),
                a_qk=jnp.where(strict2, sc[:LANES, LANES:], 0.0),
                a_rp=bf(jnp.where(incl2, sc[LANES:, :LANES], 0.0)),
                a_rk=jnp.where(incl2, sc[LANES:, LANES:], 0.0),
                dec=jnp.broadcast_to(jnp.exp(lg_tot), (LANES, LANES)).T)

    def local_terms(c):
        for pp in range(n_pairs):
            p = pre[pp, c]
            p["rhs"] = jnp.concatenate([p["q2"], bf(_dot(p["a_qk"], p["v2"]))], axis=-1)
            p["y_loc"] = _dot(p["a_rk"], p["v2"])
            p["s_loc"] = _dot_tn(p["kt2"], p["v2"])

    def solve_step(c, s):
        for pp in range(n_pairs):
            p = pre[pp, c]
            if s == 0:
                p["t"] = eye2 + p["m"]
                p["m"] = _dot(p["m"], p["m"])
            elif s < 5:
                w = _dot(p["m"], jnp.concatenate([p["t"], p["m"]], axis=-1))
                p["t"] = p["t"] + w[:, :LANES]
                p["m"] = w[:, LANES:]
            else:
                p["t"] = p["t"] + _dot(p["m"], p["t"])

    def apply_inverse(c):
        for pp in range(n_pairs):
            p = pre[pp, c]
            p["x"] = _dot(p["t"], p["rhs"])

    def recur(c):
        for pp in range(n_pairs):
            p = pre.pop((pp, c))
            x = p["x"]
            t2 = _dot(jnp.concatenate([bf(x[:, :LANES]), p["r2"]], axis=0), s_cur[pp])
            u2 = bf(t2[:LANES] + x[:, LANES:])
            y2 = t2[LANES:] + _dot(p["a_rp"], u2) + p["y_loc"]
            s_cur[pp] = s_cur[pp] * p["dec"] + _dot_tn(p["pt2"], u2) + p["s_loc"]
            ys[pp, c] = y2[:CHUNK] + y2[CHUNK:]

    phases = ([scores, local_terms] + [functools.partial(solve_step, s=s) for s in range(6)]
              + [apply_inverse, recur])
    for slot in range(n_chunks + len(phases) - 1):
        for ph in range(len(phases) - 1, -1, -1):
            if 0 <= slot - ph < n_chunks:
                phases[ph](order[slot - ph])

    y = jnp.concatenate([jnp.concatenate([ys[pp, c] for c in range(n_chunks)], axis=0)
                         for pp in range(n_pairs)], axis=1)
    mean = _dot_exact_rhs(y, head_sum) * (1.0 / RW_HEAD)
    yc = y - mean
    var = _dot_exact_rhs(yc * yc, head_sum) * (1.0 / RW_HEAD)
    yn = yc * lax.rsqrt(var + RW_GN_EPS) * lng_ref[...] + lnb_ref[...]
    out = (yn + bonus) * gate
    if final:
        o_ref[0] = (prev_ref[0] + out).astype(o_ref.dtype)
    else:
        o_ref[0] = out
    for pp in range(n_pairs):
        state[pp] = s_cur[pp]

    @pl.when(i == pl.num_programs(2) - 1)
    def _():
        for pp in range(n_pairs):
            st_ref[0, pp] = s_cur[pp]


RW_PAIRS_PER_STEP = 2
RW_TILE = 512


def _rwkv_pass(p, prm, d, s0, prev):
    bsz, length, _ = p.shape
    mu, w0, w_up, a0, a_up, g_up, k_k, k_a, r_k, ln_g, ln_b = [t[d] for t in prm]
    final = prev is not None
    rev = d == 1
    t = min(length, RW_TILE)
    nt = length // t
    tidx = (lambda i: nt - 1 - i) if rev else (lambda i: i)
    npp = RW_PAIRS_PER_STEP
    wd = npp * LANES
    c0 = COL_RW // wd
    ng = RW_W // wd
    col = lambda off: pl.BlockSpec((1, t, wd), lambda b, n, i: (b, tidx(i), c0 + off * ng + n))
    vec = pl.BlockSpec((1, wd), lambda b, n, i: (0, n))
    whole = lambda shape: pl.BlockSpec(shape, lambda b, n, i: (0,) * len(shape))
    row = lambda x: x.reshape(1, -1)
    in_specs = [col(0), col(1), col(2),
                pl.BlockSpec((1, t, 256), lambda b, n, i: (b, tidx(i), COL_RW_LORA // 256)),
                vec, vec, vec, whole((1, 256)),
                vec, pl.BlockSpec((64, wd), lambda b, n, i: (0, n)),
                vec, pl.BlockSpec((64, wd), lambda b, n, i: (0, n)),
                pl.BlockSpec((LANES, wd), lambda b, n, i: (0, n)),
                vec, vec, vec, vec, vec,
                pl.BlockSpec((1, npp, LANES, LANES), lambda b, n, i: (b, n, 0, 0))]
    args = [p, p, p, p,
            row(mu[0:512]), row(mu[512:1024]), row(mu[1024:1536]), row(mu[1536:1792]),
            row(w0), w_up.astype(BF16), row(a0), a_up.astype(BF16), g_up.astype(BF16),
            row(k_k), row(k_a), row(r_k), row(ln_g), row(ln_b), s0]
    if final:
        in_specs.append(pl.BlockSpec((1, t, wd), lambda b, n, i: (b, tidx(i), n)))
        args.append(prev)
    out, st = pl.pallas_call(
        functools.partial(_rwkv_kernel, rev=rev, final=final),
        out_shape=[jax.ShapeDtypeStruct((bsz, length, RW_W), BF16 if final else F32),
                   jax.ShapeDtypeStruct((bsz, 4, LANES, LANES), F32)],
        grid=(bsz, ng, nt),
        in_specs=in_specs,
        out_specs=[pl.BlockSpec((1, t, wd), lambda b, n, i: (b, tidx(i), n)),
                   pl.BlockSpec((1, npp, LANES, LANES), lambda b, n, i: (b, n, 0, 0))],
        scratch_shapes=[pltpu.VMEM((npp, LANES, LANES), F32), pltpu.VMEM((1, wd), F32),
                        pltpu.VMEM((1, wd), F32), pltpu.VMEM((1, wd), F32),
                        pltpu.VMEM((1, 256), F32)],
        compiler_params=_cparams(3),
        name="rwkv_bwd" if rev else "rwkv_fwd",
    )(*args)
    return out, st


def _rwkv_mixer(p, prm, s0s):
    bsz = p.shape[0]
    if s0s is None:
        s0s = [jnp.zeros((bsz, 4, LANES, LANES), F32)] * 2
    o_f, s_f = _rwkv_pass(p, prm, 0, s0s[0], None)
    out, s_b = _rwkv_pass(p, prm, 1, s0s[1], o_f)
    return out, [s_f, s_b]


def _dft_table_kernel(ca_ref, sa_ref, cb_ref, sb_ref, c_ref, s_ref):
    cb, sb = cb_ref[...], sb_ref[...]
    for j in range(ca_ref.shape[1]):
        ca, sa = ca_ref[:, j:j + 1], sa_ref[:, j:j + 1]
        c_ref[:, j * LANES:(j + 1) * LANES] = (ca * cb - sa * sb).astype(c_ref.dtype)
        s_ref[:, j * LANES:(j + 1) * LANES] = (-(sa * cb + ca * sb)).astype(s_ref.dtype)


def _dft_tables(length):
    n = 2 * length
    nfp = -(-(length + 1) // LANES) * LANES
    kf = jnp.arange(nfp, dtype=jnp.int32)[:, None]
    s1 = jnp.arange(n // LANES, dtype=jnp.int32)[None, :]
    s0 = jnp.arange(LANES, dtype=jnp.int32)[None, :]
    ang_a = (2.0 * math.pi / n) * ((kf * s1 * LANES) % n).astype(F32)
    ang_b = (2.0 * math.pi / n) * ((kf * s0) % n).astype(F32)
    ok = (kf <= length).astype(F32)
    tm = 384 if nfp % 384 == 0 else LANES
    rows = lambda w: pl.BlockSpec((tm, w), lambda i: (i, 0))
    tab_c, tab_s = pl.pallas_call(
        _dft_table_kernel,
        out_shape=[jax.ShapeDtypeStruct((nfp, n), BF16)] * 2,
        grid=(nfp // tm,),
        in_specs=[rows(n // LANES), rows(n // LANES), rows(LANES), rows(LANES)],
        out_specs=[rows(n), rows(n)],
        compiler_params=_cparams(1),
        name="hyena_dft_tables",
    )(jnp.cos(ang_a) * ok, jnp.sin(ang_a) * ok, jnp.cos(ang_b), jnp.sin(ang_b))
    kk = kf[:, 0]
    wk = jnp.where((kk == 0) | (kk == length), 1.0, 2.0) * (kk <= length) / n
    return tab_c, tab_s, jnp.broadcast_to(wk[:, None], (nfp, LANES)).astype(F32)


def _hyena_filter(length, w1, b1, w2, b2, w3, freq):
    t = jnp.arange(length, dtype=F32)
    z = t / max(length - 1, 1)
    bands = jnp.linspace(1e-4, HY_BANDS - 1, HY_BANDS, dtype=F32)
    ang = (2.0 * math.pi / length) * t[:, None] * bands[None, :]
    feat = jnp.concatenate([z[:, None], jnp.cos(ang), -jnp.sin(ang)], axis=-1)
    h = jnp.sin(freq[0] * (feat @ w1 + b1))
    h = jnp.sin(freq[1] * (h @ w2 + b2))
    h = (h @ w3).astype(F32)
    deltas = jnp.abs(jnp.linspace(math.log(1e-2) / 1.5, math.log(1e-2) / 0.3, HY_W, dtype=F32))
    h = h * jnp.exp(-z[:, None] * jnp.tile(deltas, 2)[None, :])
    h_fwd = h[:, :HY_W]
    h_bwd = jnp.where(t[:, None] > 0, h[:, HY_W:], 0.0)
    norm = jnp.sum(jnp.abs(h_fwd) + jnp.abs(h_bwd), axis=0, keepdims=True)
    return (h_fwd + h_bwd) / norm, (h_fwd - h_bwd) / norm


def _hy_zin_kernel(v_ref, x1_ref, cw_ref, o_ref, *, row_len):
    vc = _short_conv(v_ref[0], cw_ref[:, 0:512], row_len, 1)
    x1c = _short_conv(x1_ref[0], cw_ref[:, 1024:1536], row_len, 1)
    o_ref[0] = (x1c * vc).astype(o_ref.dtype)


def _hy_zin(p, conv_w, row_len):
    bsz, length, _ = p.shape
    t = min(length, 256)
    return pl.pallas_call(
        functools.partial(_hy_zin_kernel, row_len=row_len),
        out_shape=jax.ShapeDtypeStruct((bsz, length, HY_W), BF16),
        grid=(bsz, length // t),
        in_specs=[pl.BlockSpec((1, t, 512), lambda b, i: (b, i, 0)),
                  pl.BlockSpec((1, t, 512), lambda b, i: (b, i, 2)),
                  pl.BlockSpec((3, 1536), lambda b, i: (0, 0))],
        out_specs=pl.BlockSpec((1, t, 512), lambda b, i: (b, i, 0)),
        compiler_params=_cparams(2),
        name="hyena_zin",
    )(p, p, conv_w)


def _dft_fwd_kernel(*refs, mult):
    if mult:
        c_ref, s_ref, z_ref, fr_ref, fi_ref, yr_ref, yi_ref = refs
        z_cos = z_sin = z_ref[0].astype(BF16)
    else:
        c_ref, s_ref, zc_ref, zs_ref, wk_ref, yr_ref, yi_ref = refs
        z_cos, z_sin = zc_ref[0].astype(BF16), zs_ref[0].astype(BF16)
    zr = jnp.dot(c_ref[...], z_cos, preferred_element_type=F32)
    zi = jnp.dot(s_ref[...], z_sin, preferred_element_type=F32)
    if mult:
        fr, fi = fr_ref[...], fi_ref[...]
        zr, zi = zr * fr - zi * fi, zr * fi + zi * fr
    else:
        zr, zi = zr * wk_ref[:, 0:1], zi * wk_ref[:, 0:1]
    yr_ref[0] = zr.astype(yr_ref.dtype)
    yi_ref[0] = zi.astype(yi_ref.dtype)


def _dft_fwd(tab_c, tab_s, z, spec, wk=None):
    mult = spec is not None
    zs = (z,) if mult else z
    bsz, klen, _ = zs[0].shape
    nfp = tab_c.shape[0]
    tm = 384 if nfp % 384 == 0 else LANES
    in_specs = [pl.BlockSpec((tm, klen), lambda b, i: (i, 0)),
                pl.BlockSpec((tm, klen), lambda b, i: (i, 0))]
    in_specs += [pl.BlockSpec((1, klen, 512), lambda b, i: (b, 0, 0))] * len(zs)
    args = [tab_c, tab_s, *zs]
    if mult:
        in_specs += [pl.BlockSpec((tm, 512), lambda b, i: (i, 0))] * 2
        args += list(spec)
    else:
        in_specs.append(pl.BlockSpec((tm, LANES), lambda b, i: (i, 0)))
        args.append(wk)
    odt = BF16 if mult else F32
    return pl.pallas_call(
        functools.partial(_dft_fwd_kernel, mult=mult),
        out_shape=[jax.ShapeDtypeStruct((bsz, nfp, 512), odt)] * 2,
        grid=(bsz, nfp // tm),
        in_specs=in_specs,
        out_specs=[pl.BlockSpec((1, tm, 512), lambda b, i: (b, i, 0))] * 2,
        compiler_params=_cparams(2),
        name="hyena_dft_mul" if mult else "hyena_dft_filter",
    )(*args)


def _dft_inv_kernel(ci_ref, si_ref, yr_ref, yi_ref, v_ref, x0_ref, x1_ref, cw_ref, skip_ref, o_ref, *, row_len):
    y = jnp.dot(ci_ref[...], yr_ref[0], preferred_element_type=F32)
    y += jnp.dot(si_ref[...], yi_ref[0], preferred_element_type=F32)
    vc = _short_conv(v_ref[0], cw_ref[:, 0:512], row_len, 1)
    x0c = _short_conv(x0_ref[0], cw_ref[:, 512:1024], row_len, 1)
    x1c = _short_conv(x1_ref[0], cw_ref[:, 1024:1536], row_len, 1)
    zin = x1c * vc
    o_ref[0] = (x0c * (y + zin * skip_ref[...])).astype(o_ref.dtype)


def _dft_inv(tab_c, tab_s, yr, yi, p, conv_w, skip, row_len):
    bsz, length, _ = p.shape
    nfp = tab_c.shape[0]
    t = min(length, 256)
    pcol = lambda c: pl.BlockSpec((1, t, 512), lambda b, i: (b, i, c))
    return pl.pallas_call(
        functools.partial(_dft_inv_kernel, row_len=row_len),
        out_shape=jax.ShapeDtypeStruct((bsz, length, HY_W), BF16),
        grid=(bsz, length // t),
        in_specs=[pl.BlockSpec((t, nfp), lambda b, i: (i, 0)),
                  pl.BlockSpec((t, nfp), lambda b, i: (i, 0)),
                  pl.BlockSpec((1, nfp, 512), lambda b, i: (b, 0, 0)),
                  pl.BlockSpec((1, nfp, 512), lambda b, i: (b, 0, 0)),
                  pcol(0), pcol(1), pcol(2),
                  pl.BlockSpec((3, 1536), lambda b, i: (0, 0)),
                  pl.BlockSpec((1, 512), lambda b, i: (0, 0))],
        out_specs=pl.BlockSpec((1, t, 512), lambda b, i: (b, i, 0)),
        compiler_params=_cparams(2),
        name="hyena_idft_gate",
    )(tab_c, tab_s, yr, yi, p, p, p, conv_w, skip.reshape(1, HY_W))


def _hyena_mixer(p, prm, tables, row_len):
    conv_w, w1, b1, w2, b2, w3, freq, skip = prm
    length = p.shape[1]
    tab_c, tab_s, wk = tables
    f_cos, f_sin = _hyena_filter(length, w1, b1, w2, b2, w3, freq)
    spec = _dft_fwd(tab_c, tab_s, (f_cos[None], f_sin[None]), None, wk)
    spec = (spec[0][0], spec[1][0])
    zin = _hy_zin(p, conv_w, row_len)
    yr, yi = _dft_fwd(tab_c, tab_s, zin, spec)
    return _dft_inv(tab_c, tab_s, yr, yi, p, conv_w, skip, row_len)


def _permute_w_in(w_in):
    hy = w_in[:, 0:1536]
    rw = w_in[:, 1536:3328]
    gd = w_in[:, 3328:5392]
    rg = w_in[:, 5392:6416]
    pad = jnp.zeros((w_in.shape[0], N_PROJ - 6416), w_in.dtype)
    return jnp.concatenate([hy, rw[:, :1536], gd[:, :2048], rg, rw[:, 1536:], gd[:, 2048:], pad],
                           axis=-1).astype(BF16)


def kernel(x, c, ctx, c_ctx, ada_w, ada_b, norm_mix_g, norm_mlp_g, w_in, w_out,
           hy_conv, hy_w1, hy_b1, hy_w2, hy_b2, hy_w3, hy_freq, hy_skip,
           rw_mu, rw_w0, rw_w_up, rw_a0, rw_a_up, rw_g_up, rw_k_k, rw_k_a, rw_r_k, rw_ln_g, rw_ln_b,
           gd_conv, gd_a_log, gd_dt_bias, gd_norm_g,
           rg_conv, rg_conv_b, rg_wa, rg_ba, rg_wx, rg_bx, rg_lambda,
           mlp_w1, mlp_w2, final_norm_g):
    bsz, seq, _ = x.shape
    ctx_len = ctx.shape[1]
    depth = ada_w.shape[0]
    tables_x = _dft_tables(seq)
    tables_c = _dft_tables(ctx_len)
    cond8 = jnp.concatenate([c, c_ctx[None, :], jnp.zeros((8 - bsz - 1, D_MODEL), F32)], axis=0)
    for l in range(depth):
        last = l == depth - 1
        mod = _modulation(cond8, ada_w, ada_b[l], l)
        mod_x = mod[:bsz].reshape(bsz, 6, D_MODEL)
        mod_c = mod[bsz:bsz + 1].reshape(1, 6, D_MODEL)
        w_in_bf = _permute_w_in(w_in[l])
        w_out_bf = w_out[l].astype(BF16)
        w1_bf = mlp_w1[l].astype(BF16)
        w2_bf = mlp_w2[l].astype(BF16)
        px = _inproj(x, norm_mix_g[l], mod_x, w_in_bf)
        pc = _inproj(ctx, norm_mix_g[l], mod_c, w_in_bf)
        hy_prm = (hy_conv[l], hy_w1[l], hy_b1[l], hy_w2[l], hy_b2[l], hy_w3[l], hy_freq[l], hy_skip[l])
        rw_prm = (rw_mu[l], rw_w0[l], rw_w_up[l], rw_a0[l], rw_a_up[l], rw_g_up[l],
                  rw_k_k[l], rw_k_a[l], rw_r_k[l], rw_ln_g[l], rw_ln_b[l])
        gd_prm = (gd_conv[l], gd_a_log[l], gd_dt_bias[l], gd_norm_g[l])
        rg_prm = (rg_conv[l], rg_conv_b[l], rg_wa[l], rg_ba[l], rg_wx[l], rg_bx[l], rg_lambda[l])
        c_rw, s_rw = _rwkv_mixer(pc, rw_prm, None)
        c_gd, s_gd = _gdn_mixer(pc, gd_prm, None, ctx_len)
        c_rg, s_rg = _rglru_mixer(pc, rg_prm, None, ctx_len)
        x_hy = _hyena_mixer(px, hy_prm, tables_x, GRID_W)
        x_rw, _ = _rwkv_mixer(px, rw_prm, s_rw)
        x_gd, _ = _gdn_mixer(px, gd_prm, s_gd, GRID_W)
        x_rg, _ = _rglru_mixer(px, rg_prm, s_rg, GRID_W)
        x_new = _outproj(x, mod_x, (x_hy, x_rw, x_gd, x_rg), w_out_bf)
        x_new = _mlp(x_new, norm_mlp_g[l], mod_x, w1_bf, w2_bf, final_norm_g if last else None)
        if not last:
            c_hy = _hyena_mixer(pc, hy_prm, tables_c, ctx_len)
            ctx_new = _outproj(ctx, mod_c, (c_hy, c_rw, c_gd, c_rg), w_out_bf)
            ctx = _mlp(ctx_new, norm_mlp_g[l], mod_c, w1_bf, w2_bf, None)
        x = x_new
    return x
```

```python
import functools
import math

import jax
import jax.numpy as jnp
from jax import lax
from jax.experimental import pallas as pl
from jax.experimental.pallas import tpu as pltpu

F32 = jnp.float32
BF16 = jnp.bfloat16
HIGHEST = lax.Precision.HIGHEST

D_MODEL = 2048
GRID_W = 64
HY_W = RW_W = GD_W = RG_W = 512
D_FF = 4 * D_MODEL
NORM_EPS = 1e-6
HY_EMB = 33
HY_BANDS = 16
RW_HEAD = 64
RW_GN_EPS = 64e-5
GD_HEAD = 128
GD_HEADS = 4
CHUNK = 64
RG_C = 8.0
LANES = 128

COL_HY = 0
COL_RW = 1536
COL_GD = 3072
COL_RG = 5120
COL_RW_LORA = 6144
COL_GD_GB = 6400
N_PROJ = 6656

VMEM_LIMIT = 56 * 1024 * 1024


def _cparams(n_axes):
    return pltpu.CompilerParams(dimension_semantics=("arbitrary",) * n_axes,
                                vmem_limit_bytes=VMEM_LIMIT)


def _dot(a, b):
    return jnp.dot(a.astype(BF16), b.astype(BF16), preferred_element_type=F32)


def _dot_nt(a, b):
    return lax.dot_general(a.astype(BF16), b.astype(BF16), (((1,), (1,)), ((), ())),
                           preferred_element_type=F32)


def _dot_tn(a, b):
    return lax.dot_general(a.astype(BF16), b.astype(BF16), (((0,), (0,)), ((), ())),
                           preferred_element_type=F32)


def _dot_hp(a, b):
    return jnp.dot(a, b, precision=HIGHEST, preferred_element_type=F32)


def _dot_nt_hp(a, b):
    return lax.dot_general(a, b, (((1,), (1,)), ((), ())), precision=HIGHEST,
                           preferred_element_type=F32)


def _sigmoid(x):
    return 1.0 / (1.0 + jnp.exp(-x))


def _silu(x):
    return x * _sigmoid(x)


def _softplus(x):
    return jnp.maximum(x, 0.0) + jnp.log1p(jnp.exp(-jnp.abs(x)))


def _short_conv(u, w, row_len, pad_left):
    t_len = u.shape[0]
    pos = lax.broadcasted_iota(jnp.int32, u.shape, 0) & (row_len - 1)
    y = None
    for j in range(w.shape[0]):
        off = j - pad_left
        if off == 0:
            term = u * w[j:j + 1, :]
        else:
            sh = pltpu.roll(u, (-off) % t_len, 0)
            ok = (pos + off >= 0) & (pos + off < row_len)
            term = jnp.where(ok, sh, 0.0) * w[j:j + 1, :]
        y = term if y is None else y + term
    return y


def _mod_kernel(c_ref, w_ref, b_ref, o_ref):
    o_ref[...] = _dot(_silu(c_ref[...]), w_ref[0]) + b_ref[...]


def _modulation(cond8, ada_w, ada_b, layer):
    n = ada_w.shape[2]
    tn = 512
    return pl.pallas_call(
        _mod_kernel,
        out_shape=jax.ShapeDtypeStruct((8, n), F32),
        grid=(n // tn,),
        in_specs=[pl.BlockSpec((8, D_MODEL), lambda j: (0, 0)),
                  pl.BlockSpec((1, D_MODEL, tn), lambda j: (layer, 0, j)),
                  pl.BlockSpec((1, tn), lambda j: (0, j))],
        out_specs=pl.BlockSpec((8, tn), lambda j: (0, j)),
        compiler_params=_cparams(1),
        name="adaln_mod",
    )(cond8, ada_w, ada_b.reshape(1, n))


def _norm_mod(x, g, shift, scale):
    y = x * lax.rsqrt(jnp.mean(x * x, axis=-1, keepdims=True) + NORM_EPS) * g
    return y * (1.0 + scale) + shift


def _inproj_kernel(x_ref, g_ref, mod_ref, w_ref, o_ref, h_scr):
    @pl.when(pl.program_id(2) == 0)
    def _():
        h = _norm_mod(x_ref[0], g_ref[...], mod_ref[0, 0:1, :], mod_ref[0, 1:2, :])
        h_scr[...] = h.astype(BF16)

    o_ref[0] = jnp.dot(h_scr[...], w_ref[...], preferred_element_type=F32)


def _inproj(x, g, mod, w_bf):
    bsz, length, _ = x.shape
    tm = min(length, 1024)
    tn = 512
    per_batch = mod.shape[0] == bsz
    return pl.pallas_call(
        _inproj_kernel,
        out_shape=jax.ShapeDtypeStruct((bsz, length, N_PROJ), F32),
        grid=(bsz, length // tm, N_PROJ // tn),
        in_specs=[pl.BlockSpec((1, tm, D_MODEL), lambda b, i, j: (b, i, 0)),
                  pl.BlockSpec((1, D_MODEL), lambda b, i, j: (0, 0)),
                  pl.BlockSpec((1, 6, D_MODEL), (lambda b, i, j: (b, 0, 0)) if per_batch
                               else (lambda b, i, j: (0, 0, 0))),
                  pl.BlockSpec((D_MODEL, tn), lambda b, i, j: (0, j))],
        out_specs=pl.BlockSpec((1, tm, tn), lambda b, i, j: (b, i, j)),
        scratch_shapes=[pltpu.VMEM((tm, D_MODEL), BF16)],
        compiler_params=_cparams(3),
        name="inproj",
    )(x, g.reshape(1, D_MODEL), mod, w_bf)


def _outproj_kernel(x_ref, mod_ref, m0_ref, m1_ref, m2_ref, m3_ref, w_ref, o_ref):
    acc = jnp.dot(m0_ref[0], w_ref[0:512, :], preferred_element_type=F32)
    acc += jnp.dot(m1_ref[0], w_ref[512:1024, :], preferred_element_type=F32)
    acc += jnp.dot(m2_ref[0], w_ref[1024:1536, :], preferred_element_type=F32)
    acc += jnp.dot(m3_ref[0], w_ref[1536:2048, :], preferred_element_type=F32)
    o_ref[0] = x_ref[0] + mod_ref[0, 2:3, :] * acc


def _outproj(x, mod, mixers, w_bf):
    bsz, length, _ = x.shape
    tm = min(length, 512)
    per_batch = mod.shape[0] == bsz
    mspec = pl.BlockSpec((1, tm, 512), lambda b, i: (b, i, 0))
    return pl.pallas_call(
        _outproj_kernel,
        out_shape=jax.ShapeDtypeStruct((bsz, length, D_MODEL), F32),
        grid=(bsz, length // tm),
        in_specs=[pl.BlockSpec((1, tm, D_MODEL), lambda b, i: (b, i, 0)),
                  pl.BlockSpec((1, 6, D_MODEL), (lambda b, i: (b, 0, 0)) if per_batch
                               else (lambda b, i: (0, 0, 0))),
                  mspec, mspec, mspec, mspec,
                  pl.BlockSpec((D_MODEL, D_MODEL), lambda b, i: (0, 0))],
        out_specs=pl.BlockSpec((1, tm, D_MODEL), lambda b, i: (b, i, 0)),
        compiler_params=_cparams(2),
        name="outproj",
    )(x, mod, *mixers, w_bf)


def _mlp_kernel(x_ref, g_ref, mod_ref, w1_ref, w2_ref, fg_ref, o_ref, h_scr, acc_scr, *, final_norm):
    f = pl.program_id(2)

    @pl.when(f == 0)
    def _():
        h = _norm_mod(x_ref[0], g_ref[...], mod_ref[0, 3:4, :], mod_ref[0, 4:5, :])
        h_scr[...] = h.astype(BF16)
        acc_scr[...] = jnp.zeros_like(acc_scr)

    a = jnp.dot(h_scr[...], w1_ref[...], preferred_element_type=F32)
    a = jnp.square(jnp.maximum(a, 0.0)).astype(BF16)
    acc_scr[...] += jnp.dot(a, w2_ref[...], preferred_element_type=F32)

    @pl.when(f == pl.num_programs(2) - 1)
    def _():
        y = x_ref[0] + mod_ref[0, 5:6, :] * acc_scr[...]
        if final_norm:
            y = y * lax.rsqrt(jnp.mean(y * y, axis=-1, keepdims=True) + NORM_EPS) * fg_ref[...]
        o_ref[0] = y


def _mlp(x, g, mod, w1_bf, w2_bf, final_g):
    bsz, length, _ = x.shape
    tm = min(length, 512)
    tf = 1024
    per_batch = mod.shape[0] == bsz
    final_norm = final_g is not None
    fg = (final_g if final_norm else jnp.ones((D_MODEL,), F32)).reshape(1, D_MODEL)
    return pl.pallas_call(
        functools.partial(_mlp_kernel, final_norm=final_norm),
        out_shape=jax.ShapeDtypeStruct((bsz, length, D_MODEL), F32),
        grid=(bsz, length // tm, D_FF // tf),
        in_specs=[pl.BlockSpec((1, tm, D_MODEL), lambda b, i, f: (b, i, 0)),
                  pl.BlockSpec((1, D_MODEL), lambda b, i, f: (0, 0)),
                  pl.BlockSpec((1, 6, D_MODEL), (lambda b, i, f: (b, 0, 0)) if per_batch
                               else (lambda b, i, f: (0, 0, 0))),
                  pl.BlockSpec((D_MODEL, tf), lambda b, i, f: (0, f)),
                  pl.BlockSpec((tf, D_MODEL), lambda b, i, f: (f, 0)),
                  pl.BlockSpec((1, D_MODEL), lambda b, i, f: (0, 0))],
        out_specs=pl.BlockSpec((1, tm, D_MODEL), lambda b, i, f: (b, i, 0)),
        scratch_shapes=[pltpu.VMEM((tm, D_MODEL), BF16), pltpu.VMEM((tm, D_MODEL), F32)],
        compiler_params=_cparams(3),
        name="mlp",
    )(x, g.reshape(1, D_MODEL), mod, w1_bf, w2_bf, fg)


def _lin_scan(a, b, carry, rev):
    t_len, width = a.shape
    n_groups = t_len // 8
    a = a.reshape(n_groups, 8, width)
    b = b.reshape(n_groups, 8, width)
    sub = lax.broadcasted_iota(jnp.int32, a.shape, 1)
    for s in (1, 2, 4):
        if rev:
            a_sh = pltpu.roll(a, 8 - s, 1)
            b_sh = pltpu.roll(b, 8 - s, 1)
            ok = sub < 8 - s
        else:
            a_sh = pltpu.roll(a, s, 1)
            b_sh = pltpu.roll(b, s, 1)
            ok = sub >= s
        b = a * jnp.where(ok, b_sh, 0.0) + b
        a = a * jnp.where(ok, a_sh, 1.0)
    a = a.reshape(t_len, width)
    b = b.reshape(t_len, width)
    hs = [None] * n_groups
    for g in (range(n_groups - 1, -1, -1) if rev else range(n_groups)):
        h_g = a[8 * g:8 * g + 8, :] * carry + b[8 * g:8 * g + 8, :]
        hs[g] = h_g
        carry = h_g[0:1, :] if rev else h_g[7:8, :]
    return jnp.concatenate(hs, axis=0), carry


def _rglru_kernel(*refs, rev, row_len, final):
    if final:
        (x_ref, gate_ref, cw_ref, cb_ref, wa_ref, ba_ref, wx_ref, bx_ref, lam_ref, h0_ref, prev_ref,
         o_ref, st_ref, carry) = refs
    else:
        (x_ref, cw_ref, cb_ref, wa_ref, ba_ref, wx_ref, bx_ref, lam_ref, h0_ref,
         o_ref, st_ref, carry) = refs
    i = pl.program_id(1)

    @pl.when(i == 0)
    def _():
        carry[...] = h0_ref[0]

    xc = _short_conv(x_ref[0], cw_ref[...], row_len, 2) + cb_ref[...]
    xb = xc.astype(BF16)
    blocks = [slice(n * LANES, (n + 1) * LANES) for n in range(RG_W // LANES)]
    gate_r = jnp.concatenate([jnp.dot(xb[:, bs], wa_ref[n], preferred_element_type=F32)
                              for n, bs in enumerate(blocks)], axis=-1)
    gate_i = jnp.concatenate([jnp.dot(xb[:, bs], wx_ref[n], preferred_element_type=F32)
                              for n, bs in enumerate(blocks)], axis=-1)
    gate_r = _sigmoid(gate_r + ba_ref[...])
    gate_i = _sigmoid(gate_i + bx_ref[...])
    log_a = -RG_C * gate_r * _softplus(-lam_ref[...])
    a = jnp.exp(log_a)
    b = jnp.sqrt(-jnp.tanh(log_a) * (a * a + 1.0)) * (gate_i * xc)
    h, last = _lin_scan(a, b, carry[...], rev)
    carry[...] = last

    @pl.when(i == pl.num_programs(1) - 1)
    def _():
        st_ref[0] = last

    if final:
        gate = gate_ref[0]
        gelu = 0.5 * gate * (1.0 + jnp.tanh(math.sqrt(2.0 / math.pi) * (gate + 0.044715 * gate * gate * gate)))
        o_ref[0] = (gelu * (prev_ref[0] + h)).astype(o_ref.dtype)
    else:
        o_ref[0] = h


def _rglru_pass(p, prm, d, h0, prev, row_len):
    bsz, length, _ = p.shape
    conv_w, conv_b, wa, ba, wx, bx, lam = prm
    final = prev is not None
    rev = d == 1
    t = min(length, 256)
    nt = length // t
    tidx = (lambda i: nt - 1 - i) if rev else (lambda i: i)
    cx = COL_RG // RG_W
    col = lambda c0: pl.BlockSpec((1, t, RG_W), lambda b, i: (b, tidx(i), c0))
    vec = pl.BlockSpec((1, RG_W), lambda b, i: (0, 0))
    mat = pl.BlockSpec((RG_W // LANES, LANES, LANES), lambda b, i: (0, 0, 0))
    in_specs = [col(cx)]
    args = [p]
    if final:
        in_specs.append(col(cx + 1))
        args.append(p)
    in_specs += [pl.BlockSpec((4, RG_W), lambda b, i: (0, 0)), vec, mat, vec, mat, vec, vec,
                 pl.BlockSpec((1, 1, RG_W), lambda b, i: (b, 0, 0))]
    args += [conv_w, conv_b.reshape(1, RG_W), wa[d].astype(BF16), ba[d].reshape(1, RG_W),
             wx[d].astype(BF16), bx[d].reshape(1, RG_W), lam[d].reshape(1, RG_W), h0]
    if final:
        in_specs.append(pl.BlockSpec((1, t, RG_W), lambda b, i: (b, tidx(i), 0)))
        args.append(prev)
    out, st = pl.pallas_call(
        functools.partial(_rglru_kernel, rev=rev, row_len=row_len, final=final),
        out_shape=[jax.ShapeDtypeStruct((bsz, length, RG_W), BF16 if final else F32),
                   jax.ShapeDtypeStruct((bsz, 1, RG_W), F32)],
        grid=(bsz, nt),
        in_specs=in_specs,
        out_specs=[pl.BlockSpec((1, t, RG_W), lambda b, i: (b, tidx(i), 0)),
                   pl.BlockSpec((1, 1, RG_W), lambda b, i: (b, 0, 0))],
        scratch_shapes=[pltpu.VMEM((1, RG_W), F32)],
        compiler_params=_cparams(2),
        name="rglru_bwd" if rev else "rglru_fwd",
    )(*args)
    return out, st


def _rglru_mixer(p, prm, h0s, row_len):
    bsz = p.shape[0]
    if h0s is None:
        h0s = [jnp.zeros((bsz, 1, RG_W), F32)] * 2
    h_f, s_f = _rglru_pass(p, prm, 0, h0s[0], None, row_len)
    out, s_b = _rglru_pass(p, prm, 1, h0s[1], h_f, row_len)
    return out, [s_f, s_b]


def _tri_masks(n, rev):
    row = lax.broadcasted_iota(jnp.int32, (n, n), 0)
    col = lax.broadcasted_iota(jnp.int32, (n, n), 1)
    if rev:
        return row <= col, row < col
    return row >= col, row > col


def _dot_x3(a, b):
    a_hi = a.astype(BF16)
    a_lo = (a - a_hi.astype(F32)).astype(BF16)
    b_hi = b.astype(BF16)
    b_lo = (b - b_hi.astype(F32)).astype(BF16)
    mm = lambda p, q: jnp.dot(p, q, preferred_element_type=F32)
    n = b.shape[1]
    if n % LANES == 0:
        both = mm(a_hi, jnp.concatenate([b_hi, b_lo], axis=-1))
        return both[:, :n] + (both[:, n:] + mm(a_lo, b_hi))
    return mm(a_hi, b_hi) + (mm(a_hi, b_lo) + mm(a_lo, b_hi))


GD_SOLVE_DOT = _dot_x3


def _unit_solve(neg_n, rhs, steps, dot):
    x = rhs
    m = neg_n
    for s in range(steps):
        x = x + dot(m, x)
        if s + 1 < steps:
            m = dot(m, m)
    return x


def _chunk_cumsum(x, rev):
    t_len = x.shape[0]
    pos = lax.broadcasted_iota(jnp.int32, x.shape, 0) & (CHUNK - 1)
    s = 1
    while s < CHUNK:
        if rev:
            x = x + jnp.where(pos < CHUNK - s, pltpu.roll(x, t_len - s, 0), 0.0)
        else:
            x = x + jnp.where(pos >= s, pltpu.roll(x, s, 0), 0.0)
        s *= 2
    return x


def _lane_form(x):
    sel = (lax.broadcasted_iota(jnp.int32, x.shape, 1) == 0).astype(BF16)
    p1 = x.astype(BF16)
    r1 = x - p1.astype(F32)
    p2 = r1.astype(BF16)
    p3 = (r1 - p2.astype(F32)).astype(BF16)
    nt = lambda p: lax.dot_general(sel, p, (((1,), (1,)), ((), ())), preferred_element_type=F32)
    return nt(p1) + nt(p2) + nt(p3)


def _dot_exact_rhs(a, b_exact):
    a_hi = a.astype(BF16)
    a_lo = (a - a_hi.astype(F32)).astype(BF16)
    b = b_exact.astype(BF16)
    return (jnp.dot(a_hi, b, preferred_element_type=F32) + jnp.dot(a_lo, b, preferred_element_type=F32))


def _gdn_kernel(*refs, rev, row_len, final, d):
    if final:
        (q_ref, k_ref, v_ref, z_ref, gb_ref, cq_ref, ck_ref, cv_ref, alog_ref, dtb_ref, ng_ref, s0_ref,
         prev_ref, o_ref, st_ref, state) = refs
    else:
        (q_ref, k_ref, v_ref, gb_ref, cq_ref, ck_ref, cv_ref, alog_ref, dtb_ref, ng_ref, s0_ref,
         o_ref, st_ref, state) = refs
    grp = pl.program_id(1)
    i = pl.program_id(2)
    n_heads = q_ref.shape[2] // GD_HEAD

    @pl.when(i == 0)
    def _():
        state[...] = s0_ref[0]

    qc = _silu(_short_conv(q_ref[0], cq_ref[...], row_len, 2))
    kc = _silu(_short_conv(k_ref[0], ck_ref[...], row_len, 2))
    vc = _silu(_short_conv(v_ref[0], cv_ref[...], row_len, 2))
    gbb = gb_ref[0]
    lane = lax.broadcasted_iota(jnp.int32, gbb.shape, 1)
    t_len = qc.shape[0]
    n_chunks = t_len // CHUNK
    incl, strict = _tri_masks(CHUNK, rev)
    eye = (lax.broadcasted_iota(jnp.int32, (CHUNK, CHUNK), 0)
           == lax.broadcasted_iota(jnp.int32, (CHUNK, CHUNK), 1)).astype(F32)
    order = list(range(n_chunks - 1, -1, -1) if rev else range(n_chunks))
    bf = lambda x: x.astype(BF16)

    heads = []
    for hh in range(n_heads):
        hs = slice(hh * GD_HEAD, (hh + 1) * GD_HEAD)
        head = grp * n_heads + hh
        q_h, k_h = qc[:, hs], kc[:, hs]
        q_h = q_h * lax.rsqrt(jnp.sum(q_h * q_h, axis=-1, keepdims=True) + 1e-6) * (GD_HEAD ** -0.5)
        k_h = k_h * lax.rsqrt(jnp.sum(k_h * k_h, axis=-1, keepdims=True) + 1e-6)
        g_raw = jnp.sum(jnp.where(lane == d * GD_HEADS + head, gbb, 0.0), axis=-1, keepdims=True)
        b_raw = jnp.sum(jnp.where(lane == (2 + d) * GD_HEADS + head, gbb, 0.0), axis=-1, keepdims=True)
        g = -jnp.exp(alog_ref[hh]) * _softplus(g_raw + dtb_ref[hh])
        heads.append(dict(q=q_h, k=k_h, v=vc[:, hs], beta=_sigmoid(b_raw), gc=_chunk_cumsum(g, rev)))

    pre = {}
    s_cur = [state[hh] for hh in range(n_heads)]
    os = {}

    def decay_tiles(c):
        for hh in range(n_heads):
            hd = heads[hh]
            sl = slice(c * CHUNK, (c + 1) * CHUNK)
            q_c, k_c, v_c, b_c, gc = hd["q"][sl], hd["k"][sl], hd["v"][sl], hd["beta"][sl], hd["gc"][sl]
            g_row = _lane_form(gc)
            kb = k_c * b_c
            e_gc = jnp.exp(gc)
            g_last = gc[0:1, :] if rev else gc[CHUNK - 1:CHUNK, :]
            pre[hh, c] = dict(decay_in=jnp.exp(jnp.where(incl, gc[:, :CHUNK] - g_row, -jnp.inf)),
                              q=q_c, k=k_c, kb=kb, x=jnp.concatenate([v_c * b_c, kb * e_gc], axis=-1),
                              qg=bf(q_c * e_gc), k_dec=bf(k_c * jnp.exp(g_last - gc)), dec=jnp.exp(g_last))

    def score_tiles(c):
        for hh in range(n_heads):
            p = pre[hh, c]
            sc = _dot_nt(jnp.concatenate([p["kb"], p["q"]], axis=0), p["k"])
            p["m"] = -(sc[:CHUNK] * jnp.where(strict, p["decay_in"], 0.0))
            p["a_qk"] = bf(sc[CHUNK:] * p["decay_in"])

    def solve_step(c, s):
        for hh in range(n_heads):
            p = pre[hh, c]
            if s == 0:
                p["t"] = eye + p["m"]
                p["m"] = GD_SOLVE_DOT(p["m"], p["m"])
            elif s < 5:
                w = GD_SOLVE_DOT(p["m"], jnp.concatenate([p["t"], p["m"]], axis=-1))
                p["t"] = p["t"] + w[:, :CHUNK]
                p["m"] = w[:, CHUNK:]
            else:
                p["t"] = p["t"] + GD_SOLVE_DOT(p["m"], p["t"])

    def apply_inverse(c):
        for hh in range(n_heads):
            p = pre[hh, c]
            p["x"] = _dot(p["t"], p["x"])

    def recur(c):
        for hh in range(n_heads):
            p = pre.pop((hh, c))
            x = p["x"]
            t2 = _dot(jnp.concatenate([bf(x[:, GD_HEAD:]), p["qg"]], axis=0), s_cur[hh])
            v_new = bf(x[:, :GD_HEAD] - t2[:CHUNK])
            os[hh, c] = t2[CHUNK:] + _dot(p["a_qk"], v_new)
            s_cur[hh] = s_cur[hh] * p["dec"] + _dot_tn(p["k_dec"], v_new)

    phases = ([decay_tiles, score_tiles] + [functools.partial(solve_step, s=s) for s in range(6)]
              + [apply_inverse, recur])
    for slot in range(n_chunks + len(phases) - 1):
        for ph in range(len(phases) - 1, -1, -1):
            if 0 <= slot - ph < n_chunks:
                phases[ph](order[slot - ph])

    for hh in range(n_heads):
        hs = slice(hh * GD_HEAD, (hh + 1) * GD_HEAD)
        o_h = jnp.concatenate([os[hh, c] for c in range(n_chunks)], axis=0)
        if final:
            o_t = prev_ref[0, :, hs] + o_h
            o_t = o_t * lax.rsqrt(jnp.mean(o_t * o_t, axis=-1, keepdims=True) + NORM_EPS) * ng_ref[...]
            o_ref[0, :, hs] = (o_t * _silu(z_ref[0, :, hs])).astype(o_ref.dtype)
        else:
            o_ref[0, :, hs] = o_h
        state[hh] = s_cur[hh]

    @pl.when(i == pl.num_programs(2) - 1)
    def _():
        for hh in range(n_heads):
            st_ref[0, hh] = s_cur[hh]


GD_HEADS_PER_STEP = 2
GD_TILE = 1024


def _gdn_pass(p, prm, d, s0, prev, row_len):
    bsz, length, _ = p.shape
    conv_w, a_log, dt_bias, norm_g = prm
    final = prev is not None
    rev = d == 1
    t = min(length, GD_TILE)
    nt = length // t
    tidx = (lambda i: nt - 1 - i) if rev else (lambda i: i)
    nh = GD_HEADS_PER_STEP
    wd = nh * GD_HEAD
    ng = GD_W // wd
    c0 = COL_GD // wd
    col = lambda off: pl.BlockSpec((1, t, wd), lambda b, h, i: (b, tidx(i), c0 + off * ng + h))
    cw = lambda off: pl.BlockSpec((4, wd), lambda b, h, i: (0, off * ng + h))
    hvec = pl.BlockSpec((nh, 1, LANES), lambda b, h, i: (h, 0, 0))
    in_specs = [col(0), col(1), col(2)]
    args = [p, p, p]
    if final:
        in_specs.append(col(3))
        args.append(p)
    in_specs += [pl.BlockSpec((1, t, LANES), lambda b, h, i: (b, tidx(i), COL_GD_GB // LANES)),
                 cw(0), cw(1), cw(2), hvec, hvec,
                 pl.BlockSpec((1, LANES), lambda b, h, i: (0, 0)),
                 pl.BlockSpec((1, nh, GD_HEAD, GD_HEAD), lambda b, h, i: (b, h, 0, 0))]
    bcast = lambda v: jnp.broadcast_to(v.reshape(GD_HEADS, 1, 1), (GD_HEADS, 1, LANES))
    args += [p, conv_w, conv_w, conv_w, bcast(a_log[d]), bcast(dt_bias[d]), norm_g.reshape(1, GD_HEAD), s0]
    if final:
        in_specs.append(pl.BlockSpec((1, t, wd), lambda b, h, i: (b, tidx(i), h)))
        args.append(prev)
    out, st = pl.pallas_call(
        functools.partial(_gdn_kernel, rev=rev, row_len=row_len, final=final, d=d),
        out_shape=[jax.ShapeDtypeStruct((bsz, length, GD_W), BF16 if final else F32),
                   jax.ShapeDtypeStruct((bsz, GD_HEADS, GD_HEAD, GD_HEAD), F32)],
        grid=(bsz, ng, nt),
        in_specs=in_specs,
        out_specs=[pl.BlockSpec((1, t, wd), lambda b, h, i: (b, tidx(i), h)),
                   pl.BlockSpec((1, nh, GD_HEAD, GD_HEAD), lambda b, h, i: (b, h, 0, 0))],
        scratch_shapes=[pltpu.VMEM((nh, GD_HEAD, GD_HEAD), F32)],
        compiler_params=_cparams(3),
        name="gdn_bwd" if rev else "gdn_fwd",
    )(*args)
    return out, st


def _gdn_mixer(p, prm, s0s, row_len):
    bsz = p.shape[0]
    if s0s is None:
        s0s = [jnp.zeros((bsz, GD_HEADS, GD_HEAD, GD_HEAD), F32)] * 2
    o_f, s_f = _gdn_pass(p, prm, 0, s0s[0], None, row_len)
    out, s_b = _gdn_pass(p, prm, 1, s0s[1], o_f, row_len)
    return out, [s_f, s_b]


def _stack_heads(x, lo):
    return jnp.concatenate([jnp.where(lo, x, 0.0), jnp.where(lo, 0.0, x)], axis=0)


def _rwkv_kernel(*refs, rev, final):
    if final:
        (r_ref, k_ref, v_ref, lo_ref, mur_ref, muk_ref, muv_ref, mul_ref, w0_ref, wup_ref, a0_ref, aup_ref,
         gup_ref, kk_ref, ka_ref, rk_ref, lng_ref, lnb_ref, s0_ref, prev_ref,
         o_ref, st_ref, state, c_r, c_k, c_v, c_l) = refs
    else:
        (r_ref, k_ref, v_ref, lo_ref, mur_ref, muk_ref, muv_ref, mul_ref, w0_ref, wup_ref, a0_ref, aup_ref,
         gup_ref, kk_ref, ka_ref, rk_ref, lng_ref, lnb_ref, s0_ref,
         o_ref, st_ref, state, c_r, c_k, c_v, c_l) = refs
    i = pl.program_id(2)

    @pl.when(i == 0)
    def _():
        state[...] = s0_ref[0]
        c_r[...] = jnp.zeros_like(c_r)
        c_k[...] = jnp.zeros_like(c_k)
        c_v[...] = jnp.zeros_like(c_v)
        c_l[...] = jnp.zeros_like(c_l)

    t_len = r_ref.shape[1]

    def shifted(x_ref, carry, mu_ref):
        x = x_ref[0]
        row = lax.broadcasted_iota(jnp.int32, x.shape, 0)
        if rev:
            prev = jnp.where(row == t_len - 1, carry[...], pltpu.roll(x, t_len - 1, 0))
            carry[...] = x[0:1, :]
        else:
            prev = jnp.where(row == 0, carry[...], pltpu.roll(x, 1, 0))
            carry[...] = x[t_len - 1:t_len, :]
        return x + (prev - x) * mu_ref[...]

    r = shifted(r_ref, c_r, mur_ref)
    k = shifted(k_ref, c_k, muk_ref)
    v = shifted(v_ref, c_v, muv_ref)
    lora = shifted(lo_ref, c_l, mul_ref)
    xw, xa, xg = lora[:, 0:64], lora[:, 64:128], lora[:, 128:256]

    width = r_ref.shape[2]
    n_pairs = width // LANES
    lane = lax.broadcasted_iota(jnp.int32, (1, LANES), 1)
    lo = lane < RW_HEAD
    head_sum = (lax.broadcasted_iota(jnp.int32, (width, width), 0) // RW_HEAD
                == lax.broadcasted_iota(jnp.int32, (width, width), 1) // RW_HEAD).astype(F32)
    rowh = lax.broadcasted_iota(jnp.int32, (LANES, LANES), 0) // RW_HEAD
    colh = lax.broadcasted_iota(jnp.int32, (LANES, LANES), 1) // RW_HEAD

    lw = -math.exp(-0.5) * _sigmoid(w0_ref[...] + _dot(jnp.tanh(xw), wup_ref[...]))
    a = _sigmoid(a0_ref[...] + _dot(xa, aup_ref[...]))
    kk = k * kk_ref[...]
    kappa = kk / jnp.maximum(jnp.sqrt(_dot_exact_rhs(kk * kk, head_sum)), 1e-12)
    kt = k * (1.0 + (a - 1.0) * ka_ref[...])
    gate = _dot(_sigmoid(xg), gup_ref[...])
    bonus = _dot_exact_rhs(r * kt * rk_ref[...], head_sum) * v

    n_chunks = t_len // CHUNK
    lg_all = _chunk_cumsum(lw, rev)
    row_s = lax.broadcasted_iota(jnp.int32, (CHUNK, LANES), 0)
    col_s = lax.broadcasted_iota(jnp.int32, (CHUNK, LANES), 1) & (CHUNK - 1)
    incl_s = (row_s <= col_s) if rev else (row_s >= col_s)
    strict_s = (row_s < col_s) if rev else (row_s > col_s)
    eye_s = (row_s == col_s).astype(F32)
    lo2 = (lax.broadcasted_iota(jnp.int32, (1, 2 * LANES), 1) & (LANES - 1)) < RW_HEAD
    order = list(range(n_chunks - 1, -1, -1) if rev else range(n_chunks))
    bf = lambda x: x.astype(BF16)
    stack = lambda x: bf(_stack_heads(x, lo))

    pre = {}
    s_cur = [state[pp] for pp in range(n_pairs)]
    ys = {}

    def scores(c):
        for pp in range(n_pairs):
            sl = (slice(c * CHUNK, (c + 1) * CHUNK), slice(pp * LANES, (pp + 1) * LANES))
            lw_c, kap_c, a_c = lw[sl], kappa[sl], a[sl]
            lg = lg_all[sl]
            lg_tot = lg[0:1, :] if rev else lg[CHUNK - 1:CHUNK, :]
            e_neg = jnp.exp(-lg)
            e_rem = jnp.exp(lg_tot - lg)
            p_raw = -(kap_c * a_c)
            q_h = kap_c * jnp.exp(lg - lw_c)
            r_h = bf(r[sl] * jnp.exp(lg))
            sc = _dot_nt(jnp.concatenate([bf(q_h), r_h], axis=0),
                         jnp.concatenate([stack(p_raw * e_neg), stack(kt[sl] * e_neg)], axis=0))
            pre[pp, c] = dict(
                q=q_h, r=r_h, v2=stack(v[sl]), pt2=stack(p_raw * e_rem), kt2=stack(kt[sl] * e_rem),
                m=jnp.where(strict_s, sc[:CHUNK, :LANES], 0.0),
                s_qk=jnp.where(strict_s, sc[:CHUNK, LANES:], 0.0),
                s_rp=bf(jnp.where(incl_s, sc[CHUNK:, :LANES], 0.0)),
                s_rk=jnp.where(incl_s, sc[CHUNK:, LANES:], 0.0),
                dec=jnp.broadcast_to(jnp.exp(lg_tot), (LANES, LANES)).T)

    def local_terms(c):
        for pp in range(n_pairs):
            p = pre[pp, c]
            loc = _dot(jnp.concatenate([p["s_qk"], p["s_rk"]], axis=0), p["v2"])
            p["rhs"] = bf(_stack_heads(jnp.concatenate([p["q"], loc[:CHUNK]], axis=-1), lo2))
            p["y_loc"] = loc[CHUNK:]
            p["s_loc"] = _dot_tn(p["kt2"], p["v2"])

    def solve_step(c, s):
        for pp in range(n_pairs):
            p = pre[pp, c]
            if s == 0:
                p["t"] = eye_s + p["m"]
                p["m"] = _dot(p["m"], stack(p["m"]))
            elif s < 5:
                w = _dot(p["m"], jnp.concatenate([stack(p["t"]), stack(p["m"])], axis=-1))
                p["t"] = p["t"] + w[:, :LANES]
                p["m"] = w[:, LANES:]
            else:
                p["t"] = p["t"] + _dot(p["m"], stack(p["t"]))

    def apply_inverse(c):
        for pp in range(n_pairs):
            p = pre[pp, c]
            p["x"] = _dot(p["t"], p["rhs"])

    def recur(c):
        for pp in range(n_pairs):
            p = pre.pop((pp, c))
            x = p["x"]
            t2 = _dot(jnp.concatenate([bf(x[:, :LANES]), p["r"]], axis=0), s_cur[pp])
            u2 = stack(t2[:CHUNK] + x[:, LANES:])
            ys[pp, c] = t2[CHUNK:] + _dot(p["s_rp"], u2) + p["y_loc"]
            s_cur[pp] = s_cur[pp] * p["dec"] + _dot_tn(p["pt2"], u2) + p["s_loc"]

    phases = ([scores, local_terms] + [functools.partial(solve_step, s=s) for s in range(6)]
              + [apply_inverse, recur])
    for slot in range(n_chunks + len(phases) - 1):
        for ph in range(len(phases) - 1, -1, -1):
            if 0 <= slot - ph < n_chunks:
                phases[ph](order[slot - ph])

    y = jnp.concatenate([jnp.concatenate([ys[pp, c] for c in range(n_chunks)], axis=0)
                         for pp in range(n_pairs)], axis=1)
    mean = _dot_exact_rhs(y, head_sum) * (1.0 / RW_HEAD)
    yc = y - mean
    var = _dot_exact_rhs(yc * yc, head_sum) * (1.0 / RW_HEAD)
    yn = yc * lax.rsqrt(var + RW_GN_EPS) * lng_ref[...] + lnb_ref[...]
    out = (yn + bonus) * gate
    if final:
        o_ref[0] = (prev_ref[0] + out).astype(o_ref.dtype)
    else:
        o_ref[0] = out
    for pp in range(n_pairs):
        state[pp] = s_cur[pp]

    @pl.when(i == pl.num_programs(2) - 1)
    def _():
        for pp in range(n_pairs):
            st_ref[0, pp] = s_cur[pp]


RW_PAIRS_PER_STEP = 2
RW_TILE = 1024


def _rwkv_pass(p, prm, d, s0, prev):
    bsz, length, _ = p.shape
    mu, w0, w_up, a0, a_up, g_up, k_k, k_a, r_k, ln_g, ln_b = [t[d] for t in prm]
    final = prev is not None
    rev = d == 1
    t = min(length, RW_TILE)
    nt = length // t
    tidx = (lambda i: nt - 1 - i) if rev else (lambda i: i)
    npp = RW_PAIRS_PER_STEP
    wd = npp * LANES
    c0 = COL_RW // wd
    ng = RW_W // wd
    col = lambda off: pl.BlockSpec((1, t, wd), lambda b, n, i: (b, tidx(i), c0 + off * ng + n))
    vec = pl.BlockSpec((1, wd), lambda b, n, i: (0, n))
    whole = lambda shape: pl.BlockSpec(shape, lambda b, n, i: (0,) * len(shape))
    row = lambda x: x.reshape(1, -1)
    in_specs = [col(0), col(1), col(2),
                pl.BlockSpec((1, t, 256), lambda b, n, i: (b, tidx(i), COL_RW_LORA // 256)),
                vec, vec, vec, whole((1, 256)),
                vec, pl.BlockSpec((64, wd), lambda b, n, i: (0, n)),
                vec, pl.BlockSpec((64, wd), lambda b, n, i: (0, n)),
                pl.BlockSpec((LANES, wd), lambda b, n, i: (0, n)),
                vec, vec, vec, vec, vec,
                pl.BlockSpec((1, npp, LANES, LANES), lambda b, n, i: (b, n, 0, 0))]
    args = [p, p, p, p,
            row(mu[0:512]), row(mu[512:1024]), row(mu[1024:1536]), row(mu[1536:1792]),
            row(w0), w_up.astype(BF16), row(a0), a_up.astype(BF16), g_up.astype(BF16),
            row(k_k), row(k_a), row(r_k), row(ln_g), row(ln_b), s0]
    if final:
        in_specs.append(pl.BlockSpec((1, t, wd), lambda b, n, i: (b, tidx(i), n)))
        args.append(prev)
    out, st = pl.pallas_call(
        functools.partial(_rwkv_kernel, rev=rev, final=final),
        out_shape=[jax.ShapeDtypeStruct((bsz, length, RW_W), BF16 if final else F32),
                   jax.ShapeDtypeStruct((bsz, 4, LANES, LANES), F32)],
        grid=(bsz, ng, nt),
        in_specs=in_specs,
        out_specs=[pl.BlockSpec((1, t, wd), lambda b, n, i: (b, tidx(i), n)),
                   pl.BlockSpec((1, npp, LANES, LANES), lambda b, n, i: (b, n, 0, 0))],
        scratch_shapes=[pltpu.VMEM((npp, LANES, LANES), F32), pltpu.VMEM((1, wd), F32),
                        pltpu.VMEM((1, wd), F32), pltpu.VMEM((1, wd), F32),
                        pltpu.VMEM((1, 256), F32)],
        compiler_params=_cparams(3),
        name="rwkv_bwd" if rev else "rwkv_fwd",
    )(*args)
    return out, st


def _rwkv_mixer(p, prm, s0s):
    bsz = p.shape[0]
    if s0s is None:
        s0s = [jnp.zeros((bsz, 4, LANES, LANES), F32)] * 2
    o_f, s_f = _rwkv_pass(p, prm, 0, s0s[0], None)
    out, s_b = _rwkv_pass(p, prm, 1, s0s[1], o_f)
    return out, [s_f, s_b]


def _dft_table_kernel(ca_ref, sa_ref, cb_ref, sb_ref, c_ref, s_ref):
    cb, sb = cb_ref[...], sb_ref[...]
    for j in range(ca_ref.shape[1]):
        ca, sa = ca_ref[:, j:j + 1], sa_ref[:, j:j + 1]
        c_ref[:, j * LANES:(j + 1) * LANES] = (ca * cb - sa * sb).astype(c_ref.dtype)
        s_ref[:, j * LANES:(j + 1) * LANES] = (-(sa * cb + ca * sb)).astype(s_ref.dtype)


def _dft_tables(length):
    n = 2 * length
    nfp = -(-(length + 1) // LANES) * LANES
    kf = jnp.arange(nfp, dtype=jnp.int32)[:, None]
    s1 = jnp.arange(n // LANES, dtype=jnp.int32)[None, :]
    s0 = jnp.arange(LANES, dtype=jnp.int32)[None, :]
    ang_a = (2.0 * math.pi / n) * ((kf * s1 * LANES) % n).astype(F32)
    ang_b = (2.0 * math.pi / n) * ((kf * s0) % n).astype(F32)
    ok = (kf <= length).astype(F32)
    tm = 384 if nfp % 384 == 0 else LANES
    rows = lambda w: pl.BlockSpec((tm, w), lambda i: (i, 0))
    tab_c, tab_s = pl.pallas_call(
        _dft_table_kernel,
        out_shape=[jax.ShapeDtypeStruct((nfp, n), BF16)] * 2,
        grid=(nfp // tm,),
        in_specs=[rows(n // LANES), rows(n // LANES), rows(LANES), rows(LANES)],
        out_specs=[rows(n), rows(n)],
        compiler_params=_cparams(1),
        name="hyena_dft_tables",
    )(jnp.cos(ang_a) * ok, jnp.sin(ang_a) * ok, jnp.cos(ang_b), jnp.sin(ang_b))
    kk = kf[:, 0]
    wk = jnp.where((kk == 0) | (kk == length), 1.0, 2.0) * (kk <= length) / n
    return tab_c, tab_s, jnp.broadcast_to(wk[:, None], (nfp, LANES)).astype(F32)


def _hyena_filter(length, w1, b1, w2, b2, w3, freq):
    t = jnp.arange(length, dtype=F32)
    z = t / max(length - 1, 1)
    bands = jnp.linspace(1e-4, HY_BANDS - 1, HY_BANDS, dtype=F32)
    ang = (2.0 * math.pi / length) * t[:, None] * bands[None, :]
    feat = jnp.concatenate([z[:, None], jnp.cos(ang), -jnp.sin(ang)], axis=-1)
    h = jnp.sin(freq[0] * (feat @ w1 + b1))
    h = jnp.sin(freq[1] * (h @ w2 + b2))
    h = (h @ w3).astype(F32)
    deltas = jnp.abs(jnp.linspace(math.log(1e-2) / 1.5, math.log(1e-2) / 0.3, HY_W, dtype=F32))
    h = h * jnp.exp(-z[:, None] * jnp.tile(deltas, 2)[None, :])
    h_fwd = h[:, :HY_W]
    h_bwd = jnp.where(t[:, None] > 0, h[:, HY_W:], 0.0)
    norm = jnp.sum(jnp.abs(h_fwd) + jnp.abs(h_bwd), axis=0, keepdims=True)
    return (h_fwd + h_bwd) / norm, (h_fwd - h_bwd) / norm


def _hy_zin_kernel(v_ref, x1_ref, cw_ref, o_ref, *, row_len):
    vc = _short_conv(v_ref[0], cw_ref[:, 0:512], row_len, 1)
    x1c = _short_conv(x1_ref[0], cw_ref[:, 1024:1536], row_len, 1)
    o_ref[0] = (x1c * vc).astype(o_ref.dtype)


def _hy_zin(p, conv_w, row_len):
    bsz, length, _ = p.shape
    t = min(length, 256)
    return pl.pallas_call(
        functools.partial(_hy_zin_kernel, row_len=row_len),
        out_shape=jax.ShapeDtypeStruct((bsz, length, HY_W), BF16),
        grid=(bsz, length // t),
        in_specs=[pl.BlockSpec((1, t, 512), lambda b, i: (b, i, 0)),
                  pl.BlockSpec((1, t, 512), lambda b, i: (b, i, 2)),
                  pl.BlockSpec((3, 1536), lambda b, i: (0, 0))],
        out_specs=pl.BlockSpec((1, t, 512), lambda b, i: (b, i, 0)),
        compiler_params=_cparams(2),
        name="hyena_zin",
    )(p, p, conv_w)


def _dft_fwd_kernel(*refs, mult):
    if mult:
        c_ref, s_ref, z_ref, fr_ref, fi_ref, yr_ref, yi_ref = refs
        z_cos = z_sin = z_ref[0].astype(BF16)
    else:
        c_ref, s_ref, zc_ref, zs_ref, wk_ref, yr_ref, yi_ref = refs
        z_cos, z_sin = zc_ref[0].astype(BF16), zs_ref[0].astype(BF16)
    zr = jnp.dot(c_ref[...], z_cos, preferred_element_type=F32)
    zi = jnp.dot(s_ref[...], z_sin, preferred_element_type=F32)
    if mult:
        fr, fi = fr_ref[...], fi_ref[...]
        zr, zi = zr * fr - zi * fi, zr * fi + zi * fr
    else:
        zr, zi = zr * wk_ref[:, 0:1], zi * wk_ref[:, 0:1]
    yr_ref[0] = zr.astype(yr_ref.dtype)
    yi_ref[0] = zi.astype(yi_ref.dtype)


def _dft_fwd(tab_c, tab_s, z, spec, wk=None):
    mult = spec is not None
    zs = (z,) if mult else z
    bsz, klen, _ = zs[0].shape
    nfp = tab_c.shape[0]
    tm = 384 if nfp % 384 == 0 else LANES
    in_specs = [pl.BlockSpec((tm, klen), lambda b, i: (i, 0)),
                pl.BlockSpec((tm, klen), lambda b, i: (i, 0))]
    in_specs += [pl.BlockSpec((1, klen, 512), lambda b, i: (b, 0, 0))] * len(zs)
    args = [tab_c, tab_s, *zs]
    if mult:
        in_specs += [pl.BlockSpec((tm, 512), lambda b, i: (i, 0))] * 2
        args += list(spec)
    else:
        in_specs.append(pl.BlockSpec((tm, LANES), lambda b, i: (i, 0)))
        args.append(wk)
    odt = BF16 if mult else F32
    return pl.pallas_call(
        functools.partial(_dft_fwd_kernel, mult=mult),
        out_shape=[jax.ShapeDtypeStruct((bsz, nfp, 512), odt)] * 2,
        grid=(bsz, nfp // tm),
        in_specs=in_specs,
        out_specs=[pl.BlockSpec((1, tm, 512), lambda b, i: (b, i, 0))] * 2,
        compiler_params=_cparams(2),
        name="hyena_dft_mul" if mult else "hyena_dft_filter",
    )(*args)


def _dft_inv_kernel(ci_ref, si_ref, yr_ref, yi_ref, v_ref, x0_ref, x1_ref, cw_ref, skip_ref, o_ref, *, row_len):
    y = jnp.dot(ci_ref[...], yr_ref[0], preferred_element_type=F32)
    y += jnp.dot(si_ref[...], yi_ref[0], preferred_element_type=F32)
    vc = _short_conv(v_ref[0], cw_ref[:, 0:512], row_len, 1)
    x0c = _short_conv(x0_ref[0], cw_ref[:, 512:1024], row_len, 1)
    x1c = _short_conv(x1_ref[0], cw_ref[:, 1024:1536], row_len, 1)
    zin = x1c * vc
    o_ref[0] = (x0c * (y + zin * skip_ref[...])).astype(o_ref.dtype)


def _dft_inv(tab_c, tab_s, yr, yi, p, conv_w, skip, row_len):
    bsz, length, _ = p.shape
    nfp = tab_c.shape[0]
    t = min(length, 256)
    pcol = lambda c: pl.BlockSpec((1, t, 512), lambda b, i: (b, i, c))
    return pl.pallas_call(
        functools.partial(_dft_inv_kernel, row_len=row_len),
        out_shape=jax.ShapeDtypeStruct((bsz, length, HY_W), BF16),
        grid=(bsz, length // t),
        in_specs=[pl.BlockSpec((t, nfp), lambda b, i: (i, 0)),
                  pl.BlockSpec((t, nfp), lambda b, i: (i, 0)),
                  pl.BlockSpec((1, nfp, 512), lambda b, i: (b, 0, 0)),
                  pl.BlockSpec((1, nfp, 512), lambda b, i: (b, 0, 0)),
                  pcol(0), pcol(1), pcol(2),
                  pl.BlockSpec((3, 1536), lambda b, i: (0, 0)),
                  pl.BlockSpec((1, 512), lambda b, i: (0, 0))],
        out_specs=pl.BlockSpec((1, t, 512), lambda b, i: (b, i, 0)),
        compiler_params=_cparams(2),
        name="hyena_idft_gate",
    )(tab_c, tab_s, yr, yi, p, p, p, conv_w, skip.reshape(1, HY_W))


def _hyena_mixer(p, prm, tables, row_len):
    conv_w, w1, b1, w2, b2, w3, freq, skip = prm
    length = p.shape[1]
    tab_c, tab_s, wk = tables
    f_cos, f_sin = _hyena_filter(length, w1, b1, w2, b2, w3, freq)
    spec = _dft_fwd(tab_c, tab_s, (f_cos[None], f_sin[None]), None, wk)
    spec = (spec[0][0], spec[1][0])
    zin = _hy_zin(p, conv_w, row_len)
    yr, yi = _dft_fwd(tab_c, tab_s, zin, spec)
    return _dft_inv(tab_c, tab_s, yr, yi, p, conv_w, skip, row_len)


def _permute_w_in(w_in):
    w_in = w_in.astype(BF16)
    hy = w_in[:, 0:1536]
    rw = w_in[:, 1536:3328]
    gd = w_in[:, 3328:5392]
    rg = w_in[:, 5392:6416]
    pad = jnp.zeros((w_in.shape[0], N_PROJ - 6416), BF16)
    return jnp.concatenate([hy, rw[:, :1536], gd[:, :2048], rg, rw[:, 1536:], gd[:, 2048:], pad], axis=-1)


def kernel(x, c, ctx, c_ctx, ada_w, ada_b, norm_mix_g, norm_mlp_g, w_in, w_out,
           hy_conv, hy_w1, hy_b1, hy_w2, hy_b2, hy_w3, hy_freq, hy_skip,
           rw_mu, rw_w0, rw_w_up, rw_a0, rw_a_up, rw_g_up, rw_k_k, rw_k_a, rw_r_k, rw_ln_g, rw_ln_b,
           gd_conv, gd_a_log, gd_dt_bias, gd_norm_g,
           rg_conv, rg_conv_b, rg_wa, rg_ba, rg_wx, rg_bx, rg_lambda,
           mlp_w1, mlp_w2, final_norm_g):
    bsz, seq, _ = x.shape
    ctx_len = ctx.shape[1]
    depth = ada_w.shape[0]
    tables_x = _dft_tables(seq)
    tables_c = _dft_tables(ctx_len)
    cond8 = jnp.concatenate([c, c_ctx[None, :], jnp.zeros((8 - bsz - 1, D_MODEL), F32)], axis=0)
    for l in range(depth):
        last = l == depth - 1
        mod = _modulation(cond8, ada_w, ada_b[l], l)
        mod_x = mod[:bsz].reshape(bsz, 6, D_MODEL)
        mod_c = mod[bsz:bsz + 1].reshape(1, 6, D_MODEL)
        w_in_bf = _permute_w_in(w_in[l])
        w_out_bf = w_out[l].astype(BF16)
        w1_bf = mlp_w1[l].astype(BF16)
        w2_bf = mlp_w2[l].astype(BF16)
        px = _inproj(x, norm_mix_g[l], mod_x, w_in_bf)
        pc = _inproj(ctx, norm_mix_g[l], mod_c, w_in_bf)
        hy_prm = (hy_conv[l], hy_w1[l], hy_b1[l], hy_w2[l], hy_b2[l], hy_w3[l], hy_freq[l], hy_skip[l])
        rw_prm = (rw_mu[l], rw_w0[l], rw_w_up[l], rw_a0[l], rw_a_up[l], rw_g_up[l],
                  rw_k_k[l], rw_k_a[l], rw_r_k[l], rw_ln_g[l], rw_ln_b[l])
        gd_prm = (gd_conv[l], gd_a_log[l], gd_dt_bias[l], gd_norm_g[l])
        rg_prm = (rg_conv[l], rg_conv_b[l], rg_wa[l], rg_ba[l], rg_wx[l], rg_bx[l], rg_lambda[l])
        c_rw, s_rw = _rwkv_mixer(pc, rw_prm, None)
        c_gd, s_gd = _gdn_mixer(pc, gd_prm, None, ctx_len)
        c_rg, s_rg = _rglru_mixer(pc, rg_prm, None, ctx_len)
        x_hy = _hyena_mixer(px, hy_prm, tables_x, GRID_W)
        x_rw, _ = _rwkv_mixer(px, rw_prm, s_rw)
        x_gd, _ = _gdn_mixer(px, gd_prm, s_gd, GRID_W)
        x_rg, _ = _rglru_mixer(px, rg_prm, s_rg, GRID_W)
        x_new = _outproj(x, mod_x, (x_hy, x_rw, x_gd, x_rg), w_out_bf)
        x_new = _mlp(x_new, norm_mlp_g[l], mod_x, w1_bf, w2_bf, final_norm_g if last else None)
        if not last:
            c_hy = _hyena_mixer(pc, hy_prm, tables_c, ctx_len)
            ctx_new = _outproj(ctx, mod_c, (c_hy, c_rw, c_gd, c_rg), w_out_bf)
            ctx = _mlp(ctx_new, norm_mlp_g[l], mod_c, w1_bf, w2_bf, None)
        x = x_new
    return x
```

```python
import functools
import math

import jax
import jax.numpy as jnp
from jax import lax
from jax.experimental import pallas as pl
from jax.experimental.pallas import tpu as pltpu

F32 = jnp.float32
BF16 = jnp.bfloat16
HIGHEST = lax.Precision.HIGHEST

D_MODEL = 2048
GRID_W = 64
HY_W = RW_W = GD_W = RG_W = 512
D_FF = 4 * D_MODEL
NORM_EPS = 1e-6
HY_EMB = 33
HY_BANDS = 16
RW_HEAD = 64
RW_GN_EPS = 64e-5
GD_HEAD = 128
GD_HEADS = 4
CHUNK = 64
RG_C = 8.0
LANES = 128

COL_HY = 0
COL_RW = 1536
COL_GD = 3072
COL_RG = 5120
COL_RW_LORA = 6144
COL_GD_GB = 6400
N_PROJ = 6656

VMEM_LIMIT = 56 * 1024 * 1024


def _cparams(n_axes):
    return pltpu.CompilerParams(dimension_semantics=("arbitrary",) * n_axes,
                                vmem_limit_bytes=VMEM_LIMIT)


def _dot(a, b):
    return jnp.dot(a.astype(BF16), b.astype(BF16), preferred_element_type=F32)


def _dot_nt(a, b):
    return lax.dot_general(a.astype(BF16), b.astype(BF16), (((1,), (1,)), ((), ())),
                           preferred_element_type=F32)


def _dot_tn(a, b):
    return lax.dot_general(a.astype(BF16), b.astype(BF16), (((0,), (0,)), ((), ())),
                           preferred_element_type=F32)


def _dot_hp(a, b):
    return jnp.dot(a, b, precision=HIGHEST, preferred_element_type=F32)


def _sigmoid(x):
    return 1.0 / (1.0 + jnp.exp(-x))


def _silu(x):
    return x * _sigmoid(x)


def _softplus(x):
    return jnp.maximum(x, 0.0) + jnp.log1p(jnp.exp(-jnp.abs(x)))


def _short_conv(u, w, row_len, pad_left):
    t_len = u.shape[0]
    pos = lax.broadcasted_iota(jnp.int32, u.shape, 0) & (row_len - 1)
    y = None
    for j in range(w.shape[0]):
        off = j - pad_left
        if off == 0:
            term = u * w[j:j + 1, :]
        else:
            sh = pltpu.roll(u, (-off) % t_len, 0)
            ok = (pos + off >= 0) & (pos + off < row_len)
            term = jnp.where(ok, sh, 0.0) * w[j:j + 1, :]
        y = term if y is None else y + term
    return y


def _mod_kernel(c_ref, w_ref, b_ref, o_ref):
    o_ref[...] = _dot(_silu(c_ref[...]), w_ref[0]) + b_ref[...]


def _modulation(cond8, ada_w, ada_b, layer):
    n = ada_w.shape[2]
    tn = 512
    return pl.pallas_call(
        _mod_kernel,
        out_shape=jax.ShapeDtypeStruct((8, n), F32),
        grid=(n // tn,),
        in_specs=[pl.BlockSpec((8, D_MODEL), lambda j: (0, 0)),
                  pl.BlockSpec((1, D_MODEL, tn), lambda j: (layer, 0, j)),
                  pl.BlockSpec((1, tn), lambda j: (0, j))],
        out_specs=pl.BlockSpec((8, tn), lambda j: (0, j)),
        compiler_params=_cparams(1),
        name="adaln_mod",
    )(cond8, ada_w, ada_b.reshape(1, n))


def _norm_mod(x, g, shift, scale):
    y = x * lax.rsqrt(jnp.mean(x * x, axis=-1, keepdims=True) + NORM_EPS) * g
    return y * (1.0 + scale) + shift


def _inproj_kernel(x_ref, g_ref, mod_ref, w_ref, o_ref, h_scr):
    @pl.when(pl.program_id(2) == 0)
    def _():
        h = _norm_mod(x_ref[0], g_ref[...], mod_ref[0, 0:1, :], mod_ref[0, 1:2, :])
        h_scr[...] = h.astype(BF16)

    o_ref[0] = jnp.dot(h_scr[...], w_ref[...], preferred_element_type=F32)


def _inproj(x, g, mod, w_bf):
    bsz, length, _ = x.shape
    tm = min(length, 1024)
    tn = 512
    per_batch = mod.shape[0] == bsz
    return pl.pallas_call(
        _inproj_kernel,
        out_shape=jax.ShapeDtypeStruct((bsz, length, N_PROJ), F32),
        grid=(bsz, length // tm, N_PROJ // tn),
        in_specs=[pl.BlockSpec((1, tm, D_MODEL), lambda b, i, j: (b, i, 0)),
                  pl.BlockSpec((1, D_MODEL), lambda b, i, j: (0, 0)),
                  pl.BlockSpec((1, 6, D_MODEL), (lambda b, i, j: (b, 0, 0)) if per_batch
                               else (lambda b, i, j: (0, 0, 0))),
                  pl.BlockSpec((D_MODEL, tn), lambda b, i, j: (0, j))],
        out_specs=pl.BlockSpec((1, tm, tn), lambda b, i, j: (b, i, j)),
        scratch_shapes=[pltpu.VMEM((tm, D_MODEL), BF16)],
        compiler_params=_cparams(3),
        name="inproj",
    )(x, g.reshape(1, D_MODEL), mod, w_bf)


def _outproj_kernel(x_ref, mod_ref, m0_ref, m1_ref, m2_ref, m3_ref, w_ref, o_ref):
    acc = jnp.dot(m0_ref[0], w_ref[0:512, :], preferred_element_type=F32)
    acc += jnp.dot(m1_ref[0], w_ref[512:1024, :], preferred_element_type=F32)
    acc += jnp.dot(m2_ref[0], w_ref[1024:1536, :], preferred_element_type=F32)
    acc += jnp.dot(m3_ref[0], w_ref[1536:2048, :], preferred_element_type=F32)
    o_ref[0] = x_ref[0] + mod_ref[0, 2:3, :] * acc


def _outproj(x, mod, mixers, w_bf):
    bsz, length, _ = x.shape
    tm = min(length, 512)
    per_batch = mod.shape[0] == bsz
    mspec = pl.BlockSpec((1, tm, 512), lambda b, i: (b, i, 0))
    return pl.pallas_call(
        _outproj_kernel,
        out_shape=jax.ShapeDtypeStruct((bsz, length, D_MODEL), F32),
        grid=(bsz, length // tm),
        in_specs=[pl.BlockSpec((1, tm, D_MODEL), lambda b, i: (b, i, 0)),
                  pl.BlockSpec((1, 6, D_MODEL), (lambda b, i: (b, 0, 0)) if per_batch
                               else (lambda b, i: (0, 0, 0))),
                  mspec, mspec, mspec, mspec,
                  pl.BlockSpec((D_MODEL, D_MODEL), lambda b, i: (0, 0))],
        out_specs=pl.BlockSpec((1, tm, D_MODEL), lambda b, i: (b, i, 0)),
        compiler_params=_cparams(2),
        name="outproj",
    )(x, mod, *mixers, w_bf)


def _mlp_kernel(x_ref, g_ref, mod_ref, w1_ref, w2_ref, fg_ref, o_ref, h_scr, acc_scr, *, final_norm):
    f = pl.program_id(2)

    @pl.when(f == 0)
    def _():
        h = _norm_mod(x_ref[0], g_ref[...], mod_ref[0, 3:4, :], mod_ref[0, 4:5, :])
        h_scr[...] = h.astype(BF16)
        acc_scr[...] = jnp.zeros_like(acc_scr)

    a = jnp.dot(h_scr[...], w1_ref[...], preferred_element_type=F32)
    a = jnp.square(jnp.maximum(a, 0.0)).astype(BF16)
    acc_scr[...] += jnp.dot(a, w2_ref[...], preferred_element_type=F32)

    @pl.when(f == pl.num_programs(2) - 1)
    def _():
        y = x_ref[0] + mod_ref[0, 5:6, :] * acc_scr[...]
        if final_norm:
            y = y * lax.rsqrt(jnp.mean(y * y, axis=-1, keepdims=True) + NORM_EPS) * fg_ref[...]
        o_ref[0] = y


def _mlp(x, g, mod, w1_bf, w2_bf, final_g):
    bsz, length, _ = x.shape
    tm = min(length, 512)
    tf = 1024
    per_batch = mod.shape[0] == bsz
    final_norm = final_g is not None
    fg = (final_g if final_norm else jnp.ones((D_MODEL,), F32)).reshape(1, D_MODEL)
    return pl.pallas_call(
        functools.partial(_mlp_kernel, final_norm=final_norm),
        out_shape=jax.ShapeDtypeStruct((bsz, length, D_MODEL), F32),
        grid=(bsz, length // tm, D_FF // tf),
        in_specs=[pl.BlockSpec((1, tm, D_MODEL), lambda b, i, f: (b, i, 0)),
                  pl.BlockSpec((1, D_MODEL), lambda b, i, f: (0, 0)),
                  pl.BlockSpec((1, 6, D_MODEL), (lambda b, i, f: (b, 0, 0)) if per_batch
                               else (lambda b, i, f: (0, 0, 0))),
                  pl.BlockSpec((D_MODEL, tf), lambda b, i, f: (0, f)),
                  pl.BlockSpec((tf, D_MODEL), lambda b, i, f: (f, 0)),
                  pl.BlockSpec((1, D_MODEL), lambda b, i, f: (0, 0))],
        out_specs=pl.BlockSpec((1, tm, D_MODEL), lambda b, i, f: (b, i, 0)),
        scratch_shapes=[pltpu.VMEM((tm, D_MODEL), BF16), pltpu.VMEM((tm, D_MODEL), F32)],
        compiler_params=_cparams(3),
        name="mlp",
    )(x, g.reshape(1, D_MODEL), mod, w1_bf, w2_bf, fg)


def _lin_scan(a, b, carry, rev):
    t_len, width = a.shape
    n_groups = t_len // 8
    a = a.reshape(n_groups, 8, width)
    b = b.reshape(n_groups, 8, width)
    sub = lax.broadcasted_iota(jnp.int32, a.shape, 1)
    for s in (1, 2, 4):
        if rev:
            a_sh = pltpu.roll(a, 8 - s, 1)
            b_sh = pltpu.roll(b, 8 - s, 1)
            ok = sub < 8 - s
        else:
            a_sh = pltpu.roll(a, s, 1)
            b_sh = pltpu.roll(b, s, 1)
            ok = sub >= s
        b = a * jnp.where(ok, b_sh, 0.0) + b
        a = a * jnp.where(ok, a_sh, 1.0)
    a = a.reshape(t_len, width)
    b = b.reshape(t_len, width)
    hs = [None] * n_groups
    for g in (range(n_groups - 1, -1, -1) if rev else range(n_groups)):
        h_g = a[8 * g:8 * g + 8, :] * carry + b[8 * g:8 * g + 8, :]
        hs[g] = h_g
        carry = h_g[0:1, :] if rev else h_g[7:8, :]
    return jnp.concatenate(hs, axis=0), carry


def _rglru_kernel(*refs, rev, row_len, final):
    if final:
        (x_ref, gate_ref, cw_ref, cb_ref, wa_ref, ba_ref, wx_ref, bx_ref, lam_ref, h0_ref, prev_ref,
         o_ref, st_ref, carry) = refs
    else:
        (x_ref, cw_ref, cb_ref, wa_ref, ba_ref, wx_ref, bx_ref, lam_ref, h0_ref,
         o_ref, st_ref, carry) = refs
    i = pl.program_id(1)

    @pl.when(i == 0)
    def _():
        carry[...] = h0_ref[0]

    xc = _short_conv(x_ref[0], cw_ref[...], row_len, 2) + cb_ref[...]
    xb = xc.astype(BF16)
    blocks = [slice(n * LANES, (n + 1) * LANES) for n in range(RG_W // LANES)]
    gate_r = jnp.concatenate([jnp.dot(xb[:, bs], wa_ref[n], preferred_element_type=F32)
                              for n, bs in enumerate(blocks)], axis=-1)
    gate_i = jnp.concatenate([jnp.dot(xb[:, bs], wx_ref[n], preferred_element_type=F32)
                              for n, bs in enumerate(blocks)], axis=-1)
    gate_r = _sigmoid(gate_r + ba_ref[...])
    gate_i = _sigmoid(gate_i + bx_ref[...])
    log_a = -RG_C * gate_r * _softplus(-lam_ref[...])
    a = jnp.exp(log_a)
    b = jnp.sqrt(-jnp.tanh(log_a) * (a * a + 1.0)) * (gate_i * xc)
    h, last = _lin_scan(a, b, carry[...], rev)
    carry[...] = last

    @pl.when(i == pl.num_programs(1) - 1)
    def _():
        st_ref[0] = last

    if final:
        gate = gate_ref[0]
        gelu = 0.5 * gate * (1.0 + jnp.tanh(math.sqrt(2.0 / math.pi) * (gate + 0.044715 * gate * gate * gate)))
        o_ref[0] = (gelu * (prev_ref[0] + h)).astype(o_ref.dtype)
    else:
        o_ref[0] = h


def _rglru_pass(p, prm, d, h0, prev, row_len):
    bsz, length, _ = p.shape
    conv_w, conv_b, wa, ba, wx, bx, lam = prm
    final = prev is not None
    rev = d == 1
    t = min(length, 256)
    nt = length // t
    tidx = (lambda i: nt - 1 - i) if rev else (lambda i: i)
    cx = COL_RG // RG_W
    col = lambda c0: pl.BlockSpec((1, t, RG_W), lambda b, i: (b, tidx(i), c0))
    vec = pl.BlockSpec((1, RG_W), lambda b, i: (0, 0))
    mat = pl.BlockSpec((RG_W // LANES, LANES, LANES), lambda b, i: (0, 0, 0))
    in_specs = [col(cx)]
    args = [p]
    if final:
        in_specs.append(col(cx + 1))
        args.append(p)
    in_specs += [pl.BlockSpec((4, RG_W), lambda b, i: (0, 0)), vec, mat, vec, mat, vec, vec,
                 pl.BlockSpec((1, 1, RG_W), lambda b, i: (b, 0, 0))]
    args += [conv_w, conv_b.reshape(1, RG_W), wa[d].astype(BF16), ba[d].reshape(1, RG_W),
             wx[d].astype(BF16), bx[d].reshape(1, RG_W), lam[d].reshape(1, RG_W), h0]
    if final:
        in_specs.append(pl.BlockSpec((1, t, RG_W), lambda b, i: (b, tidx(i), 0)))
        args.append(prev)
    out, st = pl.pallas_call(
        functools.partial(_rglru_kernel, rev=rev, row_len=row_len, final=final),
        out_shape=[jax.ShapeDtypeStruct((bsz, length, RG_W), BF16 if final else F32),
                   jax.ShapeDtypeStruct((bsz, 1, RG_W), F32)],
        grid=(bsz, nt),
        in_specs=in_specs,
        out_specs=[pl.BlockSpec((1, t, RG_W), lambda b, i: (b, tidx(i), 0)),
                   pl.BlockSpec((1, 1, RG_W), lambda b, i: (b, 0, 0))],
        scratch_shapes=[pltpu.VMEM((1, RG_W), F32)],
        compiler_params=_cparams(2),
        name="rglru_bwd" if rev else "rglru_fwd",
    )(*args)
    return out, st


def _rglru_mixer(p, prm, h0s, row_len):
    bsz = p.shape[0]
    if h0s is None:
        h0s = [jnp.zeros((bsz, 1, RG_W), F32)] * 2
    h_f, s_f = _rglru_pass(p, prm, 0, h0s[0], None, row_len)
    out, s_b = _rglru_pass(p, prm, 1, h0s[1], h_f, row_len)
    return out, [s_f, s_b]


def _tri_masks(n, rev):
    row = lax.broadcasted_iota(jnp.int32, (n, n), 0)
    col = lax.broadcasted_iota(jnp.int32, (n, n), 1)
    if rev:
        return row <= col, row < col
    return row >= col, row > col


def _dot_x3(a, b):
    a_hi = a.astype(BF16)
    a_lo = (a - a_hi.astype(F32)).astype(BF16)
    b_hi = b.astype(BF16)
    b_lo = (b - b_hi.astype(F32)).astype(BF16)
    mm = lambda p, q: jnp.dot(p, q, preferred_element_type=F32)
    n = b.shape[1]
    if n % LANES == 0:
        both = mm(a_hi, jnp.concatenate([b_hi, b_lo], axis=-1))
        return both[:, :n] + (both[:, n:] + mm(a_lo, b_hi))
    return mm(a_hi, b_hi) + (mm(a_hi, b_lo) + mm(a_lo, b_hi))


GD_SOLVE_DOT = _dot_x3


def _chunk_cumsum(x, rev):
    t_len = x.shape[0]
    pos = lax.broadcasted_iota(jnp.int32, x.shape, 0) & (CHUNK - 1)
    s = 1
    while s < CHUNK:
        if rev:
            x = x + jnp.where(pos < CHUNK - s, pltpu.roll(x, t_len - s, 0), 0.0)
        else:
            x = x + jnp.where(pos >= s, pltpu.roll(x, s, 0), 0.0)
        s *= 2
    return x


def _lane_form(x):
    lane = lax.broadcasted_iota(jnp.int32, x.shape, 1)
    p1 = x.astype(BF16)
    r1 = x - p1.astype(F32)
    p2 = r1.astype(BF16)
    p3 = (r1 - p2.astype(F32)).astype(BF16)
    packed = jnp.where(lane == 0, p1, jnp.where(lane == 1, p2, jnp.where(lane == 2, p3, jnp.zeros_like(p1))))
    sel = (lane < 3).astype(BF16)
    return lax.dot_general(sel, packed, (((1,), (1,)), ((), ())), preferred_element_type=F32)


def _dot_exact_rhs(a, b_exact):
    a_hi = a.astype(BF16)
    a_lo = (a - a_hi.astype(F32)).astype(BF16)
    b = b_exact.astype(BF16)
    return (jnp.dot(a_hi, b, preferred_element_type=F32) + jnp.dot(a_lo, b, preferred_element_type=F32))


def _gdn_kernel(*refs, rev, row_len, final, d):
    if final:
        (q_ref, k_ref, v_ref, z_ref, gb_ref, cq_ref, ck_ref, cv_ref, alog_ref, dtb_ref, ng_ref, s0_ref,
         prev_ref, o_ref, st_ref, state) = refs
    else:
        (q_ref, k_ref, v_ref, gb_ref, cq_ref, ck_ref, cv_ref, alog_ref, dtb_ref, ng_ref, s0_ref,
         o_ref, st_ref, state) = refs
    grp = pl.program_id(1)
    i = pl.program_id(2)
    n_heads = q_ref.shape[2] // GD_HEAD

    @pl.when(i == 0)
    def _():
        state[...] = s0_ref[0]

    qc = _silu(_short_conv(q_ref[0], cq_ref[...], row_len, 2))
    kc = _silu(_short_conv(k_ref[0], ck_ref[...], row_len, 2))
    vc = _silu(_short_conv(v_ref[0], cv_ref[...], row_len, 2))
    gbb = gb_ref[0]
    lane = lax.broadcasted_iota(jnp.int32, gbb.shape, 1)
    t_len = qc.shape[0]
    n_chunks = t_len // CHUNK
    incl, strict = _tri_masks(CHUNK, rev)
    eye = (lax.broadcasted_iota(jnp.int32, (CHUNK, CHUNK), 0)
           == lax.broadcasted_iota(jnp.int32, (CHUNK, CHUNK), 1)).astype(F32)
    order = list(range(n_chunks - 1, -1, -1) if rev else range(n_chunks))
    bf = lambda x: x.astype(BF16)

    heads = []
    for hh in range(n_heads):
        hs = slice(hh * GD_HEAD, (hh + 1) * GD_HEAD)
        head = grp * n_heads + hh
        q_h, k_h = qc[:, hs], kc[:, hs]
        q_h = q_h * lax.rsqrt(jnp.sum(q_h * q_h, axis=-1, keepdims=True) + 1e-6) * (GD_HEAD ** -0.5)
        k_h = k_h * lax.rsqrt(jnp.sum(k_h * k_h, axis=-1, keepdims=True) + 1e-6)
        g_raw = jnp.sum(jnp.where(lane == d * GD_HEADS + head, gbb, 0.0), axis=-1, keepdims=True)
        b_raw = jnp.sum(jnp.where(lane == (2 + d) * GD_HEADS + head, gbb, 0.0), axis=-1, keepdims=True)
        g = -jnp.exp(alog_ref[hh]) * _softplus(g_raw + dtb_ref[hh])
        heads.append(dict(q=q_h, k=k_h, v=vc[:, hs], beta=_sigmoid(b_raw), gc=_chunk_cumsum(g, rev)))

    pre = {}
    s_cur = [state[hh] for hh in range(n_heads)]
    os = {}

    def decay_tiles(c):
        for hh in range(n_heads):
            hd = heads[hh]
            sl = slice(c * CHUNK, (c + 1) * CHUNK)
            q_c, k_c, v_c, b_c, gc = hd["q"][sl], hd["k"][sl], hd["v"][sl], hd["beta"][sl], hd["gc"][sl]
            g_row = _lane_form(gc)
            kb = k_c * b_c
            e_gc = jnp.exp(gc)
            g_last = gc[0:1, :] if rev else gc[CHUNK - 1:CHUNK, :]
            pre[hh, c] = dict(decay_in=jnp.exp(jnp.where(incl, gc[:, :CHUNK] - g_row, -jnp.inf)),
                              q=q_c, k=k_c, kb=kb, x=jnp.concatenate([v_c * b_c, kb * e_gc], axis=-1),
                              qg=bf(q_c * e_gc), k_dec=bf(k_c * jnp.exp(g_last - gc)), dec=jnp.exp(g_last))

    def score_tiles(c):
        for hh in range(n_heads):
            p = pre[hh, c]
            sc = _dot_nt(jnp.concatenate([p["kb"], p["q"]], axis=0), p["k"])
            p["m"] = -(sc[:CHUNK] * jnp.where(strict, p["decay_in"], 0.0))
            p["a_qk"] = bf(sc[CHUNK:] * p["decay_in"])

    def solve_step(c, s):
        for hh in range(n_heads):
            p = pre[hh, c]
            if s == 0:
                p["t"] = eye + p["m"]
                p["m"] = GD_SOLVE_DOT(p["m"], p["m"])
            elif s < 5:
                w = GD_SOLVE_DOT(p["m"], jnp.concatenate([p["t"], p["m"]], axis=-1))
                p["t"] = p["t"] + w[:, :CHUNK]
                p["m"] = w[:, CHUNK:]
            else:
                p["t"] = p["t"] + GD_SOLVE_DOT(p["m"], p["t"])

    def apply_inverse(c):
        for hh in range(n_heads):
            p = pre[hh, c]
            p["x"] = _dot(p["t"], p["x"])

    def recur(c):
        for hh in range(n_heads):
            p = pre.pop((hh, c))
            x = p["x"]
            t2 = _dot(jnp.concatenate([bf(x[:, GD_HEAD:]), p["qg"]], axis=0), s_cur[hh])
            v_new = bf(x[:, :GD_HEAD] - t2[:CHUNK])
            os[hh, c] = t2[CHUNK:] + _dot(p["a_qk"], v_new)
            s_cur[hh] = s_cur[hh] * p["dec"] + _dot_tn(p["k_dec"], v_new)

    phases = ([decay_tiles, score_tiles] + [functools.partial(solve_step, s=s) for s in range(6)]
              + [apply_inverse, recur])
    for slot in range(n_chunks + len(phases) - 1):
        for ph in range(len(phases) - 1, -1, -1):
            if 0 <= slot - ph < n_chunks:
                phases[ph](order[slot - ph])

    for hh in range(n_heads):
        hs = slice(hh * GD_HEAD, (hh + 1) * GD_HEAD)
        o_h = jnp.concatenate([os[hh, c] for c in range(n_chunks)], axis=0)
        if final:
            o_t = prev_ref[0, :, hs] + o_h
            o_t = o_t * lax.rsqrt(jnp.mean(o_t * o_t, axis=-1, keepdims=True) + NORM_EPS) * ng_ref[...]
            o_ref[0, :, hs] = (o_t * _silu(z_ref[0, :, hs])).astype(o_ref.dtype)
        else:
            o_ref[0, :, hs] = o_h
        state[hh] = s_cur[hh]

    @pl.when(i == pl.num_programs(2) - 1)
    def _():
        for hh in range(n_heads):
            st_ref[0, hh] = s_cur[hh]


GD_HEADS_PER_STEP = 2
GD_TILE = 1024


def _gdn_pass(p, prm, d, s0, prev, row_len):
    bsz, length, _ = p.shape
    conv_w, a_log, dt_bias, norm_g = prm
    final = prev is not None
    rev = d == 1
    t = min(length, GD_TILE)
    nt = length // t
    tidx = (lambda i: nt - 1 - i) if rev else (lambda i: i)
    nh = GD_HEADS_PER_STEP
    wd = nh * GD_HEAD
    ng = GD_W // wd
    c0 = COL_GD // wd
    col = lambda off: pl.BlockSpec((1, t, wd), lambda b, h, i: (b, tidx(i), c0 + off * ng + h))
    cw = lambda off: pl.BlockSpec((4, wd), lambda b, h, i: (0, off * ng + h))
    hvec = pl.BlockSpec((nh, 1, LANES), lambda b, h, i: (h, 0, 0))
    in_specs = [col(0), col(1), col(2)]
    args = [p, p, p]
    if final:
        in_specs.append(col(3))
        args.append(p)
    in_specs += [pl.BlockSpec((1, t, LANES), lambda b, h, i: (b, tidx(i), COL_GD_GB // LANES)),
                 cw(0), cw(1), cw(2), hvec, hvec,
                 pl.BlockSpec((1, LANES), lambda b, h, i: (0, 0)),
                 pl.BlockSpec((1, nh, GD_HEAD, GD_HEAD), lambda b, h, i: (b, h, 0, 0))]
    bcast = lambda v: jnp.broadcast_to(v.reshape(GD_HEADS, 1, 1), (GD_HEADS, 1, LANES))
    args += [p, conv_w, conv_w, conv_w, bcast(a_log[d]), bcast(dt_bias[d]), norm_g.reshape(1, GD_HEAD), s0]
    if final:
        in_specs.append(pl.BlockSpec((1, t, wd), lambda b, h, i: (b, tidx(i), h)))
        args.append(prev)
    out, st = pl.pallas_call(
        functools.partial(_gdn_kernel, rev=rev, row_len=row_len, final=final, d=d),
        out_shape=[jax.ShapeDtypeStruct((bsz, length, GD_W), BF16 if final else F32),
                   jax.ShapeDtypeStruct((bsz, GD_HEADS, GD_HEAD, GD_HEAD), F32)],
        grid=(bsz, ng, nt),
        in_specs=in_specs,
        out_specs=[pl.BlockSpec((1, t, wd), lambda b, h, i: (b, tidx(i), h)),
                   pl.BlockSpec((1, nh, GD_HEAD, GD_HEAD), lambda b, h, i: (b, h, 0, 0))],
        scratch_shapes=[pltpu.VMEM((nh, GD_HEAD, GD_HEAD), F32)],
        compiler_params=_cparams(3),
        name="gdn_bwd" if rev else "gdn_fwd",
    )(*args)
    return out, st


def _gdn_mixer(p, prm, s0s, row_len):
    bsz = p.shape[0]
    if s0s is None:
        s0s = [jnp.zeros((bsz, GD_HEADS, GD_HEAD, GD_HEAD), F32)] * 2
    o_f, s_f = _gdn_pass(p, prm, 0, s0s[0], None, row_len)
    out, s_b = _gdn_pass(p, prm, 1, s0s[1], o_f, row_len)
    return out, [s_f, s_b]


def _stack_heads(x, lo):
    return jnp.concatenate([jnp.where(lo, x, 0.0), jnp.where(lo, 0.0, x)], axis=0)


def _rwkv_kernel(*refs, rev, final):
    if final:
        (r_ref, k_ref, v_ref, lo_ref, mur_ref, muk_ref, muv_ref, mul_ref, w0_ref, wup_ref, a0_ref, aup_ref,
         gup_ref, kk_ref, ka_ref, rk_ref, lng_ref, lnb_ref, s0_ref, prev_ref,
         o_ref, st_ref, state, c_r, c_k, c_v, c_l) = refs
    else:
        (r_ref, k_ref, v_ref, lo_ref, mur_ref, muk_ref, muv_ref, mul_ref, w0_ref, wup_ref, a0_ref, aup_ref,
         gup_ref, kk_ref, ka_ref, rk_ref, lng_ref, lnb_ref, s0_ref,
         o_ref, st_ref, state, c_r, c_k, c_v, c_l) = refs
    i = pl.program_id(2)

    @pl.when(i == 0)
    def _():
        state[...] = s0_ref[0]
        c_r[...] = jnp.zeros_like(c_r)
        c_k[...] = jnp.zeros_like(c_k)
        c_v[...] = jnp.zeros_like(c_v)
        c_l[...] = jnp.zeros_like(c_l)

    t_len = r_ref.shape[1]

    def shifted(x_ref, carry, mu_ref):
        x = x_ref[0]
        row = lax.broadcasted_iota(jnp.int32, x.shape, 0)
        if rev:
            prev = jnp.where(row == t_len - 1, carry[...], pltpu.roll(x, t_len - 1, 0))
            carry[...] = x[0:1, :]
        else:
            prev = jnp.where(row == 0, carry[...], pltpu.roll(x, 1, 0))
            carry[...] = x[t_len - 1:t_len, :]
        return x + (prev - x) * mu_ref[...]

    r = shifted(r_ref, c_r, mur_ref)
    k = shifted(k_ref, c_k, muk_ref)
    v = shifted(v_ref, c_v, muv_ref)
    lora = shifted(lo_ref, c_l, mul_ref)
    xw, xa, xg = lora[:, 0:64], lora[:, 64:128], lora[:, 128:256]

    width = r_ref.shape[2]
    n_pairs = width // LANES
    lane = lax.broadcasted_iota(jnp.int32, (1, LANES), 1)
    lo = lane < RW_HEAD
    head_sum = (lax.broadcasted_iota(jnp.int32, (width, width), 0) // RW_HEAD
                == lax.broadcasted_iota(jnp.int32, (width, width), 1) // RW_HEAD).astype(F32)

    lw = -math.exp(-0.5) * _sigmoid(w0_ref[...] + _dot(jnp.tanh(xw), wup_ref[...]))
    a = _sigmoid(a0_ref[...] + _dot(xa, aup_ref[...]))
    kk = k * kk_ref[...]
    kappa = kk / jnp.maximum(jnp.sqrt(_dot_exact_rhs(kk * kk, head_sum)), 1e-12)
    kt = k * (1.0 + (a - 1.0) * ka_ref[...])
    gate = _dot(_sigmoid(xg), gup_ref[...])
    bonus = _dot_exact_rhs(r * kt * rk_ref[...], head_sum) * v

    n_chunks = t_len // CHUNK
    lg_all = _chunk_cumsum(lw, rev)
    row_s = lax.broadcasted_iota(jnp.int32, (CHUNK, LANES), 0)
    col_s = lax.broadcasted_iota(jnp.int32, (CHUNK, LANES), 1) & (CHUNK - 1)
    incl_s = (row_s <= col_s) if rev else (row_s >= col_s)
    strict_s = (row_s < col_s) if rev else (row_s > col_s)
    eye_s = (row_s == col_s).astype(F32)
    lo2 = (lax.broadcasted_iota(jnp.int32, (1, 2 * LANES), 1) & (LANES - 1)) < RW_HEAD
    order = list(range(n_chunks - 1, -1, -1) if rev else range(n_chunks))
    bf = lambda x: x.astype(BF16)
    stack = lambda x: bf(_stack_heads(x, lo))

    pre = {}
    s_cur = [state[pp] for pp in range(n_pairs)]
    ys = {}

    def scores(c):
        for pp in range(n_pairs):
            sl = (slice(c * CHUNK, (c + 1) * CHUNK), slice(pp * LANES, (pp + 1) * LANES))
            lw_c, kap_c, a_c = lw[sl], kappa[sl], a[sl]
            lg = lg_all[sl]
            lg_tot = lg[0:1, :] if rev else lg[CHUNK - 1:CHUNK, :]
            e_neg = jnp.exp(-lg)
            e_rem = jnp.exp(lg_tot - lg)
            p_raw = -(kap_c * a_c)
            q_h = kap_c * jnp.exp(lg - lw_c)
            r_h = bf(r[sl] * jnp.exp(lg))
            sc = _dot_nt(jnp.concatenate([bf(q_h), r_h], axis=0),
                         jnp.concatenate([stack(p_raw * e_neg), stack(kt[sl] * e_neg)], axis=0))
            pre[pp, c] = dict(
                q=q_h, r=r_h, v2=stack(v[sl]), pt2=stack(p_raw * e_rem), kt2=stack(kt[sl] * e_rem),
                m=jnp.where(strict_s, sc[:CHUNK, :LANES], 0.0),
                s_qk=jnp.where(strict_s, sc[:CHUNK, LANES:], 0.0),
                s_rp=bf(jnp.where(incl_s, sc[CHUNK:, :LANES], 0.0)),
                s_rk=jnp.where(incl_s, sc[CHUNK:, LANES:], 0.0),
                dec=jnp.broadcast_to(jnp.exp(lg_tot), (LANES, LANES)).T)

    def local_terms(c):
        for pp in range(n_pairs):
            p = pre[pp, c]
            loc = _dot(jnp.concatenate([p["s_qk"], p["s_rk"]], axis=0), p["v2"])
            p["rhs"] = bf(_stack_heads(jnp.concatenate([p["q"], loc[:CHUNK]], axis=-1), lo2))
            p["y_loc"] = loc[CHUNK:]
            p["s_loc"] = _dot_tn(p["kt2"], p["v2"])

    def solve_step(c, s):
        for pp in range(n_pairs):
            p = pre[pp, c]
            if s == 0:
                p["t"] = eye_s + p["m"]
                p["m"] = _dot(p["m"], stack(p["m"]))
            elif s < 5:
                w = _dot(p["m"], jnp.concatenate([stack(p["t"]), stack(p["m"])], axis=-1))
                p["t"] = p["t"] + w[:, :LANES]
                p["m"] = w[:, LANES:]
            else:
                p["t"] = p["t"] + _dot(p["m"], stack(p["t"]))

    def apply_inverse(c):
        for pp in range(n_pairs):
            p = pre[pp, c]
            p["x"] = _dot(p["t"], p["rhs"])

    def recur(c):
        for pp in range(n_pairs):
            p = pre.pop((pp, c))
            x = p["x"]
            t2 = _dot(jnp.concatenate([bf(x[:, :LANES]), p["r"]], axis=0), s_cur[pp])
            u2 = stack(t2[:CHUNK] + x[:, LANES:])
            ys[pp, c] = t2[CHUNK:] + _dot(p["s_rp"], u2) + p["y_loc"]
            s_cur[pp] = s_cur[pp] * p["dec"] + _dot_tn(p["pt2"], u2) + p["s_loc"]

    phases = ([scores, local_terms] + [functools.partial(solve_step, s=s) for s in range(6)]
              + [apply_inverse, recur])
    for slot in range(n_chunks + len(phases) - 1):
        for ph in range(len(phases) - 1, -1, -1):
            if 0 <= slot - ph < n_chunks:
                phases[ph](order[slot - ph])

    y = jnp.concatenate([jnp.concatenate([ys[pp, c] for c in range(n_chunks)], axis=0)
                         for pp in range(n_pairs)], axis=1)
    mean = _dot_exact_rhs(y, head_sum) * (1.0 / RW_HEAD)
    yc = y - mean
    var = _dot_exact_rhs(yc * yc, head_sum) * (1.0 / RW_HEAD)
    yn = yc * lax.rsqrt(var + RW_GN_EPS) * lng_ref[...] + lnb_ref[...]
    out = (yn + bonus) * gate
    if final:
        o_ref[0] = (prev_ref[0] + out).astype(o_ref.dtype)
    else:
        o_ref[0] = out
    for pp in range(n_pairs):
        state[pp] = s_cur[pp]

    @pl.when(i == pl.num_programs(2) - 1)
    def _():
        for pp in range(n_pairs):
            st_ref[0, pp] = s_cur[pp]


RW_PAIRS_PER_STEP = 2
RW_TILE = 1024


def _rwkv_pass(p, prm, d, s0, prev):
    bsz, length, _ = p.shape
    mu, w0, w_up, a0, a_up, g_up, k_k, k_a, r_k, ln_g, ln_b = [t[d] for t in prm]
    final = prev is not None
    rev = d == 1
    t = min(length, RW_TILE)
    nt = length // t
    tidx = (lambda i: nt - 1 - i) if rev else (lambda i: i)
    npp = RW_PAIRS_PER_STEP
    wd = npp * LANES
    c0 = COL_RW // wd
    ng = RW_W // wd
    col = lambda off: pl.BlockSpec((1, t, wd), lambda b, n, i: (b, tidx(i), c0 + off * ng + n))
    vec = pl.BlockSpec((1, wd), lambda b, n, i: (0, n))
    whole = lambda shape: pl.BlockSpec(shape, lambda b, n, i: (0,) * len(shape))
    row = lambda x: x.reshape(1, -1)
    in_specs = [col(0), col(1), col(2),
                pl.BlockSpec((1, t, 256), lambda b, n, i: (b, tidx(i), COL_RW_LORA // 256)),
                vec, vec, vec, whole((1, 256)),
                vec, pl.BlockSpec((64, wd), lambda b, n, i: (0, n)),
                vec, pl.BlockSpec((64, wd), lambda b, n, i: (0, n)),
                pl.BlockSpec((LANES, wd), lambda b, n, i: (0, n)),
                vec, vec, vec, vec, vec,
                pl.BlockSpec((1, npp, LANES, LANES), lambda b, n, i: (b, n, 0, 0))]
    args = [p, p, p, p,
            row(mu[0:512]), row(mu[512:1024]), row(mu[1024:1536]), row(mu[1536:1792]),
            row(w0), w_up.astype(BF16), row(a0), a_up.astype(BF16), g_up.astype(BF16),
            row(k_k), row(k_a), row(r_k), row(ln_g), row(ln_b), s0]
    if final:
        in_specs.append(pl.BlockSpec((1, t, wd), lambda b, n, i: (b, tidx(i), n)))
        args.append(prev)
    out, st = pl.pallas_call(
        functools.partial(_rwkv_kernel, rev=rev, final=final),
        out_shape=[jax.ShapeDtypeStruct((bsz, length, RW_W), BF16 if final else F32),
                   jax.ShapeDtypeStruct((bsz, 4, LANES, LANES), F32)],
        grid=(bsz, ng, nt),
        in_specs=in_specs,
        out_specs=[pl.BlockSpec((1, t, wd), lambda b, n, i: (b, tidx(i), n)),
                   pl.BlockSpec((1, npp, LANES, LANES), lambda b, n, i: (b, n, 0, 0))],
        scratch_shapes=[pltpu.VMEM((npp, LANES, LANES), F32), pltpu.VMEM((1, wd), F32),
                        pltpu.VMEM((1, wd), F32), pltpu.VMEM((1, wd), F32),
                        pltpu.VMEM((1, 256), F32)],
        compiler_params=_cparams(3),
        name="rwkv_bwd" if rev else "rwkv_fwd",
    )(*args)
    return out, st


def _rwkv_mixer(p, prm, s0s):
    bsz = p.shape[0]
    if s0s is None:
        s0s = [jnp.zeros((bsz, 4, LANES, LANES), F32)] * 2
    o_f, s_f = _rwkv_pass(p, prm, 0, s0s[0], None)
    out, s_b = _rwkv_pass(p, prm, 1, s0s[1], o_f)
    return out, [s_f, s_b]


def _dft_table_kernel(ca_ref, sa_ref, cb_ref, sb_ref, c_ref, s_ref):
    cb, sb = cb_ref[...], sb_ref[...]
    for j in range(ca_ref.shape[1]):
        ca, sa = ca_ref[:, j:j + 1], sa_ref[:, j:j + 1]
        c_ref[:, j * LANES:(j + 1) * LANES] = (ca * cb - sa * sb).astype(c_ref.dtype)
        s_ref[:, j * LANES:(j + 1) * LANES] = (-(sa * cb + ca * sb)).astype(s_ref.dtype)


def _dft_tables(length):
    n = 2 * length
    nfp = -(-(length + 1) // LANES) * LANES
    kf = jnp.arange(nfp, dtype=jnp.int32)[:, None]
    s1 = jnp.arange(n // LANES, dtype=jnp.int32)[None, :]
    s0 = jnp.arange(LANES, dtype=jnp.int32)[None, :]
    ang_a = (2.0 * math.pi / n) * ((kf * s1 * LANES) % n).astype(F32)
    ang_b = (2.0 * math.pi / n) * ((kf * s0) % n).astype(F32)
    ok = (kf <= length).astype(F32)
    tm = 384 if nfp % 384 == 0 else LANES
    rows = lambda w: pl.BlockSpec((tm, w), lambda i: (i, 0))
    tab_c, tab_s = pl.pallas_call(
        _dft_table_kernel,
        out_shape=[jax.ShapeDtypeStruct((nfp, n), BF16)] * 2,
        grid=(nfp // tm,),
        in_specs=[rows(n // LANES), rows(n // LANES), rows(LANES), rows(LANES)],
        out_specs=[rows(n), rows(n)],
        compiler_params=_cparams(1),
        name="hyena_dft_tables",
    )(jnp.cos(ang_a) * ok, jnp.sin(ang_a) * ok, jnp.cos(ang_b), jnp.sin(ang_b))
    kk = kf[:, 0]
    wk = jnp.where((kk == 0) | (kk == length), 1.0, 2.0) * (kk <= length) / n
    return tab_c, tab_s, jnp.broadcast_to(wk[:, None], (nfp, LANES)).astype(F32)


def _hy_filter_kernel(w1_ref, b1_ref, w2_ref, b2_ref, w3_ref, fq_ref, dl_ref, hf_ref, hb_ref, nrm_ref, *, length):
    i = pl.program_id(0)
    tm = hf_ref.shape[0]
    pos = (lax.broadcasted_iota(jnp.int32, (tm, LANES), 0) + i * tm).astype(F32)
    lane = lax.broadcasted_iota(jnp.int32, (tm, LANES), 1)
    z = pos / max(length - 1, 1)
    band_idx = jnp.where(lane <= HY_BANDS, lane - 1, lane - 1 - HY_BANDS).astype(F32)
    band = 1e-4 + band_idx * ((HY_BANDS - 1 - 1e-4) / (HY_BANDS - 1))
    ang = (2.0 * math.pi / length) * pos * band
    feat = jnp.where(lane == 0, z, jnp.where(lane <= HY_BANDS, jnp.cos(ang),
                                             jnp.where(lane <= 2 * HY_BANDS, -jnp.sin(ang), 0.0)))
    h = jnp.sin(fq_ref[0:1, :] * (_dot_hp(feat, w1_ref[...]) + b1_ref[...]))
    h = jnp.sin(fq_ref[1:2, :] * (_dot_hp(h, w2_ref[...]) + b2_ref[...]))
    h = _dot_hp(h, w3_ref[...]) * jnp.exp(-z[:, 0:1] * dl_ref[...])
    h_fwd = h[:, :HY_W]
    h_bwd = jnp.where(pos[:, 0:1] > 0.0, h[:, HY_W:], 0.0)
    hf_ref[...] = h_fwd
    hb_ref[...] = h_bwd

    @pl.when(i == 0)
    def _():
        nrm_ref[...] = jnp.zeros_like(nrm_ref)

    nrm_ref[...] += jnp.sum(jnp.abs(h_fwd) + jnp.abs(h_bwd), axis=0, keepdims=True)


def _hyena_filter(length, w1, b1, w2, b2, w3, freq):
    tm = min(length, 512)
    nfilt = w1.shape[1]
    w1p = jnp.zeros((LANES, nfilt), F32).at[:HY_EMB].set(w1)
    deltas = jnp.abs(jnp.linspace(math.log(1e-2) / 1.5, math.log(1e-2) / 0.3, HY_W, dtype=F32))
    whole = lambda shape: pl.BlockSpec(shape, lambda i: (0,) * len(shape))
    h_fwd, h_bwd, norm = pl.pallas_call(
        functools.partial(_hy_filter_kernel, length=length),
        out_shape=[jax.ShapeDtypeStruct((length, HY_W), F32), jax.ShapeDtypeStruct((length, HY_W), F32),
                   jax.ShapeDtypeStruct((1, HY_W), F32)],
        grid=(length // tm,),
        in_specs=[whole((LANES, nfilt)), whole((1, nfilt)), whole((nfilt, nfilt)), whole((1, nfilt)),
                  whole((nfilt, 2 * HY_W)), whole((2, nfilt)), whole((1, 2 * HY_W))],
        out_specs=[pl.BlockSpec((tm, HY_W), lambda i: (i, 0)), pl.BlockSpec((tm, HY_W), lambda i: (i, 0)),
                   whole((1, HY_W))],
        compiler_params=_cparams(1),
        name="hyena_filter_mlp",
    )(w1p, b1.reshape(1, nfilt), w2, b2.reshape(1, nfilt), w3, freq, jnp.tile(deltas, 2).reshape(1, 2 * HY_W))
    return (h_fwd + h_bwd) / norm, (h_fwd - h_bwd) / norm


def _hy_zin_kernel(v_ref, x1_ref, cw_ref, o_ref, *, row_len):
    vc = _short_conv(v_ref[0], cw_ref[:, 0:512], row_len, 1)
    x1c = _short_conv(x1_ref[0], cw_ref[:, 1024:1536], row_len, 1)
    o_ref[0] = (x1c * vc).astype(o_ref.dtype)


def _hy_zin(p, conv_w, row_len):
    bsz, length, _ = p.shape
    t = min(length, 256)
    return pl.pallas_call(
        functools.partial(_hy_zin_kernel, row_len=row_len),
        out_shape=jax.ShapeDtypeStruct((bsz, length, HY_W), BF16),
        grid=(bsz, length // t),
        in_specs=[pl.BlockSpec((1, t, 512), lambda b, i: (b, i, 0)),
                  pl.BlockSpec((1, t, 512), lambda b, i: (b, i, 2)),
                  pl.BlockSpec((3, 1536), lambda b, i: (0, 0))],
        out_specs=pl.BlockSpec((1, t, 512), lambda b, i: (b, i, 0)),
        compiler_params=_cparams(2),
        name="hyena_zin",
    )(p, p, conv_w)


def _dft_fwd_kernel(*refs, mult):
    if mult:
        c_ref, s_ref, z_ref, fr_ref, fi_ref, yr_ref, yi_ref = refs
        z_cos = z_sin = z_ref[0].astype(BF16)
    else:
        c_ref, s_ref, zc_ref, zs_ref, wk_ref, yr_ref, yi_ref = refs
        z_cos, z_sin = zc_ref[0].astype(BF16), zs_ref[0].astype(BF16)
    zr = jnp.dot(c_ref[...], z_cos, preferred_element_type=F32)
    zi = jnp.dot(s_ref[...], z_sin, preferred_element_type=F32)
    if mult:
        fr, fi = fr_ref[...], fi_ref[...]
        zr, zi = zr * fr - zi * fi, zr * fi + zi * fr
    else:
        zr, zi = zr * wk_ref[:, 0:1], zi * wk_ref[:, 0:1]
    yr_ref[0] = zr.astype(yr_ref.dtype)
    yi_ref[0] = zi.astype(yi_ref.dtype)


def _dft_fwd(tab_c, tab_s, z, spec, wk=None):
    mult = spec is not None
    zs = (z,) if mult else z
    bsz, klen, _ = zs[0].shape
    nfp = tab_c.shape[0]
    tm = 384 if nfp % 384 == 0 else LANES
    in_specs = [pl.BlockSpec((tm, klen), lambda b, i: (i, 0)),
                pl.BlockSpec((tm, klen), lambda b, i: (i, 0))]
    in_specs += [pl.BlockSpec((1, klen, 512), lambda b, i: (b, 0, 0))] * len(zs)
    args = [tab_c, tab_s, *zs]
    if mult:
        in_specs += [pl.BlockSpec((tm, 512), lambda b, i: (i, 0))] * 2
        args += list(spec)
    else:
        in_specs.append(pl.BlockSpec((tm, LANES), lambda b, i: (i, 0)))
        args.append(wk)
    odt = BF16 if mult else F32
    return pl.pallas_call(
        functools.partial(_dft_fwd_kernel, mult=mult),
        out_shape=[jax.ShapeDtypeStruct((bsz, nfp, 512), odt)] * 2,
        grid=(bsz, nfp // tm),
        in_specs=in_specs,
        out_specs=[pl.BlockSpec((1, tm, 512), lambda b, i: (b, i, 0))] * 2,
        compiler_params=_cparams(2),
        name="hyena_dft_mul" if mult else "hyena_dft_filter",
    )(*args)


def _dft_inv_kernel(ci_ref, si_ref, yr_ref, yi_ref, v_ref, x0_ref, x1_ref, cw_ref, skip_ref, o_ref, *, row_len):
    y = jnp.dot(ci_ref[...], yr_ref[0], preferred_element_type=F32)
    y += jnp.dot(si_ref[...], yi_ref[0], preferred_element_type=F32)
    vc = _short_conv(v_ref[0], cw_ref[:, 0:512], row_len, 1)
    x0c = _short_conv(x0_ref[0], cw_ref[:, 512:1024], row_len, 1)
    x1c = _short_conv(x1_ref[0], cw_ref[:, 1024:1536], row_len, 1)
    zin = x1c * vc
    o_ref[0] = (x0c * (y + zin * skip_ref[...])).astype(o_ref.dtype)


def _dft_inv(tab_c, tab_s, yr, yi, p, conv_w, skip, row_len):
    bsz, length, _ = p.shape
    nfp = tab_c.shape[0]
    t = min(length, 256)
    pcol = lambda c: pl.BlockSpec((1, t, 512), lambda b, i: (b, i, c))
    return pl.pallas_call(
        functools.partial(_dft_inv_kernel, row_len=row_len),
        out_shape=jax.ShapeDtypeStruct((bsz, length, HY_W), BF16),
        grid=(bsz, length // t),
        in_specs=[pl.BlockSpec((t, nfp), lambda b, i: (i, 0)),
                  pl.BlockSpec((t, nfp), lambda b, i: (i, 0)),
                  pl.BlockSpec((1, nfp, 512), lambda b, i: (b, 0, 0)),
                  pl.BlockSpec((1, nfp, 512), lambda b, i: (b, 0, 0)),
                  pcol(0), pcol(1), pcol(2),
                  pl.BlockSpec((3, 1536), lambda b, i: (0, 0)),
                  pl.BlockSpec((1, 512), lambda b, i: (0, 0))],
        out_specs=pl.BlockSpec((1, t, 512), lambda b, i: (b, i, 0)),
        compiler_params=_cparams(2),
        name="hyena_idft_gate",
    )(tab_c, tab_s, yr, yi, p, p, p, conv_w, skip.reshape(1, HY_W))


def _hyena_mixer(p, prm, tables, row_len):
    conv_w, w1, b1, w2, b2, w3, freq, skip = prm
    length = p.shape[1]
    tab_c, tab_s, wk = tables
    f_cos, f_sin = _hyena_filter(length, w1, b1, w2, b2, w3, freq)
    spec = _dft_fwd(tab_c, tab_s, (f_cos[None], f_sin[None]), None, wk)
    spec = (spec[0][0], spec[1][0])
    zin = _hy_zin(p, conv_w, row_len)
    yr, yi = _dft_fwd(tab_c, tab_s, zin, spec)
    return _dft_inv(tab_c, tab_s, yr, yi, p, conv_w, skip, row_len)


def _permute_w_in(w_in):
    w_in = w_in.astype(BF16)
    hy = w_in[:, 0:1536]
    rw = w_in[:, 1536:3328]
    gd = w_in[:, 3328:5392]
    rg = w_in[:, 5392:6416]
    pad = jnp.zeros((w_in.shape[0], N_PROJ - 6416), BF16)
    return jnp.concatenate([hy, rw[:, :1536], gd[:, :2048], rg, rw[:, 1536:], gd[:, 2048:], pad], axis=-1)


def kernel(x, c, ctx, c_ctx, ada_w, ada_b, norm_mix_g, norm_mlp_g, w_in, w_out,
           hy_conv, hy_w1, hy_b1, hy_w2, hy_b2, hy_w3, hy_freq, hy_skip,
           rw_mu, rw_w0, rw_w_up, rw_a0, rw_a_up, rw_g_up, rw_k_k, rw_k_a, rw_r_k, rw_ln_g, rw_ln_b,
           gd_conv, gd_a_log, gd_dt_bias, gd_norm_g,
           rg_conv, rg_conv_b, rg_wa, rg_ba, rg_wx, rg_bx, rg_lambda,
           mlp_w1, mlp_w2, final_norm_g):
    bsz, seq, _ = x.shape
    ctx_len = ctx.shape[1]
    depth = ada_w.shape[0]
    tables_x = _dft_tables(seq)
    tables_c = _dft_tables(ctx_len)
    cond8 = jnp.concatenate([c, c_ctx[None, :], jnp.zeros((8 - bsz - 1, D_MODEL), F32)], axis=0)
    for l in range(depth):
        last = l == depth - 1
        mod = _modulation(cond8, ada_w, ada_b[l], l)
        mod_x = mod[:bsz].reshape(bsz, 6, D_MODEL)
        mod_c = mod[bsz:bsz + 1].reshape(1, 6, D_MODEL)
        w_in_bf = _permute_w_in(w_in[l])
        w_out_bf = w_out[l].astype(BF16)
        w1_bf = mlp_w1[l].astype(BF16)
        w2_bf = mlp_w2[l].astype(BF16)
        px = _inproj(x, norm_mix_g[l], mod_x, w_in_bf)
        flat = lambda t: t.reshape(1, bsz * ctx_len, t.shape[-1])
        pc = _inproj(flat(ctx), norm_mix_g[l], mod_c, w_in_bf).reshape(bsz, ctx_len, N_PROJ)
        hy_prm = (hy_conv[l], hy_w1[l], hy_b1[l], hy_w2[l], hy_b2[l], hy_w3[l], hy_freq[l], hy_skip[l])
        rw_prm = (rw_mu[l], rw_w0[l], rw_w_up[l], rw_a0[l], rw_a_up[l], rw_g_up[l],
                  rw_k_k[l], rw_k_a[l], rw_r_k[l], rw_ln_g[l], rw_ln_b[l])
        gd_prm = (gd_conv[l], gd_a_log[l], gd_dt_bias[l], gd_norm_g[l])
        rg_prm = (rg_conv[l], rg_conv_b[l], rg_wa[l], rg_ba[l], rg_wx[l], rg_bx[l], rg_lambda[l])
        c_rw, s_rw = _rwkv_mixer(pc, rw_prm, None)
        c_gd, s_gd = _gdn_mixer(pc, gd_prm, None, ctx_len)
        c_rg, s_rg = _rglru_mixer(pc, rg_prm, None, ctx_len)
        x_hy = _hyena_mixer(px, hy_prm, tables_x, GRID_W)
        x_rw, _ = _rwkv_mixer(px, rw_prm, s_rw)
        x_gd, _ = _gdn_mixer(px, gd_prm, s_gd, GRID_W)
        x_rg, _ = _rglru_mixer(px, rg_prm, s_rg, GRID_W)
        x_new = _outproj(x, mod_x, (x_hy, x_rw, x_gd, x_rg), w_out_bf)
        x_new = _mlp(x_new, norm_mlp_g[l], mod_x, w1_bf, w2_bf, final_norm_g if last else None)
        if not last:
            c_hy = _hyena_mixer(pc, hy_prm, tables_c, ctx_len)
            ctx_new = _outproj(flat(ctx), mod_c, tuple(flat(m) for m in (c_hy, c_rw, c_gd, c_rg)), w_out_bf)
            ctx = _mlp(ctx_new, norm_mlp_g[l], mod_c, w1_bf, w2_bf, None).reshape(bsz, ctx_len, D_MODEL)
        x = x_new
    return x
```

```python
import functools
import math

import jax
import jax.numpy as jnp
from jax import lax
from jax.experimental import pallas as pl
from jax.experimental.pallas import tpu as pltpu

F32 = jnp.float32
BF16 = jnp.bfloat16
HIGHEST = lax.Precision.HIGHEST

D_MODEL = 2048
GRID_W = 64
HY_W = RW_W = GD_W = RG_W = 512
D_FF = 4 * D_MODEL
NORM_EPS = 1e-6
HY_EMB = 33
HY_BANDS = 16
RW_HEAD = 64
RW_GN_EPS = 64e-5
GD_HEAD = 128
GD_HEADS = 4
CHUNK = 64
RG_C = 8.0
LANES = 128

COL_HY = 0
COL_RW = 1536
COL_GD = 3072
COL_RG = 5120
COL_RW_LORA = 6144
COL_GD_GB = 6400
N_PROJ = 6656

VMEM_LIMIT = 56 * 1024 * 1024


def _cparams(n_axes):
    return pltpu.CompilerParams(dimension_semantics=("arbitrary",) * n_axes,
                                vmem_limit_bytes=VMEM_LIMIT)


def _dot(a, b):
    return jnp.dot(a.astype(BF16), b.astype(BF16), preferred_element_type=F32)


def _dot_nt(a, b):
    return lax.dot_general(a.astype(BF16), b.astype(BF16), (((1,), (1,)), ((), ())),
                           preferred_element_type=F32)


def _dot_tn(a, b):
    return lax.dot_general(a.astype(BF16), b.astype(BF16), (((0,), (0,)), ((), ())),
                           preferred_element_type=F32)


def _dot_hp(a, b):
    return jnp.dot(a, b, precision=HIGHEST, preferred_element_type=F32)


def _sigmoid(x):
    return 1.0 / (1.0 + jnp.exp(-x))


def _silu(x):
    return x * _sigmoid(x)


def _softplus(x):
    return jnp.maximum(x, 0.0) + jnp.log1p(jnp.exp(-jnp.abs(x)))


def _short_conv(u, w, row_len, pad_left):
    t_len = u.shape[0]
    pos = lax.broadcasted_iota(jnp.int32, u.shape, 0) & (row_len - 1)
    y = None
    for j in range(w.shape[0]):
        off = j - pad_left
        if off == 0:
            term = u * w[j:j + 1, :]
        else:
            sh = pltpu.roll(u, (-off) % t_len, 0)
            ok = (pos + off >= 0) & (pos + off < row_len)
            term = jnp.where(ok, sh, 0.0) * w[j:j + 1, :]
        y = term if y is None else y + term
    return y


def _mod_kernel(c_ref, w_ref, b_ref, o_ref):
    o_ref[...] = _dot(_silu(c_ref[...]), w_ref[0]) + b_ref[...]


def _modulation(cond8, ada_w, ada_b, layer):
    n = ada_w.shape[2]
    tn = 512
    return pl.pallas_call(
        _mod_kernel,
        out_shape=jax.ShapeDtypeStruct((8, n), F32),
        grid=(n // tn,),
        in_specs=[pl.BlockSpec((8, D_MODEL), lambda j: (0, 0)),
                  pl.BlockSpec((1, D_MODEL, tn), lambda j: (layer, 0, j)),
                  pl.BlockSpec((1, tn), lambda j: (0, j))],
        out_specs=pl.BlockSpec((8, tn), lambda j: (0, j)),
        compiler_params=_cparams(1),
        name="adaln_mod",
    )(cond8, ada_w, ada_b.reshape(1, n))


def _norm_mod(x, g, shift, scale):
    y = x * lax.rsqrt(jnp.mean(x * x, axis=-1, keepdims=True) + NORM_EPS) * g
    return y * (1.0 + scale) + shift


def _inproj_kernel(x_ref, g_ref, mod_ref, w_ref, o_ref, h_scr):
    @pl.when(pl.program_id(2) == 0)
    def _():
        h = _norm_mod(x_ref[0], g_ref[...], mod_ref[0, 0:1, :], mod_ref[0, 1:2, :])
        h_scr[...] = h.astype(BF16)

    o_ref[0] = jnp.dot(h_scr[...], w_ref[...], preferred_element_type=F32)


def _inproj(x, g, mod, w_bf):
    bsz, length, _ = x.shape
    tm = min(length, 1024)
    tn = 512
    per_batch = mod.shape[0] == bsz
    return pl.pallas_call(
        _inproj_kernel,
        out_shape=jax.ShapeDtypeStruct((bsz, length, N_PROJ), F32),
        grid=(bsz, length // tm, N_PROJ // tn),
        in_specs=[pl.BlockSpec((1, tm, D_MODEL), lambda b, i, j: (b, i, 0)),
                  pl.BlockSpec((1, D_MODEL), lambda b, i, j: (0, 0)),
                  pl.BlockSpec((1, 6, D_MODEL), (lambda b, i, j: (b, 0, 0)) if per_batch
                               else (lambda b, i, j: (0, 0, 0))),
                  pl.BlockSpec((D_MODEL, tn), lambda b, i, j: (0, j))],
        out_specs=pl.BlockSpec((1, tm, tn), lambda b, i, j: (b, i, j)),
        scratch_shapes=[pltpu.VMEM((tm, D_MODEL), BF16)],
        compiler_params=_cparams(3),
        name="inproj",
    )(x, g.reshape(1, D_MODEL), mod, w_bf)


def _outproj_kernel(x_ref, mod_ref, m0_ref, m1_ref, m2_ref, m3_ref, w_ref, o_ref):
    acc = jnp.dot(m0_ref[0], w_ref[0:512, :], preferred_element_type=F32)
    acc += jnp.dot(m1_ref[0], w_ref[512:1024, :], preferred_element_type=F32)
    acc += jnp.dot(m2_ref[0], w_ref[1024:1536, :], preferred_element_type=F32)
    acc += jnp.dot(m3_ref[0], w_ref[1536:2048, :], preferred_element_type=F32)
    o_ref[0] = x_ref[0] + mod_ref[0, 2:3, :] * acc


def _outproj(x, mod, mixers, w_bf):
    bsz, length, _ = x.shape
    tm = min(length, 512)
    per_batch = mod.shape[0] == bsz
    mspec = pl.BlockSpec((1, tm, 512), lambda b, i: (b, i, 0))
    return pl.pallas_call(
        _outproj_kernel,
        out_shape=jax.ShapeDtypeStruct((bsz, length, D_MODEL), F32),
        grid=(bsz, length // tm),
        in_specs=[pl.BlockSpec((1, tm, D_MODEL), lambda b, i: (b, i, 0)),
                  pl.BlockSpec((1, 6, D_MODEL), (lambda b, i: (b, 0, 0)) if per_batch
                               else (lambda b, i: (0, 0, 0))),
                  mspec, mspec, mspec, mspec,
                  pl.BlockSpec((D_MODEL, D_MODEL), lambda b, i: (0, 0))],
        out_specs=pl.BlockSpec((1, tm, D_MODEL), lambda b, i: (b, i, 0)),
        compiler_params=_cparams(2),
        name="outproj",
    )(x, mod, *mixers, w_bf)


def _mlp_kernel(x_ref, g_ref, mod_ref, w1_ref, w2_ref, fg_ref, o_ref, h_scr, acc_scr, *, final_norm):
    f = pl.program_id(2)

    @pl.when(f == 0)
    def _():
        h = _norm_mod(x_ref[0], g_ref[...], mod_ref[0, 3:4, :], mod_ref[0, 4:5, :])
        h_scr[...] = h.astype(BF16)
        acc_scr[...] = jnp.zeros_like(acc_scr)

    a = jnp.dot(h_scr[...], w1_ref[...], preferred_element_type=F32)
    a = jnp.square(jnp.maximum(a, 0.0)).astype(BF16)
    acc_scr[...] += jnp.dot(a, w2_ref[...], preferred_element_type=F32)

    @pl.when(f == pl.num_programs(2) - 1)
    def _():
        y = x_ref[0] + mod_ref[0, 5:6, :] * acc_scr[...]
        if final_norm:
            y = y * lax.rsqrt(jnp.mean(y * y, axis=-1, keepdims=True) + NORM_EPS) * fg_ref[...]
        o_ref[0] = y


def _mlp(x, g, mod, w1_bf, w2_bf, final_g):
    bsz, length, _ = x.shape
    tm = min(length, 512)
    tf = 1024
    per_batch = mod.shape[0] == bsz
    final_norm = final_g is not None
    fg = (final_g if final_norm else jnp.ones((D_MODEL,), F32)).reshape(1, D_MODEL)
    return pl.pallas_call(
        functools.partial(_mlp_kernel, final_norm=final_norm),
        out_shape=jax.ShapeDtypeStruct((bsz, length, D_MODEL), F32),
        grid=(bsz, length // tm, D_FF // tf),
        in_specs=[pl.BlockSpec((1, tm, D_MODEL), lambda b, i, f: (b, i, 0)),
                  pl.BlockSpec((1, D_MODEL), lambda b, i, f: (0, 0)),
                  pl.BlockSpec((1, 6, D_MODEL), (lambda b, i, f: (b, 0, 0)) if per_batch
                               else (lambda b, i, f: (0, 0, 0))),
                  pl.BlockSpec((D_MODEL, tf), lambda b, i, f: (0, f)),
                  pl.BlockSpec((tf, D_MODEL), lambda b, i, f: (f, 0)),
                  pl.BlockSpec((1, D_MODEL), lambda b, i, f: (0, 0))],
        out_specs=pl.BlockSpec((1, tm, D_MODEL), lambda b, i, f: (b, i, 0)),
        scratch_shapes=[pltpu.VMEM((tm, D_MODEL), BF16), pltpu.VMEM((tm, D_MODEL), F32)],
        compiler_params=_cparams(3),
        name="mlp",
    )(x, g.reshape(1, D_MODEL), mod, w1_bf, w2_bf, fg)


def _lin_scan(a, b, carry, rev):
    t_len, width = a.shape
    n_groups = t_len // 8
    a = a.reshape(n_groups, 8, width)
    b = b.reshape(n_groups, 8, width)
    sub = lax.broadcasted_iota(jnp.int32, a.shape, 1)
    for s in (1, 2, 4):
        if rev:
            a_sh = pltpu.roll(a, 8 - s, 1)
            b_sh = pltpu.roll(b, 8 - s, 1)
            ok = sub < 8 - s
        else:
            a_sh = pltpu.roll(a, s, 1)
            b_sh = pltpu.roll(b, s, 1)
            ok = sub >= s
        b = a * jnp.where(ok, b_sh, 0.0) + b
        a = a * jnp.where(ok, a_sh, 1.0)
    a = a.reshape(t_len, width)
    b = b.reshape(t_len, width)
    hs = [None] * n_groups
    for g in (range(n_groups - 1, -1, -1) if rev else range(n_groups)):
        h_g = a[8 * g:8 * g + 8, :] * carry + b[8 * g:8 * g + 8, :]
        hs[g] = h_g
        carry = h_g[0:1, :] if rev else h_g[7:8, :]
    return jnp.concatenate(hs, axis=0), carry


def _rglru_kernel(*refs, rev, row_len, final):
    if final:
        (x_ref, gate_ref, cw_ref, cb_ref, wa_ref, ba_ref, wx_ref, bx_ref, lam_ref, h0_ref, prev_ref,
         o_ref, st_ref, carry) = refs
    else:
        (x_ref, cw_ref, cb_ref, wa_ref, ba_ref, wx_ref, bx_ref, lam_ref, h0_ref,
         o_ref, st_ref, carry) = refs
    i = pl.program_id(1)

    @pl.when(i == 0)
    def _():
        carry[...] = h0_ref[0]

    xc = _short_conv(x_ref[0], cw_ref[...], row_len, 2) + cb_ref[...]
    xb = xc.astype(BF16)
    blocks = [slice(n * LANES, (n + 1) * LANES) for n in range(RG_W // LANES)]
    gate_r = jnp.concatenate([jnp.dot(xb[:, bs], wa_ref[n], preferred_element_type=F32)
                              for n, bs in enumerate(blocks)], axis=-1)
    gate_i = jnp.concatenate([jnp.dot(xb[:, bs], wx_ref[n], preferred_element_type=F32)
                              for n, bs in enumerate(blocks)], axis=-1)
    gate_r = _sigmoid(gate_r + ba_ref[...])
    gate_i = _sigmoid(gate_i + bx_ref[...])
    log_a = -RG_C * gate_r * _softplus(-lam_ref[...])
    a = jnp.exp(log_a)
    b = jnp.sqrt(-jnp.tanh(log_a) * (a * a + 1.0)) * (gate_i * xc)
    h, last = _lin_scan(a, b, carry[...], rev)
    carry[...] = last

    @pl.when(i == pl.num_programs(1) - 1)
    def _():
        st_ref[0] = last

    if final:
        gate = gate_ref[0]
        gelu = 0.5 * gate * (1.0 + jnp.tanh(math.sqrt(2.0 / math.pi) * (gate + 0.044715 * gate * gate * gate)))
        o_ref[0] = (gelu * (prev_ref[0] + h)).astype(o_ref.dtype)
    else:
        o_ref[0] = h


def _rglru_pass(p, prm, d, h0, prev, row_len):
    bsz, length, _ = p.shape
    conv_w, conv_b, wa, ba, wx, bx, lam = prm
    final = prev is not None
    rev = d == 1
    t = min(length, 512)
    nt = length // t
    tidx = (lambda i: nt - 1 - i) if rev else (lambda i: i)
    cx = COL_RG // RG_W
    col = lambda c0: pl.BlockSpec((1, t, RG_W), lambda b, i: (b, tidx(i), c0))
    vec = pl.BlockSpec((1, RG_W), lambda b, i: (0, 0))
    mat = pl.BlockSpec((RG_W // LANES, LANES, LANES), lambda b, i: (0, 0, 0))
    in_specs = [col(cx)]
    args = [p]
    if final:
        in_specs.append(col(cx + 1))
        args.append(p)
    in_specs += [pl.BlockSpec((4, RG_W), lambda b, i: (0, 0)), vec, mat, vec, mat, vec, vec,
                 pl.BlockSpec((1, 1, RG_W), lambda b, i: (b, 0, 0))]
    args += [conv_w, conv_b.reshape(1, RG_W), wa[d].astype(BF16), ba[d].reshape(1, RG_W),
             wx[d].astype(BF16), bx[d].reshape(1, RG_W), lam[d].reshape(1, RG_W), h0]
    if final:
        in_specs.append(pl.BlockSpec((1, t, RG_W), lambda b, i: (b, tidx(i), 0)))
        args.append(prev)
    out, st = pl.pallas_call(
        functools.partial(_rglru_kernel, rev=rev, row_len=row_len, final=final),
        out_shape=[jax.ShapeDtypeStruct((bsz, length, RG_W), BF16 if final else F32),
                   jax.ShapeDtypeStruct((bsz, 1, RG_W), F32)],
        grid=(bsz, nt),
        in_specs=in_specs,
        out_specs=[pl.BlockSpec((1, t, RG_W), lambda b, i: (b, tidx(i), 0)),
                   pl.BlockSpec((1, 1, RG_W), lambda b, i: (b, 0, 0))],
        scratch_shapes=[pltpu.VMEM((1, RG_W), F32)],
        compiler_params=_cparams(2),
        name="rglru_bwd" if rev else "rglru_fwd",
    )(*args)
    return out, st


def _rglru_mixer(p, prm, h0s, row_len):
    bsz = p.shape[0]
    if h0s is None:
        h0s = [jnp.zeros((bsz, 1, RG_W), F32)] * 2
    h_f, s_f = _rglru_pass(p, prm, 0, h0s[0], None, row_len)
    out, s_b = _rglru_pass(p, prm, 1, h0s[1], h_f, row_len)
    return out, [s_f, s_b]


def _tri_masks(n, rev):
    row = lax.broadcasted_iota(jnp.int32, (n, n), 0)
    col = lax.broadcasted_iota(jnp.int32, (n, n), 1)
    if rev:
        return row <= col, row < col
    return row >= col, row > col


def _dot_x3(a, b):
    a_hi = a.astype(BF16)
    a_lo = (a - a_hi.astype(F32)).astype(BF16)
    b_hi = b.astype(BF16)
    b_lo = (b - b_hi.astype(F32)).astype(BF16)
    mm = lambda p, q: jnp.dot(p, q, preferred_element_type=F32)
    n = b.shape[1]
    if n % LANES == 0:
        both = mm(a_hi, jnp.concatenate([b_hi, b_lo], axis=-1))
        return both[:, :n] + (both[:, n:] + mm(a_lo, b_hi))
    return mm(a_hi, b_hi) + (mm(a_hi, b_lo) + mm(a_lo, b_hi))


GD_SOLVE_DOT = _dot_x3


def _chunk_cumsum(x, rev):
    t_len = x.shape[0]
    pos = lax.broadcasted_iota(jnp.int32, x.shape, 0) & (CHUNK - 1)
    s = 1
    while s < CHUNK:
        if rev:
            x = x + jnp.where(pos < CHUNK - s, pltpu.roll(x, t_len - s, 0), 0.0)
        else:
            x = x + jnp.where(pos >= s, pltpu.roll(x, s, 0), 0.0)
        s *= 2
    return x


def _lane_form(x):
    lane = lax.broadcasted_iota(jnp.int32, x.shape, 1)
    p1 = x.astype(BF16)
    r1 = x - p1.astype(F32)
    p2 = r1.astype(BF16)
    p3 = (r1 - p2.astype(F32)).astype(BF16)
    packed = jnp.where(lane == 0, p1, jnp.where(lane == 1, p2, jnp.where(lane == 2, p3, jnp.zeros_like(p1))))
    sel = (lane < 3).astype(BF16)
    return lax.dot_general(sel, packed, (((1,), (1,)), ((), ())), preferred_element_type=F32)


def _dot_exact_rhs(a, b_exact):
    return jnp.dot(a.astype(BF16), b_exact.astype(BF16), preferred_element_type=F32)


def _gdn_kernel(*refs, rev, row_len, final, d):
    if final:
        (q_ref, k_ref, v_ref, z_ref, gb_ref, cq_ref, ck_ref, cv_ref, alog_ref, dtb_ref, ng_ref, s0_ref,
         prev_ref, o_ref, st_ref, state) = refs
    else:
        (q_ref, k_ref, v_ref, gb_ref, cq_ref, ck_ref, cv_ref, alog_ref, dtb_ref, ng_ref, s0_ref,
         o_ref, st_ref, state) = refs
    grp = pl.program_id(1)
    i = pl.program_id(2)
    n_heads = q_ref.shape[2] // GD_HEAD

    @pl.when(i == 0)
    def _():
        state[...] = s0_ref[0]

    qc = _silu(_short_conv(q_ref[0], cq_ref[...], row_len, 2))
    kc = _silu(_short_conv(k_ref[0], ck_ref[...], row_len, 2))
    vc = _silu(_short_conv(v_ref[0], cv_ref[...], row_len, 2))
    gbb = gb_ref[0]
    lane = lax.broadcasted_iota(jnp.int32, gbb.shape, 1)
    t_len = qc.shape[0]
    n_chunks = t_len // CHUNK
    incl, strict = _tri_masks(CHUNK, rev)
    eye = (lax.broadcasted_iota(jnp.int32, (CHUNK, CHUNK), 0)
           == lax.broadcasted_iota(jnp.int32, (CHUNK, CHUNK), 1)).astype(F32)
    order = list(range(n_chunks - 1, -1, -1) if rev else range(n_chunks))
    bf = lambda x: x.astype(BF16)

    heads = []
    for hh in range(n_heads):
        hs = slice(hh * GD_HEAD, (hh + 1) * GD_HEAD)
        head = grp * n_heads + hh
        q_h, k_h = qc[:, hs], kc[:, hs]
        q_h = q_h * lax.rsqrt(jnp.sum(q_h * q_h, axis=-1, keepdims=True) + 1e-6) * (GD_HEAD ** -0.5)
        k_h = k_h * lax.rsqrt(jnp.sum(k_h * k_h, axis=-1, keepdims=True) + 1e-6)
        g_raw = jnp.sum(jnp.where(lane == d * GD_HEADS + head, gbb, 0.0), axis=-1, keepdims=True)
        b_raw = jnp.sum(jnp.where(lane == (2 + d) * GD_HEADS + head, gbb, 0.0), axis=-1, keepdims=True)
        g = -jnp.exp(alog_ref[hh]) * _softplus(g_raw + dtb_ref[hh])
        heads.append(dict(q=q_h, k=k_h, v=vc[:, hs], beta=_sigmoid(b_raw), gc=_chunk_cumsum(g, rev)))

    pre = {}
    s_cur = [state[hh] for hh in range(n_heads)]
    os = {}

    def decay_tiles(c):
        for hh in range(n_heads):
            hd = heads[hh]
            sl = slice(c * CHUNK, (c + 1) * CHUNK)
            q_c, k_c, v_c, b_c, gc = hd["q"][sl], hd["k"][sl], hd["v"][sl], hd["beta"][sl], hd["gc"][sl]
            g_row = _lane_form(gc)
            kb = k_c * b_c
            e_gc = jnp.exp(gc)
            g_last = gc[0:1, :] if rev else gc[CHUNK - 1:CHUNK, :]
            pre[hh, c] = dict(decay_in=jnp.exp(jnp.where(incl, gc[:, :CHUNK] - g_row, -jnp.inf)),
                              q=q_c, k=k_c, kb=kb, x=jnp.concatenate([v_c * b_c, kb * e_gc], axis=-1),
                              qg=bf(q_c * e_gc), k_dec=bf(k_c * jnp.exp(g_last - gc)), dec=jnp.exp(g_last))

    def score_tiles(c):
        for hh in range(n_heads):
            p = pre[hh, c]
            sc = _dot_nt(jnp.concatenate([p["kb"], p["q"]], axis=0), p["k"])
            p["m"] = -(sc[:CHUNK] * jnp.where(strict, p["decay_in"], 0.0))
            p["a_qk"] = bf(sc[CHUNK:] * p["decay_in"])

    def solve_step(c, s):
        for hh in range(n_heads):
            p = pre[hh, c]
            if s == 0:
                p["t"] = eye + p["m"]
                p["m"] = GD_SOLVE_DOT(p["m"], p["m"])
            elif s < 5:
                w = GD_SOLVE_DOT(p["m"], jnp.concatenate([p["t"], p["m"]], axis=-1))
                p["t"] = p["t"] + w[:, :CHUNK]
                p["m"] = w[:, CHUNK:]
            else:
                p["t"] = p["t"] + GD_SOLVE_DOT(p["m"], p["t"])

    def apply_inverse(c):
        for hh in range(n_heads):
            p = pre[hh, c]
            p["x"] = _dot(p["t"], p["x"])

    def recur(c):
        for hh in range(n_heads):
            p = pre.pop((hh, c))
            x = p["x"]
            t2 = _dot(jnp.concatenate([bf(x[:, GD_HEAD:]), p["qg"]], axis=0), s_cur[hh])
            v_new = bf(x[:, :GD_HEAD] - t2[:CHUNK])
            os[hh, c] = t2[CHUNK:] + _dot(p["a_qk"], v_new)
            s_cur[hh] = s_cur[hh] * p["dec"] + _dot_tn(p["k_dec"], v_new)

    phases = ([decay_tiles, score_tiles] + [functools.partial(solve_step, s=s) for s in range(6)]
              + [apply_inverse, recur])
    for slot in range(n_chunks + len(phases) - 1):
        for ph in range(len(phases) - 1, -1, -1):
            if 0 <= slot - ph < n_chunks:
                phases[ph](order[slot - ph])

    for hh in range(n_heads):
        hs = slice(hh * GD_HEAD, (hh + 1) * GD_HEAD)
        o_h = jnp.concatenate([os[hh, c] for c in range(n_chunks)], axis=0)
        if final:
            o_t = prev_ref[0, :, hs] + o_h
            o_t = o_t * lax.rsqrt(jnp.mean(o_t * o_t, axis=-1, keepdims=True) + NORM_EPS) * ng_ref[...]
            o_ref[0, :, hs] = (o_t * _silu(z_ref[0, :, hs])).astype(o_ref.dtype)
        else:
            o_ref[0, :, hs] = o_h
        state[hh] = s_cur[hh]

    @pl.when(i == pl.num_programs(2) - 1)
    def _():
        for hh in range(n_heads):
            st_ref[0, hh] = s_cur[hh]


GD_HEADS_PER_STEP = 2
GD_TILE = 1024


def _gdn_pass(p, prm, d, s0, prev, row_len):
    bsz, length, _ = p.shape
    conv_w, a_log, dt_bias, norm_g = prm
    final = prev is not None
    rev = d == 1
    t = min(length, GD_TILE)
    nt = length // t
    tidx = (lambda i: nt - 1 - i) if rev else (lambda i: i)
    nh = GD_HEADS_PER_STEP
    wd = nh * GD_HEAD
    ng = GD_W // wd
    c0 = COL_GD // wd
    col = lambda off: pl.BlockSpec((1, t, wd), lambda b, h, i: (b, tidx(i), c0 + off * ng + h))
    cw = lambda off: pl.BlockSpec((4, wd), lambda b, h, i: (0, off * ng + h))
    hvec = pl.BlockSpec((nh, 1, LANES), lambda b, h, i: (h, 0, 0))
    in_specs = [col(0), col(1), col(2)]
    args = [p, p, p]
    if final:
        in_specs.append(col(3))
        args.append(p)
    in_specs += [pl.BlockSpec((1, t, LANES), lambda b, h, i: (b, tidx(i), COL_GD_GB // LANES)),
                 cw(0), cw(1), cw(2), hvec, hvec,
                 pl.BlockSpec((1, LANES), lambda b, h, i: (0, 0)),
                 pl.BlockSpec((1, nh, GD_HEAD, GD_HEAD), lambda b, h, i: (b, h, 0, 0))]
    bcast = lambda v: jnp.broadcast_to(v.reshape(GD_HEADS, 1, 1), (GD_HEADS, 1, LANES))
    args += [p, conv_w, conv_w, conv_w, bcast(a_log[d]), bcast(dt_bias[d]), norm_g.reshape(1, GD_HEAD), s0]
    if final:
        in_specs.append(pl.BlockSpec((1, t, wd), lambda b, h, i: (b, tidx(i), h)))
        args.append(prev)
    out, st = pl.pallas_call(
        functools.partial(_gdn_kernel, rev=rev, row_len=row_len, final=final, d=d),
        out_shape=[jax.ShapeDtypeStruct((bsz, length, GD_W), BF16 if final else F32),
                   jax.ShapeDtypeStruct((bsz, GD_HEADS, GD_HEAD, GD_HEAD), F32)],
        grid=(bsz, ng, nt),
        in_specs=in_specs,
        out_specs=[pl.BlockSpec((1, t, wd), lambda b, h, i: (b, tidx(i), h)),
                   pl.BlockSpec((1, nh, GD_HEAD, GD_HEAD), lambda b, h, i: (b, h, 0, 0))],
        scratch_shapes=[pltpu.VMEM((nh, GD_HEAD, GD_HEAD), F32)],
        compiler_params=_cparams(3),
        name="gdn_bwd" if rev else "gdn_fwd",
    )(*args)
    return out, st


def _gdn_mixer(p, prm, s0s, row_len):
    bsz = p.shape[0]
    if s0s is None:
        s0s = [jnp.zeros((bsz, GD_HEADS, GD_HEAD, GD_HEAD), F32)] * 2
    o_f, s_f = _gdn_pass(p, prm, 0, s0s[0], None, row_len)
    out, s_b = _gdn_pass(p, prm, 1, s0s[1], o_f, row_len)
    return out, [s_f, s_b]


def _stack_heads(x, lo):
    return jnp.concatenate([jnp.where(lo, x, 0.0), jnp.where(lo, 0.0, x)], axis=0)


def _rwkv_kernel(*refs, rev, final):
    if final:
        (r_ref, k_ref, v_ref, lo_ref, mur_ref, muk_ref, muv_ref, mul_ref, w0_ref, wup_ref, a0_ref, aup_ref,
         gup_ref, kk_ref, ka_ref, rk_ref, lng_ref, lnb_ref, s0_ref, prev_ref,
         o_ref, st_ref, state, c_r, c_k, c_v, c_l) = refs
    else:
        (r_ref, k_ref, v_ref, lo_ref, mur_ref, muk_ref, muv_ref, mul_ref, w0_ref, wup_ref, a0_ref, aup_ref,
         gup_ref, kk_ref, ka_ref, rk_ref, lng_ref, lnb_ref, s0_ref,
         o_ref, st_ref, state, c_r, c_k, c_v, c_l) = refs
    i = pl.program_id(2)

    @pl.when(i == 0)
    def _():
        state[...] = s0_ref[0]
        c_r[...] = jnp.zeros_like(c_r)
        c_k[...] = jnp.zeros_like(c_k)
        c_v[...] = jnp.zeros_like(c_v)
        c_l[...] = jnp.zeros_like(c_l)

    t_len = r_ref.shape[1]

    def shifted(x_ref, carry, mu_ref):
        x = x_ref[0]
        row = lax.broadcasted_iota(jnp.int32, x.shape, 0)
        if rev:
            prev = jnp.where(row == t_len - 1, carry[...], pltpu.roll(x, t_len - 1, 0))
            carry[...] = x[0:1, :]
        else:
            prev = jnp.where(row == 0, carry[...], pltpu.roll(x, 1, 0))
            carry[...] = x[t_len - 1:t_len, :]
        return x + (prev - x) * mu_ref[...]

    r = shifted(r_ref, c_r, mur_ref)
    k = shifted(k_ref, c_k, muk_ref)
    v = shifted(v_ref, c_v, muv_ref)
    lora = shifted(lo_ref, c_l, mul_ref)
    xw, xa, xg = lora[:, 0:64], lora[:, 64:128], lora[:, 128:256]

    width = r_ref.shape[2]
    n_pairs = width // LANES
    lane = lax.broadcasted_iota(jnp.int32, (1, LANES), 1)
    lo = lane < RW_HEAD
    head_sum = (lax.broadcasted_iota(jnp.int32, (width, width), 0) // RW_HEAD
                == lax.broadcasted_iota(jnp.int32, (width, width), 1) // RW_HEAD).astype(F32)

    lw = -math.exp(-0.5) * _sigmoid(w0_ref[...] + _dot(jnp.tanh(xw), wup_ref[...]))
    a = _sigmoid(a0_ref[...] + _dot(xa, aup_ref[...]))
    kk = k * kk_ref[...]
    kappa = kk / jnp.maximum(jnp.sqrt(_dot_exact_rhs(kk * kk, head_sum)), 1e-12)
    kt = k * (1.0 + (a - 1.0) * ka_ref[...])
    gate = _dot(_sigmoid(xg), gup_ref[...])
    bonus = _dot_exact_rhs(r * kt * rk_ref[...], head_sum) * v

    n_chunks = t_len // CHUNK
    lg_all = _chunk_cumsum(lw, rev)
    row_s = lax.broadcasted_iota(jnp.int32, (CHUNK, LANES), 0)
    col_s = lax.broadcasted_iota(jnp.int32, (CHUNK, LANES), 1) & (CHUNK - 1)
    incl_s = (row_s <= col_s) if rev else (row_s >= col_s)
    strict_s = (row_s < col_s) if rev else (row_s > col_s)
    eye_s = (row_s == col_s).astype(F32)
    lo2 = (lax.broadcasted_iota(jnp.int32, (1, 2 * LANES), 1) & (LANES - 1)) < RW_HEAD
    order = list(range(n_chunks - 1, -1, -1) if rev else range(n_chunks))
    bf = lambda x: x.astype(BF16)
    stack = lambda x: bf(_stack_heads(x, lo))

    pre = {}
    s_cur = [state[pp] for pp in range(n_pairs)]
    ys = {}

    def scores(c):
        for pp in range(n_pairs):
            sl = (slice(c * CHUNK, (c + 1) * CHUNK), slice(pp * LANES, (pp + 1) * LANES))
            lw_c, kap_c, a_c = lw[sl], kappa[sl], a[sl]
            lg = lg_all[sl]
            lg_tot = lg[0:1, :] if rev else lg[CHUNK - 1:CHUNK, :]
            e_neg = jnp.exp(-lg)
            e_rem = jnp.exp(lg_tot - lg)
            p_raw = -(kap_c * a_c)
            q_h = kap_c * jnp.exp(lg - lw_c)
            r_h = bf(r[sl] * jnp.exp(lg))
            sc = _dot_nt(jnp.concatenate([bf(q_h), r_h], axis=0),
                         jnp.concatenate([stack(p_raw * e_neg), stack(kt[sl] * e_neg)], axis=0))
            pre[pp, c] = dict(
                q=q_h, r=r_h, v2=stack(v[sl]), pt2=stack(p_raw * e_rem), kt2=stack(kt[sl] * e_rem),
                m=jnp.where(strict_s, sc[:CHUNK, :LANES], 0.0),
                s_qk=jnp.where(strict_s, sc[:CHUNK, LANES:], 0.0),
                s_rp=bf(jnp.where(incl_s, sc[CHUNK:, :LANES], 0.0)),
                s_rk=jnp.where(incl_s, sc[CHUNK:, LANES:], 0.0),
                dec=jnp.broadcast_to(jnp.exp(lg_tot), (LANES, LANES)).T)

    def local_terms(c):
        for pp in range(n_pairs):
            p = pre[pp, c]
            loc = _dot(jnp.concatenate([p["s_qk"], p["s_rk"]], axis=0), p["v2"])
            p["rhs"] = bf(_stack_heads(jnp.concatenate([p["q"], loc[:CHUNK]], axis=-1), lo2))
            p["y_loc"] = loc[CHUNK:]
            p["s_loc"] = _dot_tn(p["kt2"], p["v2"])

    def solve_step(c, s):
        for pp in range(n_pairs):
            p = pre[pp, c]
            if s == 0:
                p["t"] = eye_s + p["m"]
                p["m"] = _dot(p["m"], stack(p["m"]))
            elif s < 5:
                w = _dot(p["m"], jnp.concatenate([stack(p["t"]), stack(p["m"])], axis=-1))
                p["t"] = p["t"] + w[:, :LANES]
                p["m"] = w[:, LANES:]
            else:
                p["t"] = p["t"] + _dot(p["m"], stack(p["t"]))

    def apply_inverse(c):
        for pp in range(n_pairs):
            p = pre[pp, c]
            p["x"] = _dot(p["t"], p["rhs"])

    def recur(c):
        for pp in range(n_pairs):
            p = pre.pop((pp, c))
            x = p["x"]
            t2 = _dot(jnp.concatenate([bf(x[:, :LANES]), p["r"]], axis=0), s_cur[pp])
            u2 = stack(t2[:CHUNK] + x[:, LANES:])
            ys[pp, c] = t2[CHUNK:] + _dot(p["s_rp"], u2) + p["y_loc"]
            s_cur[pp] = s_cur[pp] * p["dec"] + _dot_tn(p["pt2"], u2) + p["s_loc"]

    phases = ([scores, local_terms] + [functools.partial(solve_step, s=s) for s in range(6)]
              + [apply_inverse, recur])
    for slot in range(n_chunks + len(phases) - 1):
        for ph in range(len(phases) - 1, -1, -1):
            if 0 <= slot - ph < n_chunks:
                phases[ph](order[slot - ph])

    y = jnp.concatenate([jnp.concatenate([ys[pp, c] for c in range(n_chunks)], axis=0)
                         for pp in range(n_pairs)], axis=1)
    mean = _dot_exact_rhs(y, head_sum) * (1.0 / RW_HEAD)
    yc = y - mean
    var = _dot_exact_rhs(yc * yc, head_sum) * (1.0 / RW_HEAD)
    yn = yc * lax.rsqrt(var + RW_GN_EPS) * lng_ref[...] + lnb_ref[...]
    out = (yn + bonus) * gate
    if final:
        o_ref[0] = (prev_ref[0] + out).astype(o_ref.dtype)
    else:
        o_ref[0] = out
    for pp in range(n_pairs):
        state[pp] = s_cur[pp]

    @pl.when(i == pl.num_programs(2) - 1)
    def _():
        for pp in range(n_pairs):
            st_ref[0, pp] = s_cur[pp]


RW_PAIRS_PER_STEP = 2
RW_TILE = 2048


def _rwkv_pass(p, prm, d, s0, prev):
    bsz, length, _ = p.shape
    mu, w0, w_up, a0, a_up, g_up, k_k, k_a, r_k, ln_g, ln_b = [t[d] for t in prm]
    final = prev is not None
    rev = d == 1
    t = min(length, RW_TILE)
    nt = length // t
    tidx = (lambda i: nt - 1 - i) if rev else (lambda i: i)
    npp = RW_PAIRS_PER_STEP
    wd = npp * LANES
    c0 = COL_RW // wd
    ng = RW_W // wd
    col = lambda off: pl.BlockSpec((1, t, wd), lambda b, n, i: (b, tidx(i), c0 + off * ng + n))
    vec = pl.BlockSpec((1, wd), lambda b, n, i: (0, n))
    whole = lambda shape: pl.BlockSpec(shape, lambda b, n, i: (0,) * len(shape))
    row = lambda x: x.reshape(1, -1)
    in_specs = [col(0), col(1), col(2),
                pl.BlockSpec((1, t, 256), lambda b, n, i: (b, tidx(i), COL_RW_LORA // 256)),
                vec, vec, vec, whole((1, 256)),
                vec, pl.BlockSpec((64, wd), lambda b, n, i: (0, n)),
                vec, pl.BlockSpec((64, wd), lambda b, n, i: (0, n)),
                pl.BlockSpec((LANES, wd), lambda b, n, i: (0, n)),
                vec, vec, vec, vec, vec,
                pl.BlockSpec((1, npp, LANES, LANES), lambda b, n, i: (b, n, 0, 0))]
    args = [p, p, p, p,
            row(mu[0:512]), row(mu[512:1024]), row(mu[1024:1536]), row(mu[1536:1792]),
            row(w0), w_up.astype(BF16), row(a0), a_up.astype(BF16), g_up.astype(BF16),
            row(k_k), row(k_a), row(r_k), row(ln_g), row(ln_b), s0]
    if final:
        in_specs.append(pl.BlockSpec((1, t, wd), lambda b, n, i: (b, tidx(i), n)))
        args.append(prev)
    out, st = pl.pallas_call(
        functools.partial(_rwkv_kernel, rev=rev, final=final),
        out_shape=[jax.ShapeDtypeStruct((bsz, length, RW_W), BF16 if final else F32),
                   jax.ShapeDtypeStruct((bsz, 4, LANES, LANES), F32)],
        grid=(bsz, ng, nt),
        in_specs=in_specs,
        out_specs=[pl.BlockSpec((1, t, wd), lambda b, n, i: (b, tidx(i), n)),
                   pl.BlockSpec((1, npp, LANES, LANES), lambda b, n, i: (b, n, 0, 0))],
        scratch_shapes=[pltpu.VMEM((npp, LANES, LANES), F32), pltpu.VMEM((1, wd), F32),
                        pltpu.VMEM((1, wd), F32), pltpu.VMEM((1, wd), F32),
                        pltpu.VMEM((1, 256), F32)],
        compiler_params=_cparams(3),
        name="rwkv_bwd" if rev else "rwkv_fwd",
    )(*args)
    return out, st


def _rwkv_mixer(p, prm, s0s):
    bsz = p.shape[0]
    if s0s is None:
        s0s = [jnp.zeros((bsz, 4, LANES, LANES), F32)] * 2
    o_f, s_f = _rwkv_pass(p, prm, 0, s0s[0], None)
    out, s_b = _rwkv_pass(p, prm, 1, s0s[1], o_f)
    return out, [s_f, s_b]


def _dft_table_kernel(ca_ref, sa_ref, cb_ref, sb_ref, c_ref, s_ref):
    cb, sb = cb_ref[...], sb_ref[...]
    for j in range(ca_ref.shape[1]):
        ca, sa = ca_ref[:, j:j + 1], sa_ref[:, j:j + 1]
        c_ref[:, j * LANES:(j + 1) * LANES] = (ca * cb - sa * sb).astype(c_ref.dtype)
        s_ref[:, j * LANES:(j + 1) * LANES] = (-(sa * cb + ca * sb)).astype(s_ref.dtype)


def _dft_tables(length):
    n = 2 * length
    nfp = -(-(length + 1) // LANES) * LANES
    kf = jnp.arange(nfp, dtype=jnp.int32)[:, None]
    s1 = jnp.arange(n // LANES, dtype=jnp.int32)[None, :]
    s0 = jnp.arange(LANES, dtype=jnp.int32)[None, :]
    ang_a = (2.0 * math.pi / n) * ((kf * s1 * LANES) % n).astype(F32)
    ang_b = (2.0 * math.pi / n) * ((kf * s0) % n).astype(F32)
    ok = (kf <= length).astype(F32)
    tm = 384 if nfp % 384 == 0 else LANES
    rows = lambda w: pl.BlockSpec((tm, w), lambda i: (i, 0))
    tab_c, tab_s = pl.pallas_call(
        _dft_table_kernel,
        out_shape=[jax.ShapeDtypeStruct((nfp, n), BF16)] * 2,
        grid=(nfp // tm,),
        in_specs=[rows(n // LANES), rows(n // LANES), rows(LANES), rows(LANES)],
        out_specs=[rows(n), rows(n)],
        compiler_params=_cparams(1),
        name="hyena_dft_tables",
    )(jnp.cos(ang_a) * ok, jnp.sin(ang_a) * ok, jnp.cos(ang_b), jnp.sin(ang_b))
    kk = kf[:, 0]
    wk = jnp.where((kk == 0) | (kk == length), 1.0, 2.0) * (kk <= length) / n
    return tab_c, tab_s, jnp.broadcast_to(wk[:, None], (nfp, LANES)).astype(F32)


def _hy_filter_kernel(w1_ref, b1_ref, w2_ref, b2_ref, w3_ref, fq_ref, dl_ref, hf_ref, hb_ref, nrm_ref, *, length):
    i = pl.program_id(0)
    tm = hf_ref.shape[0]
    pos = (lax.broadcasted_iota(jnp.int32, (tm, LANES), 0) + i * tm).astype(F32)
    lane = lax.broadcasted_iota(jnp.int32, (tm, LANES), 1)
    z = pos / max(length - 1, 1)
    band_idx = jnp.where(lane <= HY_BANDS, lane - 1, lane - 1 - HY_BANDS).astype(F32)
    band = 1e-4 + band_idx * ((HY_BANDS - 1 - 1e-4) / (HY_BANDS - 1))
    ang = (2.0 * math.pi / length) * pos * band
    feat = jnp.where(lane == 0, z, jnp.where(lane <= HY_BANDS, jnp.cos(ang),
                                             jnp.where(lane <= 2 * HY_BANDS, -jnp.sin(ang), 0.0)))
    h = jnp.sin(fq_ref[0:1, :] * (_dot_hp(feat, w1_ref[...]) + b1_ref[...]))
    h = jnp.sin(fq_ref[1:2, :] * (_dot_hp(h, w2_ref[...]) + b2_ref[...]))
    h = _dot_x3(h, w3_ref[...]) * jnp.exp(-z[:, 0:1] * dl_ref[...])
    h_fwd = h[:, :HY_W]
    h_bwd = jnp.where(pos[:, 0:1] > 0.0, h[:, HY_W:], 0.0)
    hf_ref[...] = h_fwd
    hb_ref[...] = h_bwd

    @pl.when(i == 0)
    def _():
        nrm_ref[...] = jnp.zeros_like(nrm_ref)

    nrm_ref[...] += jnp.sum(jnp.abs(h_fwd) + jnp.abs(h_bwd), axis=0, keepdims=True)


def _hyena_filter(length, w1, b1, w2, b2, w3, freq):
    tm = min(length, 512)
    nfilt = w1.shape[1]
    w1p = jnp.zeros((LANES, nfilt), F32).at[:HY_EMB].set(w1)
    deltas = jnp.abs(jnp.linspace(math.log(1e-2) / 1.5, math.log(1e-2) / 0.3, HY_W, dtype=F32))
    whole = lambda shape: pl.BlockSpec(shape, lambda i: (0,) * len(shape))
    h_fwd, h_bwd, norm = pl.pallas_call(
        functools.partial(_hy_filter_kernel, length=length),
        out_shape=[jax.ShapeDtypeStruct((length, HY_W), F32), jax.ShapeDtypeStruct((length, HY_W), F32),
                   jax.ShapeDtypeStruct((1, HY_W), F32)],
        grid=(length // tm,),
        in_specs=[whole((LANES, nfilt)), whole((1, nfilt)), whole((nfilt, nfilt)), whole((1, nfilt)),
                  whole((nfilt, 2 * HY_W)), whole((2, nfilt)), whole((1, 2 * HY_W))],
        out_specs=[pl.BlockSpec((tm, HY_W), lambda i: (i, 0)), pl.BlockSpec((tm, HY_W), lambda i: (i, 0)),
                   whole((1, HY_W))],
        compiler_params=_cparams(1),
        name="hyena_filter_mlp",
    )(w1p, b1.reshape(1, nfilt), w2, b2.reshape(1, nfilt), w3, freq, jnp.tile(deltas, 2).reshape(1, 2 * HY_W))
    return (h_fwd + h_bwd) / norm, (h_fwd - h_bwd) / norm


def _hy_zin_kernel(v_ref, x1_ref, cw_ref, o_ref, *, row_len):
    vc = _short_conv(v_ref[0], cw_ref[:, 0:512], row_len, 1)
    x1c = _short_conv(x1_ref[0], cw_ref[:, 1024:1536], row_len, 1)
    o_ref[0] = (x1c * vc).astype(o_ref.dtype)


def _hy_zin(p, conv_w, row_len):
    bsz, length, _ = p.shape
    t = min(length, 512)
    return pl.pallas_call(
        functools.partial(_hy_zin_kernel, row_len=row_len),
        out_shape=jax.ShapeDtypeStruct((bsz, length, HY_W), BF16),
        grid=(bsz, length // t),
        in_specs=[pl.BlockSpec((1, t, 512), lambda b, i: (b, i, 0)),
                  pl.BlockSpec((1, t, 512), lambda b, i: (b, i, 2)),
                  pl.BlockSpec((3, 1536), lambda b, i: (0, 0))],
        out_specs=pl.BlockSpec((1, t, 512), lambda b, i: (b, i, 0)),
        compiler_params=_cparams(2),
        name="hyena_zin",
    )(p, p, conv_w)


def _dft_fwd_kernel(*refs, mult):
    if mult:
        c_ref, s_ref, z_ref, fr_ref, fi_ref, yr_ref, yi_ref = refs
        z_cos = z_sin = z_ref[0].astype(BF16)
    else:
        c_ref, s_ref, zc_ref, zs_ref, wk_ref, yr_ref, yi_ref = refs
        z_cos, z_sin = zc_ref[0].astype(BF16), zs_ref[0].astype(BF16)
    zr = jnp.dot(c_ref[...], z_cos, preferred_element_type=F32)
    zi = jnp.dot(s_ref[...], z_sin, preferred_element_type=F32)
    if mult:
        fr, fi = fr_ref[...], fi_ref[...]
        zr, zi = zr * fr - zi * fi, zr * fi + zi * fr
    else:
        zr, zi = zr * wk_ref[:, 0:1], zi * wk_ref[:, 0:1]
    yr_ref[0] = zr.astype(yr_ref.dtype)
    yi_ref[0] = zi.astype(yi_ref.dtype)


def _dft_fwd(tab_c, tab_s, z, spec, wk=None):
    mult = spec is not None
    zs = (z,) if mult else z
    bsz, klen, _ = zs[0].shape
    nfp = tab_c.shape[0]
    tm = 384 if nfp % 384 == 0 else LANES
    in_specs = [pl.BlockSpec((tm, klen), lambda b, i: (i, 0)),
                pl.BlockSpec((tm, klen), lambda b, i: (i, 0))]
    in_specs += [pl.BlockSpec((1, klen, 512), lambda b, i: (b, 0, 0))] * len(zs)
    args = [tab_c, tab_s, *zs]
    if mult:
        in_specs += [pl.BlockSpec((tm, 512), lambda b, i: (i, 0))] * 2
        args += list(spec)
    else:
        in_specs.append(pl.BlockSpec((tm, LANES), lambda b, i: (i, 0)))
        args.append(wk)
    odt = BF16 if mult else F32
    return pl.pallas_call(
        functools.partial(_dft_fwd_kernel, mult=mult),
        out_shape=[jax.ShapeDtypeStruct((bsz, nfp, 512), odt)] * 2,
        grid=(bsz, nfp // tm),
        in_specs=in_specs,
        out_specs=[pl.BlockSpec((1, tm, 512), lambda b, i: (b, i, 0))] * 2,
        compiler_params=_cparams(2),
        name="hyena_dft_mul" if mult else "hyena_dft_filter",
    )(*args)


def _dft_inv_kernel(ci_ref, si_ref, yr_ref, yi_ref, v_ref, x0_ref, x1_ref, cw_ref, skip_ref, o_ref, *, row_len):
    y = jnp.dot(ci_ref[...], yr_ref[0], preferred_element_type=F32)
    y += jnp.dot(si_ref[...], yi_ref[0], preferred_element_type=F32)
    vc = _short_conv(v_ref[0], cw_ref[:, 0:512], row_len, 1)
    x0c = _short_conv(x0_ref[0], cw_ref[:, 512:1024], row_len, 1)
    x1c = _short_conv(x1_ref[0], cw_ref[:, 1024:1536], row_len, 1)
    zin = x1c * vc
    o_ref[0] = (x0c * (y + zin * skip_ref[...])).astype(o_ref.dtype)


def _dft_inv(tab_c, tab_s, yr, yi, p, conv_w, skip, row_len):
    bsz, length, _ = p.shape
    nfp = tab_c.shape[0]
    t = min(length, 512)
    pcol = lambda c: pl.BlockSpec((1, t, 512), lambda b, i: (b, i, c))
    return pl.pallas_call(
        functools.partial(_dft_inv_kernel, row_len=row_len),
        out_shape=jax.ShapeDtypeStruct((bsz, length, HY_W), BF16),
        grid=(bsz, length // t),
        in_specs=[pl.BlockSpec((t, nfp), lambda b, i: (i, 0)),
                  pl.BlockSpec((t, nfp), lambda b, i: (i, 0)),
                  pl.BlockSpec((1, nfp, 512), lambda b, i: (b, 0, 0)),
                  pl.BlockSpec((1, nfp, 512), lambda b, i: (b, 0, 0)),
                  pcol(0), pcol(1), pcol(2),
                  pl.BlockSpec((3, 1536), lambda b, i: (0, 0)),
                  pl.BlockSpec((1, 512), lambda b, i: (0, 0))],
        out_specs=pl.BlockSpec((1, t, 512), lambda b, i: (b, i, 0)),
        compiler_params=_cparams(2),
        name="hyena_idft_gate",
    )(tab_c, tab_s, yr, yi, p, p, p, conv_w, skip.reshape(1, HY_W))


def _hyena_mixer(p, prm, tables, row_len):
    conv_w, w1, b1, w2, b2, w3, freq, skip = prm
    length = p.shape[1]
    tab_c, tab_s, wk = tables
    f_cos, f_sin = _hyena_filter(length, w1, b1, w2, b2, w3, freq)
    spec = _dft_fwd(tab_c, tab_s, (f_cos[None], f_sin[None]), None, wk)
    spec = (spec[0][0], spec[1][0])
    zin = _hy_zin(p, conv_w, row_len)
    yr, yi = _dft_fwd(tab_c, tab_s, zin, spec)
    return _dft_inv(tab_c, tab_s, yr, yi, p, conv_w, skip, row_len)


def _permute_w_in(w_in):
    w_in = w_in.astype(BF16)
    hy = w_in[:, 0:1536]
    rw = w_in[:, 1536:3328]
    gd = w_in[:, 3328:5392]
    rg = w_in[:, 5392:6416]
    pad = jnp.zeros((w_in.shape[0], N_PROJ - 6416), BF16)
    return jnp.concatenate([hy, rw[:, :1536], gd[:, :2048], rg, rw[:, 1536:], gd[:, 2048:], pad], axis=-1)


def kernel(x, c, ctx, c_ctx, ada_w, ada_b, norm_mix_g, norm_mlp_g, w_in, w_out,
           hy_conv, hy_w1, hy_b1, hy_w2, hy_b2, hy_w3, hy_freq, hy_skip,
           rw_mu, rw_w0, rw_w_up, rw_a0, rw_a_up, rw_g_up, rw_k_k, rw_k_a, rw_r_k, rw_ln_g, rw_ln_b,
           gd_conv, gd_a_log, gd_dt_bias, gd_norm_g,
           rg_conv, rg_conv_b, rg_wa, rg_ba, rg_wx, rg_bx, rg_lambda,
           mlp_w1, mlp_w2, final_norm_g):
    bsz, seq, _ = x.shape
    ctx_len = ctx.shape[1]
    depth = ada_w.shape[0]
    tables_x = _dft_tables(seq)
    tables_c = _dft_tables(ctx_len)
    cond8 = jnp.concatenate([c, c_ctx[None, :], jnp.zeros((8 - bsz - 1, D_MODEL), F32)], axis=0)
    for l in range(depth):
        last = l == depth - 1
        mod = _modulation(cond8, ada_w, ada_b[l], l)
        mod_x = mod[:bsz].reshape(bsz, 6, D_MODEL)
        mod_c = mod[bsz:bsz + 1].reshape(1, 6, D_MODEL)
        w_in_bf = _permute_w_in(w_in[l])
        w_out_bf = w_out[l].astype(BF16)
        w1_bf = mlp_w1[l].astype(BF16)
        w2_bf = mlp_w2[l].astype(BF16)
        px = _inproj(x, norm_mix_g[l], mod_x, w_in_bf)
        flat = lambda t: t.reshape(1, bsz * ctx_len, t.shape[-1])
        pc = _inproj(flat(ctx), norm_mix_g[l], mod_c, w_in_bf).reshape(bsz, ctx_len, N_PROJ)
        hy_prm = (hy_conv[l], hy_w1[l], hy_b1[l], hy_w2[l], hy_b2[l], hy_w3[l], hy_freq[l], hy_skip[l])
        rw_prm = (rw_mu[l], rw_w0[l], rw_w_up[l], rw_a0[l], rw_a_up[l], rw_g_up[l],
                  rw_k_k[l], rw_k_a[l], rw_r_k[l], rw_ln_g[l], rw_ln_b[l])
        gd_prm = (gd_conv[l], gd_a_log[l], gd_dt_bias[l], gd_norm_g[l])
        rg_prm = (rg_conv[l], rg_conv_b[l], rg_wa[l], rg_ba[l], rg_wx[l], rg_bx[l], rg_lambda[l])
        c_rw, s_rw = _rwkv_mixer(pc, rw_prm, None)
        c_gd, s_gd = _gdn_mixer(pc, gd_prm, None, ctx_len)
        c_rg, s_rg = _rglru_mixer(pc, rg_prm, None, ctx_len)
        x_hy = _hyena_mixer(px, hy_prm, tables_x, GRID_W)
        x_rw, _ = _rwkv_mixer(px, rw_prm, s_rw)
        x_gd, _ = _gdn_mixer(px, gd_prm, s_gd, GRID_W)
        x_rg, _ = _rglru_mixer(px, rg_prm, s_rg, GRID_W)
        x_new = _outproj(x, mod_x, (x_hy, x_rw, x_gd, x_rg), w_out_bf)
        x_new = _mlp(x_new, norm_mlp_g[l], mod_x, w1_bf, w2_bf, final_norm_g if last else None)
        if not last:
            c_hy = _hyena_mixer(pc, hy_prm, tables_c, ctx_len)
            ctx_new = _outproj(flat(ctx), mod_c, tuple(flat(m) for m in (c_hy, c_rw, c_gd, c_rg)), w_out_bf)
            ctx = _mlp(ctx_new, norm_mlp_g[l], mod_c, w1_bf, w2_bf, None).reshape(bsz, ctx_len, D_MODEL)
        x = x_new
    return x
```
